```python
import jax, jax.numpy as jnp
from jax import lax
import numpy as np

D_MODEL = 1024
BATCH = 8
SEQ = 2048
DEPTH = 4

GRID_W = 64
CTX_LEN = 256
HEAD_DIM = 64
N_Q_HEADS = 12
N_KV_HEADS = 4
Q_PER_KV = N_Q_HEADS // N_KV_HEADS
WINDOW = 128
BLOCK = 128
ROPE_BASE = 10000.0
N_FOURIER_GROUPS = 4
FOURIER_GROUP_DIM = 64
FOURIER_WIDTH = N_FOURIER_GROUPS * FOURIER_GROUP_DIM
ATTN_WIDTH = N_Q_HEADS * HEAD_DIM
KV_WIDTH = N_KV_HEADS * HEAD_DIM
Q_END = ATTN_WIDTH
K_END = Q_END + KV_WIDTH
V_END = K_END + KV_WIDTH
EVEN_IN_WIDTH = V_END + FOURIER_WIDTH
EVEN_MIX_WIDTH = ATTN_WIDTH + FOURIER_WIDTH
CONV_WIDTH = D_MODEL
CONV_K = 3
N_EXPERTS = 32
TOP_K = 4
D_EXPERT = D_MODEL
EXPERT_BLOCK = 128
SWIGLU_LIMIT = 7.0
SWIGLU_ALPHA = 1.702
LN_EPS = 1e-5
NEG_INF = -1e30
DEEPNORM_ALPHA = (2 * DEPTH) ** 0.25
DEEPNORM_BETA = (8 * DEPTH) ** -0.25
N_EVEN = (DEPTH + 1) // 2
N_ODD = DEPTH // 2

kernel_name = "hybrid_swa_fnet_shortconv_moe_dit"


def layer_norm(x, g, b):
    xf = x.astype(jnp.float32)
    mu = xf.mean(-1, keepdims=True)
    var = jnp.square(xf - mu).mean(-1, keepdims=True)
    return ((xf - mu) * lax.rsqrt(var + LN_EPS) * g.astype(jnp.float32) + b.astype(jnp.float32)).astype(x.dtype)


def axial_rope_tables(seq_len, dtype):
    rows = seq_len // GRID_W
    row = jnp.repeat(jnp.arange(rows), GRID_W).astype(jnp.float32)
    col = jnp.tile(jnp.arange(GRID_W), rows).astype(jnp.float32)
    n_freq = HEAD_DIM // 4
    inv = ROPE_BASE ** (-jnp.arange(n_freq, dtype=jnp.float32) / n_freq)
    ang = jnp.concatenate([row[:, None] * inv, col[:, None] * inv], -1)
    ang = jnp.concatenate([ang, ang], -1)
    return jnp.cos(ang).astype(dtype), jnp.sin(ang).astype(dtype)


def apply_rope(t, cos, sin):
    t1, t2 = jnp.split(t, 2, axis=-1)
    rot = jnp.concatenate([-t2, t1], -1)
    return t * cos[None, :, None, :] + rot * sin[None, :, None, :]


def sink_softmax(scores, sink):
    sk = sink.astype(jnp.float32).reshape(N_KV_HEADS, Q_PER_KV, 1, 1)
    m = sk
    for s in scores:
        m = jnp.maximum(m, s.max(axis=-1, keepdims=True))
    ex = [jnp.exp(s - m) for s in scores]
    denom = jnp.exp(sk - m)
    for e in ex:
        denom = denom + e.sum(axis=-1, keepdims=True)
    return [e / denom for e in ex]


def windowed_attention(q, k, v, kc, vc, sink):
    B, S = q.shape[0], q.shape[1]
    nb = S // BLOCK
    scale = HEAD_DIM ** -0.5
    qb = q.reshape(B, nb, BLOCK, N_KV_HEADS, Q_PER_KV, HEAD_DIM)

    def band(t):
        tp = jnp.pad(t, ((0, 0), (BLOCK, BLOCK), (0, 0), (0, 0))).reshape(B, nb + 2, BLOCK, N_KV_HEADS, HEAD_DIM)
        return jnp.concatenate([tp[:, :-2], tp[:, 1:-1], tp[:, 2:]], axis=2)

    kw, vw = band(k), band(v)
    s_loc = jnp.einsum('bnqkgd,bnjkd->bnkgqj', qb, kw).astype(jnp.float32) * scale
    qpos = jnp.arange(nb)[:, None, None] * BLOCK + jnp.arange(BLOCK)[None, :, None]
    kpos = jnp.arange(nb)[:, None, None] * BLOCK - BLOCK + jnp.arange(3 * BLOCK)[None, None, :]
    valid = (jnp.abs(qpos - kpos) <= WINDOW) & (kpos >= 0) & (kpos < S)
    s_loc = jnp.where(valid[None, :, None, None], s_loc, NEG_INF)
    s_ctx = jnp.einsum('bnqkgd,bckd->bnkgqc', qb, kc).astype(jnp.float32) * scale
    p_loc, p_ctx = sink_softmax([s_loc, s_ctx], sink)
    o = (jnp.einsum('bnkgqj,bnjkd->bnqkgd', p_loc.astype(vw.dtype), vw)
         + jnp.einsum('bnkgqc,bckd->bnqkgd', p_ctx.astype(vc.dtype), vc))
    return o.reshape(B, S, ATTN_WIDTH)


def context_attention(qc, kc, vc, sink):
    B, L = qc.shape[0], qc.shape[1]
    qg = qc.reshape(B, L, N_KV_HEADS, Q_PER_KV, HEAD_DIM)
    s = jnp.einsum('bqkgd,bckd->bkgqc', qg, kc).astype(jnp.float32) * (HEAD_DIM ** -0.5)
    (p,) = sink_softmax([s], sink)
    o = jnp.einsum('bkgqc,bckd->bqkgd', p.astype(vc.dtype), vc)
    return o.reshape(B, L, ATTN_WIDTH)


def fourier_mix(f):
    B, N = f.shape[0], f.shape[1]
    fg = f.astype(jnp.float32).reshape(B, N, N_FOURIER_GROUPS, FOURIER_GROUP_DIM)
    y = jnp.fft.fft2(fg, axes=(1, 3), norm='ortho').real
    return y.astype(f.dtype).reshape(B, N, FOURIER_WIDTH)


def even_mixer(u, uc, w_in, sink, w_out, cos, sin, ctx_out):
    B, S = u.shape[0], u.shape[1]
    L = uc.shape[1]
    h = u @ w_in
    q = apply_rope(h[..., :Q_END].reshape(B, S, N_Q_HEADS, HEAD_DIM), cos, sin)
    k = apply_rope(h[..., Q_END:K_END].reshape(B, S, N_KV_HEADS, HEAD_DIM), cos, sin)
    v = h[..., K_END:V_END].reshape(B, S, N_KV_HEADS, HEAD_DIM)
    if ctx_out:
        hc = uc @ w_in
        kvc = hc[..., Q_END:V_END]
    else:
        kvc = uc @ w_in[:, Q_END:V_END]
    kc = kvc[..., :KV_WIDTH].reshape(B, L, N_KV_HEADS, HEAD_DIM)
    vc = kvc[..., KV_WIDTH:].reshape(B, L, N_KV_HEADS, HEAD_DIM)
    y = jnp.concatenate([windowed_attention(q, k, v, kc, vc, sink), fourier_mix(h[..., V_END:])], -1) @ w_out
    if not ctx_out:
        return y, None
    qc = hc[..., :Q_END].reshape(B, L, N_Q_HEADS, HEAD_DIM)
    yc = jnp.concatenate([context_attention(qc, kc, vc, sink), fourier_mix(hc[..., V_END:])], -1) @ w_out
    return y, yc


def short_conv(h, w):
    return lax.conv_general_dilated(h, w[:, None, :].astype(h.dtype), window_strides=(1,), padding=((1, 1),),
                                    dimension_numbers=('NWC', 'WIO', 'NWC'), feature_group_count=CONV_WIDTH)


def gated_conv(u, w_in, conv_w, w_out):
    h = u @ w_in
    b_gate, c_gate, hh = jnp.split(h, 3, axis=-1)
    return (b_gate * short_conv(c_gate * hh, conv_w)) @ w_out


def moe(t, w_r, b_r, w_up, b_up, w_down, b_down):
    T, D = t.shape
    logits = (t @ w_r).astype(jnp.float32) + b_r.astype(jnp.float32)
    top_v, top_i = lax.top_k(logits, TOP_K)
    gates = jax.nn.softmax(top_v, axis=-1).astype(t.dtype)
    n = T * TOP_K
    flat_e = top_i.reshape(-1)
    order = jnp.argsort(flat_e)
    e_sorted = flat_e[order]
    tok_sorted = order // TOP_K
    sizes = jnp.bincount(flat_e, length=N_EXPERTS)
    padded = (sizes + EXPERT_BLOCK - 1) // EXPERT_BLOCK * EXPERT_BLOCK
    pad_ends = jnp.cumsum(padded)
    pad_starts = pad_ends - padded
    starts = jnp.cumsum(sizes) - sizes
    dest = pad_starts[e_sorted] + (jnp.arange(n) - starts[e_sorted])
    n_blk = (n + N_EXPERTS * (EXPERT_BLOCK - 1) + EXPERT_BLOCK - 1) // EXPERT_BLOCK
    xbuf = jnp.zeros((n_blk * EXPERT_BLOCK, D), t.dtype).at[dest].set(t[tok_sorted])
    blk_e = jnp.minimum(jnp.searchsorted(pad_ends, jnp.arange(n_blk) * EXPERT_BLOCK, side='right'), N_EXPERTS - 1)

    def expert_block(args):
        xb, e = args
        h = xb @ w_up[e] + b_up[e]
        glu = jnp.minimum(h[:, ::2], SWIGLU_LIMIT)
        lin = jnp.clip(h[:, 1::2], -SWIGLU_LIMIT, SWIGLU_LIMIT)
        a = glu * jax.nn.sigmoid(SWIGLU_ALPHA * glu) * (lin + 1)
        return a @ w_down[e] + b_down[e]

    out = lax.map(expert_block, (xbuf.reshape(n_blk, EXPERT_BLOCK, D), blk_e)).reshape(n_blk * EXPERT_BLOCK, D)
    ys = out[dest] * gates.reshape(-1)[order][:, None]
    return jax.ops.segment_sum(ys, tok_sorted, num_segments=T)


def setup_inputs(seed: int = 0) -> dict:
    key = jax.random.key(seed)
    ks = jax.random.split(key, 20)
    D = D_MODEL

    def nrm(k, shape, s):
        return jax.random.normal(k, shape, jnp.float32) * s

    return {
        "x": nrm(ks[0], (BATCH, SEQ, D), 1.0),
        "c": nrm(ks[1], (BATCH, D), 1.0),
        "ctx": nrm(ks[2], (BATCH, CTX_LEN, D), 1.0),
        "c_ctx": nrm(ks[3], (D,), 1.0),
        "w_mod": nrm(ks[4], (DEPTH, D, 6 * D), 0.5 * D ** -0.5),
        "b_mod": nrm(ks[5], (DEPTH, 6 * D), 0.02),
        "w_in_even": nrm(ks[6], (N_EVEN, D, EVEN_IN_WIDTH), D ** -0.5),
        "sink": nrm(ks[7], (N_EVEN, N_Q_HEADS), 0.5),
        "w_out_even": nrm(ks[8], (N_EVEN, EVEN_MIX_WIDTH, D), EVEN_MIX_WIDTH ** -0.5 * DEEPNORM_BETA),
        "w_in_odd": nrm(ks[9], (N_ODD, D, 3 * CONV_WIDTH), D ** -0.5),
        "conv_w": nrm(ks[10], (N_ODD, CONV_K, CONV_WIDTH), CONV_K ** -0.5),
        "w_out_odd": nrm(ks[11], (N_ODD, CONV_WIDTH, D), CONV_WIDTH ** -0.5 * DEEPNORM_BETA),
        "ln_g": 1.0 + nrm(ks[12], (DEPTH, 2, D), 0.02),
        "ln_b": nrm(ks[13], (DEPTH, 2, D), 0.02),
        "w_router": nrm(ks[14], (DEPTH, D, N_EXPERTS), D ** -0.5),
        "b_router": nrm(ks[15], (DEPTH, N_EXPERTS), 0.01),
        "w_up": nrm(ks[16], (DEPTH, N_EXPERTS, D, 2 * D_EXPERT), D ** -0.5),
        "b_up": nrm(ks[17], (DEPTH, N_EXPERTS, 2 * D_EXPERT), 0.02),
        "w_down": nrm(ks[18], (DEPTH, N_EXPERTS, D_EXPERT, D), D_EXPERT ** -0.5 * DEEPNORM_BETA),
        "b_down": nrm(ks[19], (DEPTH, N_EXPERTS, D), 0.02),
    }


def reference(x, c, ctx, c_ctx, w_mod, b_mod, w_in_even, sink, w_out_even, w_in_odd, conv_w, w_out_odd,
              ln_g, ln_b, w_router, b_router, w_up, b_up, w_down, b_down):
    B, S, D = x.shape
    L = ctx.shape[1]
    cos, sin = axial_rope_tables(S, x.dtype)
    silu_c = jax.nn.silu(c)
    silu_cc = jax.nn.silu(c_ctx)
    xc = ctx
    for l in range(DEPTH):
        even = l % 2 == 0
        j = l // 2
        ctx_after = any(m % 2 == 0 for m in range(l + 1, DEPTH))
        sh1, sc1, g1, sh2, sc2, g2 = jnp.split((silu_c @ w_mod[l] + b_mod[l])[:, None, :], 6, axis=-1)
        sh1c, sc1c, g1c, sh2c, sc2c, g2c = jnp.split(silu_cc @ w_mod[l] + b_mod[l], 6, axis=-1)
        u = x * (1 + sc1) + sh1
        if even:
            uc = xc * (1 + sc1c) + sh1c
            y, yc = even_mixer(u, uc, w_in_even[j], sink[j], w_out_even[j], cos, sin, ctx_after)
        else:
            y = gated_conv(u, w_in_odd[j], conv_w[j], w_out_odd[j])
            if ctx_after:
                uc = xc * (1 + sc1c) + sh1c
                yc = gated_conv(uc, w_in_odd[j], conv_w[j], w_out_odd[j])
        x = layer_norm(DEEPNORM_ALPHA * x + g1 * y, ln_g[l, 0], ln_b[l, 0])
        v = x * (1 + sc2) + sh2
        if ctx_after:
            xc = layer_norm(DEEPNORM_ALPHA * xc + g1c * yc, ln_g[l, 0], ln_b[l, 0])
            vc = xc * (1 + sc2c) + sh2c
            f_all = moe(jnp.concatenate([v.reshape(B * S, D), vc.reshape(B * L, D)], 0),
                        w_router[l], b_router[l], w_up[l], b_up[l], w_down[l], b_down[l])
            f = f_all[:B * S].reshape(B, S, D)
            fc = f_all[B * S:].reshape(B, L, D)
            xc = layer_norm(DEEPNORM_ALPHA * xc + g2c * fc, ln_g[l, 1], ln_b[l, 1])
        else:
            f = moe(v.reshape(B * S, D), w_router[l], b_router[l], w_up[l], b_up[l], w_down[l], b_down[l]).reshape(B, S, D)
        x = layer_norm(DEEPNORM_ALPHA * x + g2 * f, ln_g[l, 1], ln_b[l, 1])
    return x
```

```python
import functools

import numpy as np
import jax
import jax.numpy as jnp
from jax import lax
from jax.experimental import pallas as pl
from jax.experimental.pallas import tpu as pltpu

D_MODEL = 1024
DEPTH = 4
GRID_W = 64
HEAD_DIM = 64
N_Q_HEADS = 12
N_KV_HEADS = 4
Q_PER_KV = N_Q_HEADS // N_KV_HEADS
BLOCK = 128
ROPE_BASE = 10000.0
FOURIER_GROUP_DIM = 64
FOURIER_WIDTH = 256
ATTN_WIDTH = N_Q_HEADS * HEAD_DIM
KV_WIDTH = N_KV_HEADS * HEAD_DIM
EVEN_IN_WIDTH = ATTN_WIDTH + 2 * KV_WIDTH + FOURIER_WIDTH
N_EXPERTS = 32
TOP_K = 4
D_EXPERT = D_MODEL
SWIGLU_LIMIT = 7.0
SWIGLU_ALPHA = 1.702
LN_EPS = 1e-5
NEG_INF = -1e30
DEEPNORM_ALPHA = (2 * DEPTH) ** 0.25

LANES = 128
ROW_TILE = 512
EXPERT_ROWS = 256
MOD_ROWS = 16
VMEM_LIMIT = 48 * 1024 * 1024

F32 = jnp.float32
BF16 = jnp.bfloat16


def _params(*sem):
    return pltpu.CompilerParams(dimension_semantics=sem, vmem_limit_bytes=VMEM_LIMIT)


def _mod_spec(tm, seq_rows, mod_row):
    if mod_row is None:
        return pl.BlockSpec((1, 1, D_MODEL), lambda i: ((i * tm) // seq_rows, 0, 0))
    return pl.BlockSpec((1, 1, D_MODEL), lambda i: (mod_row, 0, 0))


def _mod_kernel(c_ref, w_ref, b_ref, o_ref):
    c = c_ref[...]
    s = (c * jax.nn.sigmoid(c)).astype(BF16)
    o_ref[0] = jnp.dot(s, w_ref[0].astype(BF16), preferred_element_type=F32) + b_ref[0]


def _modulation(cond, w_mod, b_mod):
    tn = 1536
    n = w_mod.shape[-1]
    return pl.pallas_call(
        _mod_kernel,
        grid=(DEPTH, n // tn),
        in_specs=[
            pl.BlockSpec((MOD_ROWS, D_MODEL), lambda l, j: (0, 0)),
            pl.BlockSpec((1, D_MODEL, tn), lambda l, j: (l, 0, j)),
            pl.BlockSpec((1, 1, tn), lambda l, j: (l, 0, j)),
        ],
        out_specs=pl.BlockSpec((1, MOD_ROWS, tn), lambda l, j: (l, 0, j)),
        out_shape=jax.ShapeDtypeStruct((DEPTH, MOD_ROWS, n), F32),
        compiler_params=_params("arbitrary", "arbitrary"),
        name="modulation",
    )(cond, w_mod, b_mod.reshape(DEPTH, 1, n))


def _rope_chunk(h, cos, sin_signed, first_half):
    swapped = jnp.where(first_half, pltpu.roll(h, LANES - HEAD_DIM // 2, axis=1), pltpu.roll(h, HEAD_DIM // 2, axis=1))
    return h * cos + swapped * sin_signed


def _inproj_even_kernel(x_ref, sc_ref, sh_ref, w_ref, cos_ref, sin_ref, o_ref, *, rope):
    u = (x_ref[...] * (1.0 + sc_ref[0]) + sh_ref[0]).astype(BF16)
    tn = 512
    if rope:
        cos = cos_ref[...]
        sin = sin_ref[...]
        lane = lax.broadcasted_iota(jnp.int32, cos.shape, 1)
        first_half = (lane % HEAD_DIM) < HEAD_DIM // 2
    scale = HEAD_DIM ** -0.5
    for jt in range(EVEN_IN_WIDTH // tn):
        acc = jnp.dot(u, w_ref[:, jt * tn:(jt + 1) * tn], preferred_element_type=F32)
        for k in range(tn // LANES):
            col = jt * tn + k * LANES
            h = acc[:, k * LANES:(k + 1) * LANES]
            if rope and col < ATTN_WIDTH + KV_WIDTH:
                h = _rope_chunk(h, cos, sin, first_half)
            if col < ATTN_WIDTH:
                h = h * scale
            o_ref[:, col:col + LANES] = h.astype(BF16)


def _inproj_even(x2, sc, sh, w, cos, sin, seq_rows, mod_row, rope):
    t = x2.shape[0]
    tm = min(ROW_TILE, seq_rows)
    pos_tiles = cos.shape[0] // tm
    mod = _mod_spec(tm, seq_rows, mod_row)

    return pl.pallas_call(
        functools.partial(_inproj_even_kernel, rope=rope),
        grid=(t // tm,),
        in_specs=[
            pl.BlockSpec((tm, D_MODEL), lambda i: (i, 0)),
            mod,
            mod,
            pl.BlockSpec((D_MODEL, EVEN_IN_WIDTH), lambda i: (0, 0)),
            pl.BlockSpec((tm, LANES), lambda i: (i % pos_tiles, 0)),
            pl.BlockSpec((tm, LANES), lambda i: (i % pos_tiles, 0)),
        ],
        out_specs=pl.BlockSpec((tm, EVEN_IN_WIDTH), lambda i: (i, 0)),
        out_shape=jax.ShapeDtypeStruct((t, EVEN_IN_WIDTH), BF16),
        compiler_params=_params("arbitrary"),
        name="inproj_even",
    )(x2, sc, sh, w, cos, sin)


def _attend(q_ref, k_all, v_all, valid, sink_ref, o_ref):
    rows = q_ref.shape[0]
    lane = lax.broadcasted_iota(jnp.int32, (rows, LANES), 1)
    low = lane < HEAD_DIM
    for kv in range(N_KV_HEADS):
        kv_chunk, kv_half = divmod(kv, 2)
        keep = low if kv_half == 0 else jnp.logical_not(low)
        stack, sinks = [], []
        for g in range(Q_PER_KV):
            head = kv * Q_PER_KV + g
            chunk, half = divmod(head, 2)
            qh = q_ref[:, chunk * LANES:(chunk + 1) * LANES].astype(F32)
            if half != kv_half:
                qh = pltpu.roll(qh, HEAD_DIM, axis=1)
            stack.append(jnp.where(keep, qh, 0.0).astype(BF16))
            sinks.append(jnp.full((rows, 1), sink_ref[head], F32))
        qs = jnp.concatenate(stack, axis=0)
        sk = jnp.concatenate(sinks, axis=0)
        kc = k_all[:, kv_chunk * LANES:(kv_chunk + 1) * LANES]
        vc = v_all[:, kv_chunk * LANES:(kv_chunk + 1) * LANES]
        s = lax.dot_general(qs, kc, (((1,), (1,)), ((), ())), preferred_element_type=F32)
        if valid is not None:
            s = jnp.where(valid, s, NEG_INF)
        m = jnp.maximum(sk, jnp.max(s, axis=-1, keepdims=True))
        e = jnp.exp(s - m)
        denom = jnp.exp(sk - m) + jnp.sum(e, axis=-1, keepdims=True)
        o = jnp.dot(e.astype(BF16), vc, preferred_element_type=F32) / denom
        for g in range(Q_PER_KV):
            head = kv * Q_PER_KV + g
            half = head % 2
            oh = o[g * rows:(g + 1) * rows]
            if half != kv_half:
                oh = pltpu.roll(oh, HEAD_DIM, axis=1)
            o_ref[:, head * HEAD_DIM:(head + 1) * HEAD_DIM] = oh[:, half * HEAD_DIM:(half + 1) * HEAD_DIM].astype(BF16)


def _win_attn_kernel(sink_ref, q_ref, kp_ref, kc_ref, kn_ref, vp_ref, vc_ref, vn_ref, kx_ref, vx_ref, o_ref, *, nb):
    n = pl.program_id(1)
    k_all = jnp.concatenate([kp_ref[...], kc_ref[...], kn_ref[...], kx_ref[...]], axis=0)
    v_all = jnp.concatenate([vp_ref[...], vc_ref[...], vn_ref[...], vx_ref[...]], axis=0)
    n_keys = k_all.shape[0]
    r = lax.broadcasted_iota(jnp.int32, (Q_PER_KV * BLOCK, n_keys), 0) % BLOCK
    c = lax.broadcasted_iota(jnp.int32, (Q_PER_KV * BLOCK, n_keys), 1)
    lo = jnp.where(n > 0, r, BLOCK)
    hi = jnp.where(n < nb - 1, r, -1)
    valid = (c >= lo) & ((c < 2 * BLOCK) | (c >= 3 * BLOCK) | (c - 2 * BLOCK <= hi))
    _attend(q_ref, k_all, v_all, valid, sink_ref, o_ref)


def _win_attention(h, hc, sink, batch, seq, kc_col, vc_col):
    nb = seq // BLOCK
    qcol, kcol, vcol = 0, ATTN_WIDTH // KV_WIDTH, ATTN_WIDTH // KV_WIDTH + 1

    def blk(shift, col):
        return pl.BlockSpec((BLOCK, KV_WIDTH), lambda b, n: (b * nb + jnp.clip(n + shift, 0, nb - 1), col))

    return pl.pallas_call(
        functools.partial(_win_attn_kernel, nb=nb),
        grid=(batch, nb),
        in_specs=[
            pl.BlockSpec(memory_space=pltpu.SMEM),
            pl.BlockSpec((BLOCK, ATTN_WIDTH), lambda b, n: (b * nb + n, qcol)),
            blk(-1, kcol), blk(0, kcol), blk(1, kcol),
            blk(-1, vcol), blk(0, vcol), blk(1, vcol),
            pl.BlockSpec((hc.shape[0] // batch, KV_WIDTH), lambda b, n: (b, kc_col)),
            pl.BlockSpec((hc.shape[0] // batch, KV_WIDTH), lambda b, n: (b, vc_col)),
        ],
        out_specs=pl.BlockSpec((BLOCK, ATTN_WIDTH), lambda b, n: (b * nb + n, 0)),
        out_shape=jax.ShapeDtypeStruct((batch * seq, ATTN_WIDTH), BF16),
        compiler_params=_params("arbitrary", "arbitrary"),
        name="window_attention",
    )(sink, h, h, h, h, h, h, h, hc, hc)


def _ctx_attn_kernel(sink_ref, q_ref, k_ref, v_ref, o_ref):
    _attend(q_ref, k_ref[...], v_ref[...], None, sink_ref, o_ref)


def _ctx_attention(hc, sink, batch):
    ctx_len = hc.shape[0] // batch
    kcol, vcol = ATTN_WIDTH // KV_WIDTH, ATTN_WIDTH // KV_WIDTH + 1
    return pl.pallas_call(
        _ctx_attn_kernel,
        grid=(batch,),
        in_specs=[
            pl.BlockSpec(memory_space=pltpu.SMEM),
            pl.BlockSpec((ctx_len, ATTN_WIDTH), lambda b: (b, 0)),
            pl.BlockSpec((ctx_len, KV_WIDTH), lambda b: (b, kcol)),
            pl.BlockSpec((ctx_len, KV_WIDTH), lambda b: (b, vcol)),
        ],
        out_specs=pl.BlockSpec((ctx_len, ATTN_WIDTH), lambda b: (b, 0)),
        out_shape=jax.ShapeDtypeStruct((hc.shape[0], ATTN_WIDTH), BF16),
        compiler_params=_params("arbitrary"),
        name="context_attention",
    )(sink, hc, hc, hc)


def _dft_mats(n):
    k = np.arange(n, dtype=np.int64)
    ang = 2.0 * np.pi * ((k[:, None] * k[None, :]) % n).astype(np.float64) / n
    return np.cos(ang), np.sin(ang)


def _group_dft():
    c, s = _dft_mats(FOURIER_GROUP_DIM)
    groups = FOURIER_WIDTH // FOURIER_GROUP_DIM
    eye = np.eye(groups)
    return np.concatenate([np.kron(eye, c), np.kron(eye, s)], axis=1)


def _fourier_kernel(f_ref, w1_ref, c_ref, s_ref, o_ref, ab_ref, *, scale):
    @pl.when(pl.program_id(1) == 0)
    def _():
        ab_ref[...] = jnp.dot(f_ref[...], w1_ref[...], preferred_element_type=F32).astype(BF16)

    y = (jnp.dot(c_ref[...], ab_ref[:, :FOURIER_WIDTH], preferred_element_type=F32)
         - jnp.dot(s_ref[...], ab_ref[:, FOURIER_WIDTH:], preferred_element_type=F32))
    o_ref[...] = (y * scale).astype(BF16)


def _fourier_mix(h, batch, n):
    tr = 256
    cn, sn = _dft_mats(n)
    fcol = (EVEN_IN_WIDTH - FOURIER_WIDTH) // FOURIER_WIDTH
    return pl.pallas_call(
        functools.partial(_fourier_kernel, scale=float((n * FOURIER_GROUP_DIM) ** -0.5)),
        grid=(batch, n // tr),
        in_specs=[
            pl.BlockSpec((n, FOURIER_WIDTH), lambda b, i: (b, fcol)),
            pl.BlockSpec((FOURIER_WIDTH, 2 * FOURIER_WIDTH), lambda b, i: (0, 0)),
            pl.BlockSpec((tr, n), lambda b, i: (i, 0)),
            pl.BlockSpec((tr, n), lambda b, i: (i, 0)),
        ],
        out_specs=pl.BlockSpec((tr, FOURIER_WIDTH), lambda b, i: (b * (n // tr) + i, 0)),
        out_shape=jax.ShapeDtypeStruct((batch * n, FOURIER_WIDTH), BF16),
        scratch_shapes=[pltpu.VMEM((n, 2 * FOURIER_WIDTH), BF16)],
        compiler_params=_params("arbitrary", "arbitrary"),
        name="fourier_mix",
    )(h, jnp.asarray(_group_dft(), BF16), jnp.asarray(cn, BF16), jnp.asarray(sn, BF16))


def _inproj_odd_kernel(x_ref, sc_ref, sh_ref, w_ref, o_ref):
    u = (x_ref[...] * (1.0 + sc_ref[0]) + sh_ref[0]).astype(BF16)
    tn = 512
    for jt in range(D_MODEL // tn):
        cols = slice(jt * tn, (jt + 1) * tn)
        b_gate = jnp.dot(u, w_ref[:, jt * tn:(jt + 1) * tn], preferred_element_type=F32)
        o_ref[:, cols] = b_gate.astype(BF16)
        c_gate = jnp.dot(u, w_ref[:, D_MODEL + jt * tn:D_MODEL + (jt + 1) * tn], preferred_element_type=F32)
        hh = jnp.dot(u, w_ref[:, 2 * D_MODEL + jt * tn:2 * D_MODEL + (jt + 1) * tn], preferred_element_type=F32)
        o_ref[:, D_MODEL + jt * tn:D_MODEL + (jt + 1) * tn] = (c_gate * hh).astype(BF16)


def _inproj_odd(x2, sc, sh, w, seq_rows, mod_row):
    t = x2.shape[0]
    tm = min(ROW_TILE, seq_rows)
    mod = _mod_spec(tm, seq_rows, mod_row)

    return pl.pallas_call(
        _inproj_odd_kernel,
        grid=(t // tm,),
        in_specs=[
            pl.BlockSpec((tm, D_MODEL), lambda i: (i, 0)),
            mod,
            mod,
            pl.BlockSpec((D_MODEL, 3 * D_MODEL), lambda i: (0, 0)),
        ],
        out_specs=pl.BlockSpec((tm, 2 * D_MODEL), lambda i: (i, 0)),
        out_shape=jax.ShapeDtypeStruct((t, 2 * D_MODEL), BF16),
        compiler_params=_params("arbitrary"),
        name="inproj_odd",
    )(x2, sc, sh, w)


def _layer_norm(r, g, b):
    mu = jnp.mean(r, axis=-1, keepdims=True)
    d = r - mu
    var = jnp.mean(d * d, axis=-1, keepdims=True)
    return d * lax.rsqrt(var + LN_EPS) * g + b


def _post_mixer(y, x_ref, g1_ref, sc2_ref, sh2_ref, lng_ref, lnb_ref, wrh_ref, wrl_ref, br_ref, xo_ref, v_ref, lg_ref):
    xn = _layer_norm(DEEPNORM_ALPHA * x_ref[...] + g1_ref[0] * y, lng_ref[...], lnb_ref[...])
    xo_ref[...] = xn
    v = xn * (1.0 + sc2_ref[0]) + sh2_ref[0]
    v_hi = v.astype(BF16)
    v_ref[...] = v_hi
    v_lo = (v - v_hi.astype(F32)).astype(BF16)
    lg_ref[...] = (jnp.dot(v_hi, wrh_ref[...], preferred_element_type=F32)
                   + (jnp.dot(v_lo, wrh_ref[...], preferred_element_type=F32)
                      + jnp.dot(v_hi, wrl_ref[...], preferred_element_type=F32))) + br_ref[...]


def _outproj_even_kernel(a_ref, f_ref, w_ref, *rest):
    y = (jnp.dot(a_ref[...], w_ref[:ATTN_WIDTH, :], preferred_element_type=F32)
         + jnp.dot(f_ref[...], w_ref[ATTN_WIDTH:, :], preferred_element_type=F32))
    _post_mixer(y, *rest)


def _outproj_odd_kernel(b_ref, z_ref, zp_ref, zn_ref, cw_ref, w_ref, *rest, rows_per_seq):
    tm = z_ref.shape[0]
    halo = zp_ref.shape[0]
    row0 = pl.program_id(0) * tm
    z = z_ref[...].astype(F32)
    prev_row = jnp.where(row0 % rows_per_seq == 0, 0.0, zp_ref[halo - 1:halo, :].astype(F32))
    next_row = jnp.where((row0 + tm) % rows_per_seq == 0, 0.0, zn_ref[0:1, :].astype(F32))
    r = lax.broadcasted_iota(jnp.int32, z.shape, 0)
    z_prev = jnp.where(r == 0, prev_row, pltpu.roll(z, 1, axis=0))
    z_next = jnp.where(r == tm - 1, next_row, pltpu.roll(z, tm - 1, axis=0))
    conv = cw_ref[0:1, :] * z_prev + cw_ref[1:2, :] * z + cw_ref[2:3, :] * z_next
    a = (b_ref[...].astype(F32) * conv).astype(BF16)
    y = jnp.dot(a, w_ref[...], preferred_element_type=F32)
    _post_mixer(y, *rest)


def _outproj(kind, mix_inputs, w_out, x2, g1, sc2, sh2, lng, lnb, wr_hi, wr_lo, br, seq_rows, mod_row, conv_w=None):
    t = x2.shape[0]
    tm = min(ROW_TILE, seq_rows)
    mod = _mod_spec(tm, seq_rows, mod_row)

    row = lambda width: pl.BlockSpec((tm, width), lambda i: (i, 0))
    const = lambda shape: pl.BlockSpec(shape, lambda i: (0,) * len(shape))
    if kind == "even":
        attn, four = mix_inputs
        kern = _outproj_even_kernel
        head_specs = [row(ATTN_WIDTH), row(FOURIER_WIDTH)]
        head_args = [attn, four]
    else:
        (bz,) = mix_inputs
        halo = 16
        last = t // halo - 1
        kern = functools.partial(_outproj_odd_kernel, rows_per_seq=seq_rows)
        head_specs = [
            pl.BlockSpec((tm, D_MODEL), lambda i: (i, 0)),
            pl.BlockSpec((tm, D_MODEL), lambda i: (i, 1)),
            pl.BlockSpec((halo, D_MODEL), lambda i: (jnp.maximum(i * (tm // halo) - 1, 0), 1)),
            pl.BlockSpec((halo, D_MODEL), lambda i: (jnp.minimum((i + 1) * (tm // halo), last), 1)),
            const((8, D_MODEL)),
        ]
        head_args = [bz, bz, bz, bz, conv_w]
    return pl.pallas_call(
        kern,
        grid=(t // tm,),
        in_specs=head_specs + [
            const((D_MODEL, D_MODEL)),
            row(D_MODEL),
            mod,
            mod,
            mod,
            const((1, D_MODEL)), const((1, D_MODEL)),
            const((D_MODEL, LANES)), const((D_MODEL, LANES)), const((1, LANES)),
        ],
        out_specs=[row(D_MODEL), row(D_MODEL), row(LANES)],
        out_shape=[
            jax.ShapeDtypeStruct((t, D_MODEL), F32),
            jax.ShapeDtypeStruct((t, D_MODEL), BF16),
            jax.ShapeDtypeStruct((t, LANES), F32),
        ],
        compiler_params=_params("arbitrary"),
        name="outproj_" + kind,
    )(*head_args, w_out, x2, g1, sc2, sh2, lng, lnb, wr_hi, wr_lo, br)


def _expert_kernel(blk_e_ref, n_used_ref, x_ref, gate_ref, wu_ref, bu_ref, wd_ref, bd_ref, o_ref):
    del blk_e_ref

    @pl.when(pl.program_id(0) < n_used_ref[0])
    def _():
        h = jnp.dot(x_ref[...], wu_ref[0], preferred_element_type=F32) + bu_ref[0]
        glu = jnp.minimum(h[:, :D_EXPERT], SWIGLU_LIMIT)
        lin = jnp.clip(h[:, D_EXPERT:], -SWIGLU_LIMIT, SWIGLU_LIMIT)
        a = glu * jax.nn.sigmoid(SWIGLU_ALPHA * glu) * (lin + 1.0)
        y = jnp.dot(a.astype(BF16), wd_ref[0], preferred_element_type=F32) + bd_ref[0]
        o_ref[...] = y * gate_ref[...]


def _experts(blk_e, n_used, xbuf, gate_buf, wu, bu, wd, bd):
    n_rows = xbuf.shape[0]
    n_blk = n_rows // EXPERT_ROWS
    grid_spec = pltpu.PrefetchScalarGridSpec(
        num_scalar_prefetch=2,
        grid=(n_blk,),
        in_specs=[
            pl.BlockSpec((EXPERT_ROWS, D_MODEL), lambda i, e, u: (i, 0)),
            pl.BlockSpec((EXPERT_ROWS, 1), lambda i, e, u: (i, 0)),
            pl.BlockSpec((1, D_MODEL, 2 * D_EXPERT), lambda i, e, u: (e[i], 0, 0)),
            pl.BlockSpec((1, 1, 2 * D_EXPERT), lambda i, e, u: (e[i], 0, 0)),
            pl.BlockSpec((1, D_EXPERT, D_MODEL), lambda i, e, u: (e[i], 0, 0)),
            pl.BlockSpec((1, 1, D_MODEL), lambda i, e, u: (e[i], 0, 0)),
        ],
        out_specs=pl.BlockSpec((EXPERT_ROWS, D_MODEL), lambda i, e, u: (i, 0)),
    )
    return pl.pallas_call(
        _expert_kernel,
        grid_spec=grid_spec,
        out_shape=jax.ShapeDtypeStruct((n_rows, D_MODEL), F32),
        compiler_params=_params("arbitrary"),
        name="experts",
    )(blk_e, n_used, xbuf, gate_buf, wu, bu, wd, bd)


def _resid_ln_kernel(x_ref, f_ref, g2_ref, lng_ref, lnb_ref, o_ref):
    o_ref[...] = _layer_norm(DEEPNORM_ALPHA * x_ref[...] + g2_ref[0] * f_ref[...], lng_ref[...], lnb_ref[...])


def _resid_ln(x2, f, g2, lng, lnb, seq_rows, mod_row):
    t = x2.shape[0]
    tm = min(ROW_TILE, seq_rows)
    row = pl.BlockSpec((tm, D_MODEL), lambda i: (i, 0))
    const = pl.BlockSpec((1, D_MODEL), lambda i: (0, 0))
    return pl.pallas_call(
        _resid_ln_kernel,
        grid=(t // tm,),
        in_specs=[row, row, _mod_spec(tm, seq_rows, mod_row), const, const],
        out_specs=row,
        out_shape=jax.ShapeDtypeStruct((t, D_MODEL), F32),
        compiler_params=_params("arbitrary"),
        name="resid_ln",
    )(x2, f, g2, lng, lnb)


def _moe(v_all, logits, wu, bu, wd, bd):
    t = v_all.shape[0]
    n = t * TOP_K
    top_v, top_i = lax.top_k(logits[:, :N_EXPERTS], TOP_K)
    gates = jax.nn.softmax(top_v, axis=-1)
    flat_e = top_i.reshape(-1)
    order = jnp.argsort(flat_e)
    e_sorted = flat_e[order]
    tok_sorted = order // TOP_K
    sizes = jnp.bincount(flat_e, length=N_EXPERTS)
    padded = (sizes + EXPERT_ROWS - 1) // EXPERT_ROWS * EXPERT_ROWS
    pad_ends = jnp.cumsum(padded)
    pad_starts = pad_ends - padded
    starts = jnp.cumsum(sizes) - sizes
    dest = (pad_starts[e_sorted] + (jnp.arange(n) - starts[e_sorted])).astype(jnp.int32)
    n_blk = n // EXPERT_ROWS + N_EXPERTS
    n_rows = n_blk * EXPERT_ROWS
    src = jnp.zeros((n_rows,), jnp.int32).at[dest].set(tok_sorted.astype(jnp.int32))
    gate_buf = jnp.zeros((n_rows,), F32).at[dest].set(gates.reshape(-1)[order])
    blk_e = jnp.minimum(jnp.searchsorted(pad_ends, jnp.arange(n_blk) * EXPERT_ROWS, side='right'),
                        N_EXPERTS - 1).astype(jnp.int32)
    n_used = (pad_ends[-1:] // EXPERT_ROWS).astype(jnp.int32)
    xbuf = jnp.take(v_all, src, axis=0)
    ybuf = _experts(blk_e, n_used, xbuf, gate_buf.reshape(n_rows, 1), wu, bu, wd, bd)
    pos = jnp.zeros((n,), jnp.int32).at[order].set(dest)
    return jnp.take(ybuf, pos, axis=0).reshape(t, TOP_K, D_MODEL).sum(axis=1)


def _rope_tables(seq):
    rows = seq // GRID_W
    row = jnp.repeat(jnp.arange(rows), GRID_W).astype(F32)
    col = jnp.tile(jnp.arange(GRID_W), rows).astype(F32)
    n_freq = HEAD_DIM // 4
    inv = ROPE_BASE ** (-jnp.arange(n_freq, dtype=F32) / n_freq)
    ang = jnp.concatenate([row[:, None] * inv, col[:, None] * inv], -1)
    ang = jnp.concatenate([ang, ang], -1)
    sign = jnp.where(jnp.arange(HEAD_DIM) < HEAD_DIM // 2, -1.0, 1.0).astype(F32)
    cos = jnp.cos(ang)
    sin = jnp.sin(ang) * sign
    return jnp.tile(cos, (1, LANES // HEAD_DIM)), jnp.tile(sin, (1, LANES // HEAD_DIM))


def kernel(x, c, ctx, c_ctx, w_mod, b_mod, w_in_even, sink, w_out_even, w_in_odd, conv_w, w_out_odd, ln_g, ln_b,
           w_router, b_router, w_up, b_up, w_down, b_down):
    B, S, D = x.shape
    L = ctx.shape[1]
    T, TC = B * S, B * L
    cos, sin = _rope_tables(S)
    cond = jnp.concatenate([c, c_ctx[None, :], jnp.zeros((MOD_ROWS - B - 1, D), F32)], 0)
    mod = _modulation(cond, w_mod, b_mod)

    wu = jnp.concatenate([w_up[..., 0::2], w_up[..., 1::2]], -1).astype(BF16)
    bu = jnp.concatenate([b_up[..., 0::2], b_up[..., 1::2]], -1).reshape(DEPTH, N_EXPERTS, 1, 2 * D_EXPERT)
    wd = w_down.astype(BF16)
    bd = b_down.reshape(DEPTH, N_EXPERTS, 1, D)
    wr_hi = w_router.astype(BF16)
    wr_lo = (w_router - wr_hi.astype(F32)).astype(BF16)
    pad_r = ((0, 0), (0, 0), (0, LANES - N_EXPERTS))
    wr_hi, wr_lo = jnp.pad(wr_hi, pad_r), jnp.pad(wr_lo, pad_r)
    br = jnp.pad(b_router, ((0, 0), (0, LANES - N_EXPERTS)), constant_values=NEG_INF).reshape(DEPTH, 1, LANES)

    x2 = x.reshape(T, D)
    xc2 = ctx.reshape(TC, D)
    for l in range(DEPTH):
        even = l % 2 == 0
        j = l // 2
        ctx_after = any(m % 2 == 0 for m in range(l + 1, DEPTH))
        sh1, sc1, g1, sh2, sc2, g2 = [m.reshape(MOD_ROWS, 1, D) for m in jnp.split(mod[l], 6, axis=-1)]
        lng1, lnb1 = ln_g[l, 0].reshape(1, D), ln_b[l, 0].reshape(1, D)
        lng2, lnb2 = ln_g[l, 1].reshape(1, D), ln_b[l, 1].reshape(1, D)
        post = (g1, sc2, sh2, lng1, lnb1, wr_hi[l], wr_lo[l], br[l])
        if even:
            w_in = w_in_even[j].astype(BF16)
            w_out = w_out_even[j].astype(BF16)
            h = _inproj_even(x2, sc1, sh1, w_in, cos, sin, S, None, True)
            hc = _inproj_even(xc2, sc1, sh1, w_in, cos, sin, L, B, False)
            kcol = ATTN_WIDTH // KV_WIDTH
            attn = _win_attention(h, hc, sink[j], B, S, kcol, kcol + 1)
            four = _fourier_mix(h, B, S)
            x2, v, logits = _outproj("even", (attn, four), w_out, x2, *post, S, None)
            if ctx_after:
                attn_c = _ctx_attention(hc, sink[j], B)
                four_c = _fourier_mix(hc, B, L)
                xc2, vc, logits_c = _outproj("even", (attn_c, four_c), w_out, xc2, *post, L, B)
        else:
            w_in = w_in_odd[j].astype(BF16)
            w_out = w_out_odd[j].astype(BF16)
            cw = jnp.pad(conv_w[j], ((0, 8 - conv_w.shape[1]), (0, 0)))
            bz = _inproj_odd(x2, sc1, sh1, w_in, S, None)
            x2, v, logits = _outproj("odd", (bz,), w_out, x2, *post, S, None, conv_w=cw)
            if ctx_after:
                bzc = _inproj_odd(xc2, sc1, sh1, w_in, L, B)
                xc2, vc, logits_c = _outproj("odd", (bzc,), w_out, xc2, *post, L, B, conv_w=cw)
        if ctx_after:
            f_all = _moe(jnp.concatenate([v, vc], 0), jnp.concatenate([logits, logits_c], 0), wu[l], bu[l], wd[l], bd[l])
            xc2 = _resid_ln(xc2, f_all[T:], g2, lng2, lnb2, L, B)
            f = f_all[:T]
        else:
            f = _moe(v, logits, wu[l], bu[l], wd[l], bd[l])
        x2 = _resid_ln(x2, f, g2, lng2, lnb2, S, None)
    return x2.reshape(B, S, D)
```

```python
import functools

import numpy as np
import jax
import jax.numpy as jnp
from jax import lax
from jax.experimental import pallas as pl
from jax.experimental.pallas import tpu as pltpu

D_MODEL = 1024
DEPTH = 4
GRID_W = 64
HEAD_DIM = 64
N_Q_HEADS = 12
N_KV_HEADS = 4
Q_PER_KV = N_Q_HEADS // N_KV_HEADS
BLOCK = 128
ROPE_BASE = 10000.0
FOURIER_GROUP_DIM = 64
FOURIER_WIDTH = 256
ATTN_WIDTH = N_Q_HEADS * HEAD_DIM
KV_WIDTH = N_KV_HEADS * HEAD_DIM
EVEN_IN_WIDTH = ATTN_WIDTH + 2 * KV_WIDTH + FOURIER_WIDTH
N_EXPERTS = 32
TOP_K = 4
D_EXPERT = D_MODEL
SWIGLU_LIMIT = 7.0
SWIGLU_ALPHA = 1.702
LN_EPS = 1e-5
NEG_INF = -1e30
DEEPNORM_ALPHA = (2 * DEPTH) ** 0.25

LANES = 128
ROW_TILE = 512
EXPERT_ROWS = 256
MOD_ROWS = 16
ROUTE_TILE = 512
DISPATCH_TILE = 512
COMBINE_TILE = 256
DMA_UNROLL = 8
VMEM_LIMIT = 48 * 1024 * 1024
EXPERT_VMEM_LIMIT = 56 * 1024 * 1024

F32 = jnp.float32
BF16 = jnp.bfloat16


def _params(*sem):
    return pltpu.CompilerParams(dimension_semantics=sem, vmem_limit_bytes=VMEM_LIMIT)


def _mod_spec(tm, seq_rows, mod_row):
    if mod_row is None:
        return pl.BlockSpec((1, 1, D_MODEL), lambda i: ((i * tm) // seq_rows, 0, 0))
    return pl.BlockSpec((1, 1, D_MODEL), lambda i: (mod_row, 0, 0))


def _mod_kernel(c_ref, w_ref, b_ref, o_ref):
    c = c_ref[...]
    s = (c * jax.nn.sigmoid(c)).astype(BF16)
    o_ref[0] = jnp.dot(s, w_ref[0].astype(BF16), preferred_element_type=F32) + b_ref[0]


def _modulation(cond, w_mod, b_mod):
    tn = 1536
    n = w_mod.shape[-1]
    return pl.pallas_call(
        _mod_kernel,
        grid=(DEPTH, n // tn),
        in_specs=[
            pl.BlockSpec((MOD_ROWS, D_MODEL), lambda l, j: (0, 0)),
            pl.BlockSpec((1, D_MODEL, tn), lambda l, j: (l, 0, j)),
            pl.BlockSpec((1, 1, tn), lambda l, j: (l, 0, j)),
        ],
        out_specs=pl.BlockSpec((1, MOD_ROWS, tn), lambda l, j: (l, 0, j)),
        out_shape=jax.ShapeDtypeStruct((DEPTH, MOD_ROWS, n), F32),
        compiler_params=_params("arbitrary", "arbitrary"),
        name="modulation",
    )(cond, w_mod, b_mod.reshape(DEPTH, 1, n))


def _rope_chunk(h, cos, sin_signed, first_half):
    swapped = jnp.where(first_half, pltpu.roll(h, LANES - HEAD_DIM // 2, axis=1), pltpu.roll(h, HEAD_DIM // 2, axis=1))
    return h * cos + swapped * sin_signed


def _inproj_even_kernel(x_ref, sc_ref, sh_ref, w_ref, cos_ref, sin_ref, o_ref, *, rope):
    u = (x_ref[...] * (1.0 + sc_ref[0]) + sh_ref[0]).astype(BF16)
    tn = 512
    if rope:
        cos = cos_ref[...]
        sin = sin_ref[...]
        lane = lax.broadcasted_iota(jnp.int32, cos.shape, 1)
        first_half = (lane % HEAD_DIM) < HEAD_DIM // 2
    scale = HEAD_DIM ** -0.5
    for jt in range(EVEN_IN_WIDTH // tn):
        acc = jnp.dot(u, w_ref[:, jt * tn:(jt + 1) * tn], preferred_element_type=F32)
        for k in range(tn // LANES):
            col = jt * tn + k * LANES
            h = acc[:, k * LANES:(k + 1) * LANES]
            if rope and col < ATTN_WIDTH + KV_WIDTH:
                h = _rope_chunk(h, cos, sin, first_half)
            if col < ATTN_WIDTH:
                h = h * scale
            o_ref[:, col:col + LANES] = h.astype(BF16)


def _inproj_even(x2, sc, sh, w, cos, sin, seq_rows, mod_row, rope):
    t = x2.shape[0]
    tm = min(ROW_TILE, seq_rows)
    pos_tiles = cos.shape[0] // tm
    mod = _mod_spec(tm, seq_rows, mod_row)

    return pl.pallas_call(
        functools.partial(_inproj_even_kernel, rope=rope),
        grid=(t // tm,),
        in_specs=[
            pl.BlockSpec((tm, D_MODEL), lambda i: (i, 0)),
            mod,
            mod,
            pl.BlockSpec((D_MODEL, EVEN_IN_WIDTH), lambda i: (0, 0)),
            pl.BlockSpec((tm, LANES), lambda i: (i % pos_tiles, 0)),
            pl.BlockSpec((tm, LANES), lambda i: (i % pos_tiles, 0)),
        ],
        out_specs=pl.BlockSpec((tm, EVEN_IN_WIDTH), lambda i: (i, 0)),
        out_shape=jax.ShapeDtypeStruct((t, EVEN_IN_WIDTH), BF16),
        compiler_params=_params("arbitrary"),
        name="inproj_even",
    )(x2, sc, sh, w, cos, sin)


def _attend(q_ref, k_all, v_all, valid, sink_ref, o_ref):
    rows = q_ref.shape[0]
    lane = lax.broadcasted_iota(jnp.int32, (rows, LANES), 1)
    low = lane < HEAD_DIM
    for kv in range(N_KV_HEADS):
        kv_chunk, kv_half = divmod(kv, 2)
        keep = low if kv_half == 0 else jnp.logical_not(low)
        stack, sinks = [], []
        for g in range(Q_PER_KV):
            head = kv * Q_PER_KV + g
            chunk, half = divmod(head, 2)
            qh = q_ref[:, chunk * LANES:(chunk + 1) * LANES].astype(F32)
            if half != kv_half:
                qh = pltpu.roll(qh, HEAD_DIM, axis=1)
            stack.append(jnp.where(keep, qh, 0.0).astype(BF16))
            sinks.append(jnp.full((rows, 1), sink_ref[head], F32))
        qs = jnp.concatenate(stack, axis=0)
        sk = jnp.concatenate(sinks, axis=0)
        kc = k_all[:, kv_chunk * LANES:(kv_chunk + 1) * LANES]
        vc = v_all[:, kv_chunk * LANES:(kv_chunk + 1) * LANES]
        s = lax.dot_general(qs, kc, (((1,), (1,)), ((), ())), preferred_element_type=F32)
        if valid is not None:
            s = jnp.where(valid, s, NEG_INF)
        m = jnp.maximum(sk, jnp.max(s, axis=-1, keepdims=True))
        e = jnp.exp(s - m)
        denom = jnp.exp(sk - m) + jnp.sum(e, axis=-1, keepdims=True)
        o = jnp.dot(e.astype(BF16), vc, preferred_element_type=F32) / denom
        for g in range(Q_PER_KV):
            head = kv * Q_PER_KV + g
            half = head % 2
            oh = o[g * rows:(g + 1) * rows]
            if half != kv_half:
                oh = pltpu.roll(oh, HEAD_DIM, axis=1)
            o_ref[:, head * HEAD_DIM:(head + 1) * HEAD_DIM] = oh[:, half * HEAD_DIM:(half + 1) * HEAD_DIM].astype(BF16)


def _win_attn_kernel(sink_ref, q_ref, kp_ref, kc_ref, kn_ref, vp_ref, vc_ref, vn_ref, kx_ref, vx_ref, o_ref, *, nb):
    n = pl.program_id(1)
    k_all = jnp.concatenate([kp_ref[...], kc_ref[...], kn_ref[...], kx_ref[...]], axis=0)
    v_all = jnp.concatenate([vp_ref[...], vc_ref[...], vn_ref[...], vx_ref[...]], axis=0)
    n_keys = k_all.shape[0]
    r = lax.broadcasted_iota(jnp.int32, (Q_PER_KV * BLOCK, n_keys), 0) % BLOCK
    c = lax.broadcasted_iota(jnp.int32, (Q_PER_KV * BLOCK, n_keys), 1)
    lo = jnp.where(n > 0, r, BLOCK)
    hi = jnp.where(n < nb - 1, r, -1)
    valid = (c >= lo) & ((c < 2 * BLOCK) | (c >= 3 * BLOCK) | (c - 2 * BLOCK <= hi))
    _attend(q_ref, k_all, v_all, valid, sink_ref, o_ref)


def _win_attention(h, hc, sink, batch, seq, kc_col, vc_col):
    nb = seq // BLOCK
    qcol, kcol, vcol = 0, ATTN_WIDTH // KV_WIDTH, ATTN_WIDTH // KV_WIDTH + 1

    def blk(shift, col):
        return pl.BlockSpec((BLOCK, KV_WIDTH), lambda b, n: (b * nb + jnp.clip(n + shift, 0, nb - 1), col))

    return pl.pallas_call(
        functools.partial(_win_attn_kernel, nb=nb),
        grid=(batch, nb),
        in_specs=[
            pl.BlockSpec(memory_space=pltpu.SMEM),
            pl.BlockSpec((BLOCK, ATTN_WIDTH), lambda b, n: (b * nb + n, qcol)),
            blk(-1, kcol), blk(0, kcol), blk(1, kcol),
            blk(-1, vcol), blk(0, vcol), blk(1, vcol),
            pl.BlockSpec((hc.shape[0] // batch, KV_WIDTH), lambda b, n: (b, kc_col)),
            pl.BlockSpec((hc.shape[0] // batch, KV_WIDTH), lambda b, n: (b, vc_col)),
        ],
        out_specs=pl.BlockSpec((BLOCK, ATTN_WIDTH), lambda b, n: (b * nb + n, 0)),
        out_shape=jax.ShapeDtypeStruct((batch * seq, ATTN_WIDTH), BF16),
        compiler_params=_params("arbitrary", "arbitrary"),
        name="window_attention",
    )(sink, h, h, h, h, h, h, h, hc, hc)


def _ctx_attn_kernel(sink_ref, q_ref, k_ref, v_ref, o_ref):
    _attend(q_ref, k_ref[...], v_ref[...], None, sink_ref, o_ref)


def _ctx_attention(hc, sink, batch):
    ctx_len = hc.shape[0] // batch
    kcol, vcol = ATTN_WIDTH // KV_WIDTH, ATTN_WIDTH // KV_WIDTH + 1
    return pl.pallas_call(
        _ctx_attn_kernel,
        grid=(batch,),
        in_specs=[
            pl.BlockSpec(memory_space=pltpu.SMEM),
            pl.BlockSpec((ctx_len, ATTN_WIDTH), lambda b: (b, 0)),
            pl.BlockSpec((ctx_len, KV_WIDTH), lambda b: (b, kcol)),
            pl.BlockSpec((ctx_len, KV_WIDTH), lambda b: (b, vcol)),
        ],
        out_specs=pl.BlockSpec((ctx_len, ATTN_WIDTH), lambda b: (b, 0)),
        out_shape=jax.ShapeDtypeStruct((hc.shape[0], ATTN_WIDTH), BF16),
        compiler_params=_params("arbitrary"),
        name="context_attention",
    )(sink, hc, hc, hc)


def _dft_mats(n):
    k = np.arange(n, dtype=np.int64)
    ang = 2.0 * np.pi * ((k[:, None] * k[None, :]) % n).astype(np.float64) / n
    return np.cos(ang), np.sin(ang)


def _group_dft():
    c, s = _dft_mats(FOURIER_GROUP_DIM)
    groups = FOURIER_WIDTH // FOURIER_GROUP_DIM
    eye = np.eye(groups)
    return np.concatenate([np.kron(eye, c), np.kron(eye, s)], axis=1)


def _fourier_kernel(f_ref, w1_ref, c_ref, s_ref, o_ref, ab_ref, *, scale):
    @pl.when(pl.program_id(1) == 0)
    def _():
        ab_ref[...] = jnp.dot(f_ref[...], w1_ref[...], preferred_element_type=F32).astype(BF16)

    y = (jnp.dot(c_ref[...], ab_ref[:, :FOURIER_WIDTH], preferred_element_type=F32)
         - jnp.dot(s_ref[...], ab_ref[:, FOURIER_WIDTH:], preferred_element_type=F32))
    o_ref[...] = (y * scale).astype(BF16)


def _fourier_mix(h, batch, n):
    tr = 256
    cn, sn = _dft_mats(n)
    fcol = (EVEN_IN_WIDTH - FOURIER_WIDTH) // FOURIER_WIDTH
    return pl.pallas_call(
        functools.partial(_fourier_kernel, scale=float((n * FOURIER_GROUP_DIM) ** -0.5)),
        grid=(batch, n // tr),
        in_specs=[
            pl.BlockSpec((n, FOURIER_WIDTH), lambda b, i: (b, fcol)),
            pl.BlockSpec((FOURIER_WIDTH, 2 * FOURIER_WIDTH), lambda b, i: (0, 0)),
            pl.BlockSpec((tr, n), lambda b, i: (i, 0)),
            pl.BlockSpec((tr, n), lambda b, i: (i, 0)),
        ],
        out_specs=pl.BlockSpec((tr, FOURIER_WIDTH), lambda b, i: (b * (n // tr) + i, 0)),
        out_shape=jax.ShapeDtypeStruct((batch * n, FOURIER_WIDTH), BF16),
        scratch_shapes=[pltpu.VMEM((n, 2 * FOURIER_WIDTH), BF16)],
        compiler_params=_params("arbitrary", "arbitrary"),
        name="fourier_mix",
    )(h, jnp.asarray(_group_dft(), BF16), jnp.asarray(cn, BF16), jnp.asarray(sn, BF16))


def _inproj_odd_kernel(x_ref, sc_ref, sh_ref, w_ref, o_ref):
    u = (x_ref[...] * (1.0 + sc_ref[0]) + sh_ref[0]).astype(BF16)
    tn = 512
    for jt in range(D_MODEL // tn):
        cols = slice(jt * tn, (jt + 1) * tn)
        b_gate = jnp.dot(u, w_ref[:, jt * tn:(jt + 1) * tn], preferred_element_type=F32)
        o_ref[:, cols] = b_gate.astype(BF16)
        c_gate = jnp.dot(u, w_ref[:, D_MODEL + jt * tn:D_MODEL + (jt + 1) * tn], preferred_element_type=F32)
        hh = jnp.dot(u, w_ref[:, 2 * D_MODEL + jt * tn:2 * D_MODEL + (jt + 1) * tn], preferred_element_type=F32)
        o_ref[:, D_MODEL + jt * tn:D_MODEL + (jt + 1) * tn] = (c_gate * hh).astype(BF16)


def _inproj_odd(x2, sc, sh, w, seq_rows, mod_row):
    t = x2.shape[0]
    tm = min(ROW_TILE, seq_rows)
    mod = _mod_spec(tm, seq_rows, mod_row)

    return pl.pallas_call(
        _inproj_odd_kernel,
        grid=(t // tm,),
        in_specs=[
            pl.BlockSpec((tm, D_MODEL), lambda i: (i, 0)),
            mod,
            mod,
            pl.BlockSpec((D_MODEL, 3 * D_MODEL), lambda i: (0, 0)),
        ],
        out_specs=pl.BlockSpec((tm, 2 * D_MODEL), lambda i: (i, 0)),
        out_shape=jax.ShapeDtypeStruct((t, 2 * D_MODEL), BF16),
        compiler_params=_params("arbitrary"),
        name="inproj_odd",
    )(x2, sc, sh, w)


def _layer_norm(r, g, b):
    mu = jnp.mean(r, axis=-1, keepdims=True)
    d = r - mu
    var = jnp.mean(d * d, axis=-1, keepdims=True)
    return d * lax.rsqrt(var + LN_EPS) * g + b


def _post_mixer(y, x_ref, g1_ref, sc2_ref, sh2_ref, lng_ref, lnb_ref, wrh_ref, wrl_ref, br_ref, xo_ref, v_ref, lg_ref):
    xn = _layer_norm(DEEPNORM_ALPHA * x_ref[...] + g1_ref[0] * y, lng_ref[...], lnb_ref[...])
    xo_ref[...] = xn
    v = xn * (1.0 + sc2_ref[0]) + sh2_ref[0]
    v_ref[...] = v
    v_hi = v.astype(BF16)
    v_lo = (v - v_hi.astype(F32)).astype(BF16)
    lg_ref[...] = (jnp.dot(v_hi, wrh_ref[...], preferred_element_type=F32)
                   + (jnp.dot(v_lo, wrh_ref[...], preferred_element_type=F32)
                      + jnp.dot(v_hi, wrl_ref[...], preferred_element_type=F32))) + br_ref[...]


def _outproj_even_kernel(a_ref, f_ref, w_ref, *rest):
    y = (jnp.dot(a_ref[...], w_ref[:ATTN_WIDTH, :], preferred_element_type=F32)
         + jnp.dot(f_ref[...], w_ref[ATTN_WIDTH:, :], preferred_element_type=F32))
    _post_mixer(y, *rest)


def _outproj_odd_kernel(b_ref, z_ref, zp_ref, zn_ref, cw_ref, w_ref, *rest, rows_per_seq):
    tm = z_ref.shape[0]
    halo = zp_ref.shape[0]
    row0 = pl.program_id(0) * tm
    z = z_ref[...].astype(F32)
    prev_row = jnp.where(row0 % rows_per_seq == 0, 0.0, zp_ref[halo - 1:halo, :].astype(F32))
    next_row = jnp.where((row0 + tm) % rows_per_seq == 0, 0.0, zn_ref[0:1, :].astype(F32))
    r = lax.broadcasted_iota(jnp.int32, z.shape, 0)
    z_prev = jnp.where(r == 0, prev_row, pltpu.roll(z, 1, axis=0))
    z_next = jnp.where(r == tm - 1, next_row, pltpu.roll(z, tm - 1, axis=0))
    conv = cw_ref[0:1, :] * z_prev + cw_ref[1:2, :] * z + cw_ref[2:3, :] * z_next
    a = (b_ref[...].astype(F32) * conv).astype(BF16)
    y = jnp.dot(a, w_ref[...], preferred_element_type=F32)
    _post_mixer(y, *rest)


def _outproj(kind, mix_inputs, w_out, x2, g1, sc2, sh2, lng, lnb, wr_hi, wr_lo, br, seq_rows, mod_row, conv_w=None):
    t = x2.shape[0]
    tm = min(ROW_TILE, seq_rows)
    mod = _mod_spec(tm, seq_rows, mod_row)

    row = lambda width: pl.BlockSpec((tm, width), lambda i: (i, 0))
    const = lambda shape: pl.BlockSpec(shape, lambda i: (0,) * len(shape))
    if kind == "even":
        attn, four = mix_inputs
        kern = _outproj_even_kernel
        head_specs = [row(ATTN_WIDTH), row(FOURIER_WIDTH)]
        head_args = [attn, four]
    else:
        (bz,) = mix_inputs
        halo = 16
        last = t // halo - 1
        kern = functools.partial(_outproj_odd_kernel, rows_per_seq=seq_rows)
        head_specs = [
            pl.BlockSpec((tm, D_MODEL), lambda i: (i, 0)),
            pl.BlockSpec((tm, D_MODEL), lambda i: (i, 1)),
            pl.BlockSpec((halo, D_MODEL), lambda i: (jnp.maximum(i * (tm // halo) - 1, 0), 1)),
            pl.BlockSpec((halo, D_MODEL), lambda i: (jnp.minimum((i + 1) * (tm // halo), last), 1)),
            const((8, D_MODEL)),
        ]
        head_args = [bz, bz, bz, bz, conv_w]
    return pl.pallas_call(
        kern,
        grid=(t // tm,),
        in_specs=head_specs + [
            const((D_MODEL, D_MODEL)),
            row(D_MODEL),
            mod,
            mod,
            mod,
            const((1, D_MODEL)), const((1, D_MODEL)),
            const((D_MODEL, LANES)), const((D_MODEL, LANES)), const((1, LANES)),
        ],
        out_specs=[row(D_MODEL), row(D_MODEL), row(LANES)],
        out_shape=[
            jax.ShapeDtypeStruct((t, D_MODEL), F32),
            jax.ShapeDtypeStruct((t, D_MODEL), F32),
            jax.ShapeDtypeStruct((t, LANES), F32),
        ],
        compiler_params=_params("arbitrary"),
        name="outproj_" + kind,
    )(*head_args, w_out, x2, g1, sc2, sh2, lng, lnb, wr_hi, wr_lo, br)


def _route_kernel(lg_ref, tri_ref, idx_ref, gate_ref, cnt_ref, carry_ref):
    @pl.when(pl.program_id(0) == 0)
    def _():
        carry_ref[...] = jnp.zeros_like(carry_ref)

    work = lg_ref[...]
    lane = lax.broadcasted_iota(jnp.int32, work.shape, 1).astype(F32)
    sels, vals, ids = [], [], []
    for _ in range(TOP_K):
        m = jnp.max(work, axis=1, keepdims=True)
        first = jnp.min(jnp.where(work == m, lane, float(LANES)), axis=1, keepdims=True)
        sel = lane == first
        work = jnp.where(sel, -jnp.inf, work)
        sels.append(sel)
        vals.append(m)
        ids.append(first)
    onehot = jnp.where(sels[0] | sels[1] | sels[2] | sels[3], 1.0, 0.0)
    before = jnp.dot(tri_ref[...], onehot.astype(BF16), preferred_element_type=F32) + carry_ref[...]
    carry_ref[...] = carry_ref[...] + jnp.sum(onehot, axis=0, keepdims=True)
    cnt_ref[...] = carry_ref[...]
    exps = [jnp.exp(v - vals[0]) for v in vals]
    denom = exps[0] + exps[1] + exps[2] + exps[3]
    idx_out = jnp.zeros_like(work)
    gate_out = jnp.zeros_like(work)
    for k in range(TOP_K):
        rank = jnp.sum(jnp.where(sels[k], before, 0.0), axis=1, keepdims=True)
        idx_out = jnp.where(lane == k, ids[k], jnp.where(lane == TOP_K + k, rank, idx_out))
        gate_out = jnp.where(lane == k, exps[k] / denom, gate_out)
    idx_ref[...] = idx_out.astype(jnp.int32)
    gate_ref[...] = gate_out


def _route(logits):
    t = logits.shape[0]
    tm = ROUTE_TILE
    tri = jnp.asarray(np.tril(np.ones((tm, tm), np.float32), -1), BF16)
    row = pl.BlockSpec((tm, LANES), lambda i: (i, 0))
    return pl.pallas_call(
        _route_kernel,
        grid=(t // tm,),
        in_specs=[row, pl.BlockSpec((tm, tm), lambda i: (0, 0))],
        out_specs=[row, row, pl.BlockSpec((1, LANES), lambda i: (0, 0))],
        out_shape=[
            jax.ShapeDtypeStruct((t, LANES), jnp.int32),
            jax.ShapeDtypeStruct((t, LANES), F32),
            jax.ShapeDtypeStruct((1, LANES), F32),
        ],
        scratch_shapes=[pltpu.VMEM((1, LANES), F32)],
        compiler_params=_params("arbitrary"),
        name="route",
    )(logits, tri)


def _dispatch_kernel(v_ref, pos_hbm, buf_in_hbm, buf_hbm, idx_smem, sem_idx, sem_rows):
    del buf_in_hbm
    tm = v_ref.shape[0]
    per_tile = tm * TOP_K
    i = pl.program_id(0)
    fetch = pltpu.make_async_copy(pos_hbm.at[pl.ds(i * per_tile, per_tile)], idx_smem, sem_idx)
    fetch.start()
    fetch.wait()

    def row_copy(t, p):
        return pltpu.make_async_copy(v_ref.at[pl.ds(t, 1), :], buf_hbm.at[pl.ds(p, 1), :], sem_rows)

    def issue(j, carry):
        for u in range(DMA_UNROLL):
            t = j * DMA_UNROLL + u
            for k in range(TOP_K):
                row_copy(t, idx_smem[t * TOP_K + k]).start()
        return carry

    def drain(j, carry):
        for _ in range(DMA_UNROLL * TOP_K):
            row_copy(0, 0).wait()
        return carry

    lax.fori_loop(0, tm // DMA_UNROLL, issue, 0)
    lax.fori_loop(0, tm // DMA_UNROLL, drain, 0)


def _dispatch(v, pos_flat, buf):
    t = v.shape[0]
    tm = min(DISPATCH_TILE, t)
    return pl.pallas_call(
        _dispatch_kernel,
        grid=(t // tm,),
        in_specs=[
            pl.BlockSpec((tm, D_MODEL), lambda i: (i, 0)),
            pl.BlockSpec(memory_space=pl.ANY),
            pl.BlockSpec(memory_space=pl.ANY),
        ],
        out_specs=pl.BlockSpec(memory_space=pl.ANY),
        out_shape=jax.ShapeDtypeStruct(buf.shape, buf.dtype),
        scratch_shapes=[pltpu.SMEM((tm * TOP_K,), jnp.int32), pltpu.SemaphoreType.DMA, pltpu.SemaphoreType.DMA],
        input_output_aliases={2: 0},
        compiler_params=_params("arbitrary"),
        name="dispatch",
    )(v, pos_flat, buf)


def _pair_shuffle():
    l = np.arange(LANES)
    p = np.zeros((LANES, LANES), np.float32)
    p[l, l // 2 + (LANES // 2) * (l % 2)] = 1.0
    return p


def _expert_kernel(blk_e_ref, n_used_ref, x_ref, wu_ref, bu_ref, wd_ref, bd_ref, perm_ref, o_ref, wu_bf, wd_bf):
    i = pl.program_id(0)
    active = i < n_used_ref[0]
    new_expert = (i == 0) | (blk_e_ref[i] != blk_e_ref[jnp.maximum(i - 1, 0)])

    @pl.when(active & new_expert)
    def _():
        def cast(c, carry):
            r = pl.multiple_of(c * LANES, LANES)
            wu_bf[pl.ds(r, LANES), :] = wu_ref[0, 0, pl.ds(r, LANES), :].astype(BF16)
            wd_bf[pl.ds(r, LANES), :] = jnp.dot(perm_ref[...], wd_ref[0, 0, pl.ds(r, LANES), :].astype(BF16),
                                               preferred_element_type=F32).astype(BF16)
            return carry

        lax.fori_loop(0, D_MODEL // LANES, cast, 0)

    @pl.when(active)
    def _():
        h = jnp.dot(x_ref[...].astype(BF16), wu_bf[...], preferred_element_type=F32) + bu_ref[0]
        even = lax.broadcasted_iota(jnp.int32, (x_ref.shape[0], LANES), 1) % 2 == 0
        acts = []
        for k in range(D_EXPERT // LANES):
            ha = h[:, 2 * k * LANES:(2 * k + 1) * LANES]
            hb = h[:, (2 * k + 1) * LANES:(2 * k + 2) * LANES]
            glu = jnp.where(even, ha, pltpu.roll(hb, 1, axis=1))
            lin = jnp.where(even, pltpu.roll(ha, LANES - 1, axis=1), hb)
            glu = jnp.minimum(glu, SWIGLU_LIMIT)
            lin = jnp.clip(lin, -SWIGLU_LIMIT, SWIGLU_LIMIT)
            acts.append((glu * jax.nn.sigmoid(SWIGLU_ALPHA * glu) * (lin + 1.0)).astype(BF16))
        a = jnp.concatenate(acts, axis=1)
        o_ref[...] = jnp.dot(a, wd_bf[...], preferred_element_type=F32) + bd_ref[0]

    @pl.when(jnp.logical_not(active))
    def _():
        o_ref[...] = jnp.zeros_like(o_ref)


def _experts(blk_e, n_used, xbuf, layer, wu, bu, wd, bd):
    n_rows = xbuf.shape[0]
    n_blk = n_rows // EXPERT_ROWS
    grid_spec = pltpu.PrefetchScalarGridSpec(
        num_scalar_prefetch=2,
        grid=(n_blk,),
        in_specs=[
            pl.BlockSpec((EXPERT_ROWS, D_MODEL), lambda i, e, u: (i, 0)),
            pl.BlockSpec((1, 1, D_MODEL, 2 * D_EXPERT), lambda i, e, u: (layer, e[i], 0, 0)),
            pl.BlockSpec((1, 1, 2 * D_EXPERT), lambda i, e, u: (e[i], 0, 0)),
            pl.BlockSpec((1, 1, D_EXPERT, D_MODEL), lambda i, e, u: (layer, e[i], 0, 0)),
            pl.BlockSpec((1, 1, D_MODEL), lambda i, e, u: (e[i], 0, 0)),
            pl.BlockSpec((LANES, LANES), lambda i, e, u: (0, 0)),
        ],
        out_specs=pl.BlockSpec((EXPERT_ROWS, D_MODEL), lambda i, e, u: (i, 0)),
        scratch_shapes=[pltpu.VMEM((D_MODEL, 2 * D_EXPERT), BF16), pltpu.VMEM((D_EXPERT, D_MODEL), BF16)],
    )
    return pl.pallas_call(
        _expert_kernel,
        grid_spec=grid_spec,
        out_shape=jax.ShapeDtypeStruct((n_rows, D_MODEL), F32),
        compiler_params=pltpu.CompilerParams(dimension_semantics=("arbitrary",), vmem_limit_bytes=EXPERT_VMEM_LIMIT),
        name="experts",
    )(blk_e, n_used, xbuf, wu, bu, wd, bd, jnp.asarray(_pair_shuffle(), BF16))


def _combine_kernel(x_ref, gate_ref, g2_ref, lng_ref, lnb_ref, pos_hbm, y_hbm, o_ref, idx_smem, rows, sem_idx, sem_rows):
    tm = x_ref.shape[0]
    per_tile = tm * TOP_K
    i = pl.program_id(0)

    def row_copy(slot, t, k, p):
        return pltpu.make_async_copy(y_hbm.at[pl.ds(p, 1), :], rows.at[slot, k, pl.ds(t, 1), :], sem_rows.at[slot])

    def start_gather(tile, slot):
        fetch = pltpu.make_async_copy(pos_hbm.at[pl.ds(tile * per_tile, per_tile)], idx_smem, sem_idx)
        fetch.start()
        fetch.wait()

        def issue(j, carry):
            for u in range(DMA_UNROLL):
                t = j * DMA_UNROLL + u
                for k in range(TOP_K):
                    row_copy(slot, t, k, idx_smem[t * TOP_K + k]).start()
            return carry

        lax.fori_loop(0, tm // DMA_UNROLL, issue, 0)

    @pl.when(i == 0)
    def _():
        start_gather(0, 0)

    @pl.when(i + 1 < pl.num_programs(0))
    def _():
        start_gather(i + 1, (i + 1) % 2)

    slot = i % 2

    def drain(j, carry):
        for _ in range(DMA_UNROLL * TOP_K):
            row_copy(slot, 0, 0, 0).wait()
        return carry

    lax.fori_loop(0, tm // DMA_UNROLL, drain, 0)
    f = gate_ref[:, 0:1] * rows[slot, 0]
    for k in range(1, TOP_K):
        f = f + gate_ref[:, k:k + 1] * rows[slot, k]
    o_ref[...] = _layer_norm(DEEPNORM_ALPHA * x_ref[...] + g2_ref[0] * f, lng_ref[...], lnb_ref[...])


def _combine(x2, gates, pos_flat, ybuf, g2, lng, lnb, seq_rows, mod_row):
    t = x2.shape[0]
    tm = COMBINE_TILE
    row = pl.BlockSpec((tm, D_MODEL), lambda i: (i, 0))
    const = pl.BlockSpec((1, D_MODEL), lambda i: (0, 0))
    return pl.pallas_call(
        _combine_kernel,
        grid=(t // tm,),
        in_specs=[row, pl.BlockSpec((tm, LANES), lambda i: (i, 0)), _mod_spec(tm, seq_rows, mod_row), const, const,
                  pl.BlockSpec(memory_space=pl.ANY), pl.BlockSpec(memory_space=pl.ANY)],
        out_specs=row,
        out_shape=jax.ShapeDtypeStruct((t, D_MODEL), F32),
        scratch_shapes=[
            pltpu.SMEM((tm * TOP_K,), jnp.int32),
            pltpu.VMEM((2, TOP_K, tm, D_MODEL), F32),
            pltpu.SemaphoreType.DMA,
            pltpu.SemaphoreType.DMA((2,)),
        ],
        compiler_params=_params("arbitrary"),
        name="combine",
    )(x2, gates, g2, lng, lnb, pos_flat, ybuf)


def _moe(v_parts, logits, layer, wu, bu, wd, bd):
    t = logits.shape[0]
    n = t * TOP_K
    idx, gates, cnt = _route(logits)
    counts = cnt[0, :N_EXPERTS].astype(jnp.int32)
    padded = (counts + EXPERT_ROWS - 1) // EXPERT_ROWS * EXPERT_ROWS
    pad_ends = jnp.cumsum(padded)
    pad_starts = pad_ends - padded
    experts = jnp.arange(N_EXPERTS, dtype=jnp.int32)
    start = jnp.sum(jnp.where(idx[:, :TOP_K, None] == experts, pad_starts, 0), axis=-1)
    pos_flat = (start + idx[:, TOP_K:2 * TOP_K]).reshape(n)
    n_blk = n // EXPERT_ROWS + N_EXPERTS
    blk_start = jnp.arange(n_blk, dtype=jnp.int32) * EXPERT_ROWS
    blk_e = jnp.minimum(jnp.sum((pad_ends[None, :] <= blk_start[:, None]).astype(jnp.int32), axis=1), N_EXPERTS - 1)
    n_used = pad_ends[-1:] // EXPERT_ROWS
    xbuf = jnp.zeros((n_blk * EXPERT_ROWS, D_MODEL), F32)
    row0 = 0
    for v in v_parts:
        xbuf = _dispatch(v, pos_flat[row0 * TOP_K:(row0 + v.shape[0]) * TOP_K], xbuf)
        row0 += v.shape[0]
    ybuf = _experts(blk_e, n_used, xbuf, layer, wu, bu, wd, bd)
    return ybuf, pos_flat, gates


def _rope_tables(seq):
    rows = seq // GRID_W
    row = jnp.repeat(jnp.arange(rows), GRID_W).astype(F32)
    col = jnp.tile(jnp.arange(GRID_W), rows).astype(F32)
    n_freq = HEAD_DIM // 4
    inv = ROPE_BASE ** (-jnp.arange(n_freq, dtype=F32) / n_freq)
    ang = jnp.concatenate([row[:, None] * inv, col[:, None] * inv], -1)
    ang = jnp.concatenate([ang, ang], -1)
    sign = jnp.where(jnp.arange(HEAD_DIM) < HEAD_DIM // 2, -1.0, 1.0).astype(F32)
    cos = jnp.cos(ang)
    sin = jnp.sin(ang) * sign
    return jnp.tile(cos, (1, LANES // HEAD_DIM)), jnp.tile(sin, (1, LANES // HEAD_DIM))


def kernel(x, c, ctx, c_ctx, w_mod, b_mod, w_in_even, sink, w_out_even, w_in_odd, conv_w, w_out_odd, ln_g, ln_b,
           w_router, b_router, w_up, b_up, w_down, b_down):
    B, S, D = x.shape
    L = ctx.shape[1]
    T, TC = B * S, B * L
    cos, sin = _rope_tables(S)
    cond = jnp.concatenate([c, c_ctx[None, :], jnp.zeros((MOD_ROWS - B - 1, D), F32)], 0)
    mod = _modulation(cond, w_mod, b_mod)

    bu = b_up.reshape(DEPTH, N_EXPERTS, 1, 2 * D_EXPERT)
    bd = b_down.reshape(DEPTH, N_EXPERTS, 1, D)
    wr_hi = w_router.astype(BF16)
    wr_lo = (w_router - wr_hi.astype(F32)).astype(BF16)
    pad_r = ((0, 0), (0, 0), (0, LANES - N_EXPERTS))
    wr_hi, wr_lo = jnp.pad(wr_hi, pad_r), jnp.pad(wr_lo, pad_r)
    br = jnp.pad(b_router, ((0, 0), (0, LANES - N_EXPERTS)), constant_values=NEG_INF).reshape(DEPTH, 1, LANES)

    x2 = x.reshape(T, D)
    xc2 = ctx.reshape(TC, D)
    for l in range(DEPTH):
        even = l % 2 == 0
        j = l // 2
        ctx_after = any(m % 2 == 0 for m in range(l + 1, DEPTH))
        sh1, sc1, g1, sh2, sc2, g2 = [m.reshape(MOD_ROWS, 1, D) for m in jnp.split(mod[l], 6, axis=-1)]
        lng1, lnb1 = ln_g[l, 0].reshape(1, D), ln_b[l, 0].reshape(1, D)
        lng2, lnb2 = ln_g[l, 1].reshape(1, D), ln_b[l, 1].reshape(1, D)
        post = (g1, sc2, sh2, lng1, lnb1, wr_hi[l], wr_lo[l], br[l])
        if even:
            w_in = w_in_even[j].astype(BF16)
            w_out = w_out_even[j].astype(BF16)
            h = _inproj_even(x2, sc1, sh1, w_in, cos, sin, S, None, True)
            hc = _inproj_even(xc2, sc1, sh1, w_in, cos, sin, L, B, False)
            kcol = ATTN_WIDTH // KV_WIDTH
            attn = _win_attention(h, hc, sink[j], B, S, kcol, kcol + 1)
            four = _fourier_mix(h, B, S)
            x2, v, logits = _outproj("even", (attn, four), w_out, x2, *post, S, None)
            if ctx_after:
                attn_c = _ctx_attention(hc, sink[j], B)
                four_c = _fourier_mix(hc, B, L)
                xc2, vc, logits_c = _outproj("even", (attn_c, four_c), w_out, xc2, *post, L, B)
        else:
            w_in = w_in_odd[j].astype(BF16)
            w_out = w_out_odd[j].astype(BF16)
            cw = jnp.pad(conv_w[j], ((0, 8 - conv_w.shape[1]), (0, 0)))
            bz = _inproj_odd(x2, sc1, sh1, w_in, S, None)
            x2, v, logits = _outproj("odd", (bz,), w_out, x2, *post, S, None, conv_w=cw)
            if ctx_after:
                bzc = _inproj_odd(xc2, sc1, sh1, w_in, L, B)
                xc2, vc, logits_c = _outproj("odd", (bzc,), w_out, xc2, *post, L, B, conv_w=cw)
        if ctx_after:
            ybuf, pos, gates = _moe((v, vc), jnp.concatenate([logits, logits_c], 0), l, w_up, bu[l], w_down, bd[l])
            xc2 = _combine(xc2, gates[T:], pos[T * TOP_K:], ybuf, g2, lng2, lnb2, L, B)
        else:
            ybuf, pos, gates = _moe((v,), logits, l, w_up, bu[l], w_down, bd[l])
        x2 = _combine(x2, gates[:T], pos[:T * TOP_K], ybuf, g2, lng2, lnb2, S, None)
    return x2.reshape(B, S, D)
```

```python
import functools

import numpy as np
import jax
import jax.numpy as jnp
from jax import lax
from jax.experimental import pallas as pl
from jax.experimental.pallas import tpu as pltpu

D_MODEL = 1024
DEPTH = 4
GRID_W = 64
HEAD_DIM = 64
N_Q_HEADS = 12
N_KV_HEADS = 4
Q_PER_KV = N_Q_HEADS // N_KV_HEADS
BLOCK = 128
ROPE_BASE = 10000.0
FOURIER_GROUP_DIM = 64
FOURIER_WIDTH = 256
ATTN_WIDTH = N_Q_HEADS * HEAD_DIM
KV_WIDTH = N_KV_HEADS * HEAD_DIM
EVEN_IN_WIDTH = ATTN_WIDTH + 2 * KV_WIDTH + FOURIER_WIDTH
N_EXPERTS = 32
TOP_K = 4
D_EXPERT = D_MODEL
SWIGLU_LIMIT = 7.0
SWIGLU_ALPHA = 1.702
LN_EPS = 1e-5
NEG_INF = -1e30
DEEPNORM_ALPHA = (2 * DEPTH) ** 0.25

LANES = 128
SUBLANES = 8
ROW_TILE = 512
EXPERT_ROWS = 256
MOD_ROWS = 16
ROUTE_TILE = 512
DISPATCH_TILE = 512
COMBINE_TILE = 256
DMA_UNROLL = 8
VMEM_LIMIT = 48 * 1024 * 1024
EXPERT_VMEM_LIMIT = 56 * 1024 * 1024

F32 = jnp.float32
BF16 = jnp.bfloat16


def _params(*sem):
    return pltpu.CompilerParams(dimension_semantics=sem, vmem_limit_bytes=VMEM_LIMIT)


def _mod_spec(tm, seq_rows, mod_row):
    if mod_row is None:
        return pl.BlockSpec((1, 1, D_MODEL), lambda i: ((i * tm) // seq_rows, 0, 0))
    return pl.BlockSpec((1, 1, D_MODEL), lambda i: (mod_row, 0, 0))


def _mod_kernel(c_ref, w_ref, b_ref, o_ref):
    c = c_ref[...]
    s = (c * jax.nn.sigmoid(c)).astype(BF16)
    o_ref[0] = jnp.dot(s, w_ref[0].astype(BF16), preferred_element_type=F32) + b_ref[0]


def _modulation(cond, w_mod, b_mod):
    tn = 1536
    n = w_mod.shape[-1]
    return pl.pallas_call(
        _mod_kernel,
        grid=(DEPTH, n // tn),
        in_specs=[
            pl.BlockSpec((MOD_ROWS, D_MODEL), lambda l, j: (0, 0)),
            pl.BlockSpec((1, D_MODEL, tn), lambda l, j: (l, 0, j)),
            pl.BlockSpec((1, 1, tn), lambda l, j: (l, 0, j)),
        ],
        out_specs=pl.BlockSpec((1, MOD_ROWS, tn), lambda l, j: (l, 0, j)),
        out_shape=jax.ShapeDtypeStruct((DEPTH, MOD_ROWS, n), F32),
        compiler_params=_params("arbitrary", "arbitrary"),
        name="modulation",
    )(cond, w_mod, b_mod.reshape(DEPTH, 1, n))


def _rope_chunk(h, cos, sin_signed, first_half):
    swapped = jnp.where(first_half, pltpu.roll(h, LANES - HEAD_DIM // 2, axis=1), pltpu.roll(h, HEAD_DIM // 2, axis=1))
    return h * cos + swapped * sin_signed


def _inproj_even_kernel(x_ref, sc_ref, sh_ref, w_ref, cos_ref, sin_ref, o_ref, *, rope):
    u = (x_ref[...] * (1.0 + sc_ref[0]) + sh_ref[0]).astype(BF16)
    tn = 512
    if rope:
        cos = cos_ref[...]
        sin = sin_ref[...]
        lane = lax.broadcasted_iota(jnp.int32, cos.shape, 1)
        first_half = (lane % HEAD_DIM) < HEAD_DIM // 2
    scale = HEAD_DIM ** -0.5
    for jt in range(EVEN_IN_WIDTH // tn):
        acc = jnp.dot(u, w_ref[:, jt * tn:(jt + 1) * tn], preferred_element_type=F32)
        for k in range(tn // LANES):
            col = jt * tn + k * LANES
            h = acc[:, k * LANES:(k + 1) * LANES]
            if rope and col < ATTN_WIDTH + KV_WIDTH:
                h = _rope_chunk(h, cos, sin, first_half)
            if col < ATTN_WIDTH:
                h = h * scale
            o_ref[:, col:col + LANES] = h.astype(BF16)


def _inproj_even(x2, sc, sh, w, cos, sin, seq_rows, mod_row, rope):
    t = x2.shape[0]
    tm = min(ROW_TILE, seq_rows)
    pos_tiles = cos.shape[0] // tm
    mod = _mod_spec(tm, seq_rows, mod_row)

    return pl.pallas_call(
        functools.partial(_inproj_even_kernel, rope=rope),
        grid=(t // tm,),
        in_specs=[
            pl.BlockSpec((tm, D_MODEL), lambda i: (i, 0)),
            mod,
            mod,
            pl.BlockSpec((D_MODEL, EVEN_IN_WIDTH), lambda i: (0, 0)),
            pl.BlockSpec((tm, LANES), lambda i: (i % pos_tiles, 0)),
            pl.BlockSpec((tm, LANES), lambda i: (i % pos_tiles, 0)),
        ],
        out_specs=pl.BlockSpec((tm, EVEN_IN_WIDTH), lambda i: (i, 0)),
        out_shape=jax.ShapeDtypeStruct((t, EVEN_IN_WIDTH), BF16),
        compiler_params=_params("arbitrary"),
        name="inproj_even",
    )(x2, sc, sh, w, cos, sin)


def _attend(q_ref, k_all, v_all, valid, sink_ref, o_ref):
    rows = q_ref.shape[0]
    lane = lax.broadcasted_iota(jnp.int32, (rows, LANES), 1)
    low = lane < HEAD_DIM
    for kv in range(N_KV_HEADS):
        kv_chunk, kv_half = divmod(kv, 2)
        keep = low if kv_half == 0 else jnp.logical_not(low)
        stack, sinks = [], []
        for g in range(Q_PER_KV):
            head = kv * Q_PER_KV + g
            chunk, half = divmod(head, 2)
            qh = q_ref[:, chunk * LANES:(chunk + 1) * LANES].astype(F32)
            if half != kv_half:
                qh = pltpu.roll(qh, HEAD_DIM, axis=1)
            stack.append(jnp.where(keep, qh, 0.0).astype(BF16))
            sinks.append(jnp.full((rows, 1), sink_ref[head], F32))
        qs = jnp.concatenate(stack, axis=0)
        sk = jnp.concatenate(sinks, axis=0)
        kc = k_all[:, kv_chunk * LANES:(kv_chunk + 1) * LANES]
        vc = v_all[:, kv_chunk * LANES:(kv_chunk + 1) * LANES]
        s = lax.dot_general(qs, kc, (((1,), (1,)), ((), ())), preferred_element_type=F32)
        if valid is not None:
            s = jnp.where(valid, s, NEG_INF)
        m = jnp.maximum(sk, jnp.max(s, axis=-1, keepdims=True))
        e = jnp.exp(s - m)
        denom = jnp.exp(sk - m) + jnp.sum(e, axis=-1, keepdims=True)
        o = jnp.dot(e.astype(BF16), vc, preferred_element_type=F32) / denom
        for g in range(Q_PER_KV):
            head = kv * Q_PER_KV + g
            half = head % 2
            oh = o[g * rows:(g + 1) * rows]
            if half != kv_half:
                oh = pltpu.roll(oh, HEAD_DIM, axis=1)
            o_ref[:, head * HEAD_DIM:(head + 1) * HEAD_DIM] = oh[:, half * HEAD_DIM:(half + 1) * HEAD_DIM].astype(BF16)


def _win_attn_kernel(sink_ref, q_ref, kp_ref, kc_ref, kn_ref, vp_ref, vc_ref, vn_ref, kx_ref, vx_ref, o_ref, *, nb):
    n = pl.program_id(1)
    k_all = jnp.concatenate([kp_ref[...], kc_ref[...], kn_ref[...], kx_ref[...]], axis=0)
    v_all = jnp.concatenate([vp_ref[...], vc_ref[...], vn_ref[...], vx_ref[...]], axis=0)
    n_keys = k_all.shape[0]
    r = lax.broadcasted_iota(jnp.int32, (Q_PER_KV * BLOCK, n_keys), 0) % BLOCK
    c = lax.broadcasted_iota(jnp.int32, (Q_PER_KV * BLOCK, n_keys), 1)
    lo = jnp.where(n > 0, r, BLOCK)
    hi = jnp.where(n < nb - 1, r, -1)
    valid = (c >= lo) & ((c < 2 * BLOCK) | (c >= 3 * BLOCK) | (c - 2 * BLOCK <= hi))
    _attend(q_ref, k_all, v_all, valid, sink_ref, o_ref)


def _win_attention(h, hc, sink, batch, seq, kc_col, vc_col):
    nb = seq // BLOCK
    qcol, kcol, vcol = 0, ATTN_WIDTH // KV_WIDTH, ATTN_WIDTH // KV_WIDTH + 1

    def blk(shift, col):
        return pl.BlockSpec((BLOCK, KV_WIDTH), lambda b, n: (b * nb + jnp.clip(n + shift, 0, nb - 1), col))

    return pl.pallas_call(
        functools.partial(_win_attn_kernel, nb=nb),
        grid=(batch, nb),
        in_specs=[
            pl.BlockSpec(memory_space=pltpu.SMEM),
            pl.BlockSpec((BLOCK, ATTN_WIDTH), lambda b, n: (b * nb + n, qcol)),
            blk(-1, kcol), blk(0, kcol), blk(1, kcol),
            blk(-1, vcol), blk(0, vcol), blk(1, vcol),
            pl.BlockSpec((hc.shape[0] // batch, KV_WIDTH), lambda b, n: (b, kc_col)),
            pl.BlockSpec((hc.shape[0] // batch, KV_WIDTH), lambda b, n: (b, vc_col)),
        ],
        out_specs=pl.BlockSpec((BLOCK, ATTN_WIDTH), lambda b, n: (b * nb + n, 0)),
        out_shape=jax.ShapeDtypeStruct((batch * seq, ATTN_WIDTH), BF16),
        compiler_params=_params("arbitrary", "arbitrary"),
        name="window_attention",
    )(sink, h, h, h, h, h, h, h, hc, hc)


def _ctx_attn_kernel(sink_ref, q_ref, k_ref, v_ref, o_ref):
    _attend(q_ref, k_ref[...], v_ref[...], None, sink_ref, o_ref)


def _ctx_attention(hc, sink, batch):
    ctx_len = hc.shape[0] // batch
    kcol, vcol = ATTN_WIDTH // KV_WIDTH, ATTN_WIDTH // KV_WIDTH + 1
    return pl.pallas_call(
        _ctx_attn_kernel,
        grid=(batch,),
        in_specs=[
            pl.BlockSpec(memory_space=pltpu.SMEM),
            pl.BlockSpec((ctx_len, ATTN_WIDTH), lambda b: (b, 0)),
            pl.BlockSpec((ctx_len, KV_WIDTH), lambda b: (b, kcol)),
            pl.BlockSpec((ctx_len, KV_WIDTH), lambda b: (b, vcol)),
        ],
        out_specs=pl.BlockSpec((ctx_len, ATTN_WIDTH), lambda b: (b, 0)),
        out_shape=jax.ShapeDtypeStruct((hc.shape[0], ATTN_WIDTH), BF16),
        compiler_params=_params("arbitrary"),
        name="context_attention",
    )(sink, hc, hc, hc)


def _dft_mats(n):
    k = np.arange(n, dtype=np.int64)
    ang = 2.0 * np.pi * ((k[:, None] * k[None, :]) % n).astype(np.float64) / n
    return np.cos(ang), np.sin(ang)


def _group_dft():
    c, s = _dft_mats(FOURIER_GROUP_DIM)
    groups = FOURIER_WIDTH // FOURIER_GROUP_DIM
    eye = np.eye(groups)
    return np.concatenate([np.kron(eye, c), np.kron(eye, s)], axis=1)


def _fourier_kernel(f_ref, w1_ref, c_ref, s_ref, o_ref, ab_ref, *, scale):
    @pl.when(pl.program_id(1) == 0)
    def _():
        ab_ref[...] = jnp.dot(f_ref[...], w1_ref[...], preferred_element_type=F32).astype(BF16)

    y = (jnp.dot(c_ref[...], ab_ref[:, :FOURIER_WIDTH], preferred_element_type=F32)
         - jnp.dot(s_ref[...], ab_ref[:, FOURIER_WIDTH:], preferred_element_type=F32))
    o_ref[...] = (y * scale).astype(BF16)


def _fourier_mix(h, batch, n):
    tr = 256
    cn, sn = _dft_mats(n)
    fcol = (EVEN_IN_WIDTH - FOURIER_WIDTH) // FOURIER_WIDTH
    return pl.pallas_call(
        functools.partial(_fourier_kernel, scale=float((n * FOURIER_GROUP_DIM) ** -0.5)),
        grid=(batch, n // tr),
        in_specs=[
            pl.BlockSpec((n, FOURIER_WIDTH), lambda b, i: (b, fcol)),
            pl.BlockSpec((FOURIER_WIDTH, 2 * FOURIER_WIDTH), lambda b, i: (0, 0)),
            pl.BlockSpec((tr, n), lambda b, i: (i, 0)),
            pl.BlockSpec((tr, n), lambda b, i: (i, 0)),
        ],
        out_specs=pl.BlockSpec((tr, FOURIER_WIDTH), lambda b, i: (b * (n // tr) + i, 0)),
        out_shape=jax.ShapeDtypeStruct((batch * n, FOURIER_WIDTH), BF16),
        scratch_shapes=[pltpu.VMEM((n, 2 * FOURIER_WIDTH), BF16)],
        compiler_params=_params("arbitrary", "arbitrary"),
        name="fourier_mix",
    )(h, jnp.asarray(_group_dft(), BF16), jnp.asarray(cn, BF16), jnp.asarray(sn, BF16))


def _inproj_odd_kernel(x_ref, sc_ref, sh_ref, w_ref, o_ref):
    u = (x_ref[...] * (1.0 + sc_ref[0]) + sh_ref[0]).astype(BF16)
    tn = 512
    for jt in range(D_MODEL // tn):
        cols = slice(jt * tn, (jt + 1) * tn)
        b_gate = jnp.dot(u, w_ref[:, jt * tn:(jt + 1) * tn], preferred_element_type=F32)
        o_ref[:, cols] = b_gate.astype(BF16)
        c_gate = jnp.dot(u, w_ref[:, D_MODEL + jt * tn:D_MODEL + (jt + 1) * tn], preferred_element_type=F32)
        hh = jnp.dot(u, w_ref[:, 2 * D_MODEL + jt * tn:2 * D_MODEL + (jt + 1) * tn], preferred_element_type=F32)
        o_ref[:, D_MODEL + jt * tn:D_MODEL + (jt + 1) * tn] = (c_gate * hh).astype(BF16)


def _inproj_odd(x2, sc, sh, w, seq_rows, mod_row):
    t = x2.shape[0]
    tm = min(ROW_TILE, seq_rows)
    mod = _mod_spec(tm, seq_rows, mod_row)

    return pl.pallas_call(
        _inproj_odd_kernel,
        grid=(t // tm,),
        in_specs=[
            pl.BlockSpec((tm, D_MODEL), lambda i: (i, 0)),
            mod,
            mod,
            pl.BlockSpec((D_MODEL, 3 * D_MODEL), lambda i: (0, 0)),
        ],
        out_specs=pl.BlockSpec((tm, 2 * D_MODEL), lambda i: (i, 0)),
        out_shape=jax.ShapeDtypeStruct((t, 2 * D_MODEL), BF16),
        compiler_params=_params("arbitrary"),
        name="inproj_odd",
    )(x2, sc, sh, w)


def _layer_norm(r, g, b):
    mu = jnp.mean(r, axis=-1, keepdims=True)
    d = r - mu
    var = jnp.mean(d * d, axis=-1, keepdims=True)
    return d * lax.rsqrt(var + LN_EPS) * g + b


def _post_mixer(y, x_ref, g1_ref, sc2_ref, sh2_ref, lng_ref, lnb_ref, wrh_ref, wrl_ref, br_ref, xo_ref, v_ref, lg_ref):
    xn = _layer_norm(DEEPNORM_ALPHA * x_ref[...] + g1_ref[0] * y, lng_ref[...], lnb_ref[...])
    xo_ref[...] = xn
    v = xn * (1.0 + sc2_ref[0]) + sh2_ref[0]
    v_ref[...] = v
    v_hi = v.astype(BF16)
    v_lo = (v - v_hi.astype(F32)).astype(BF16)
    lg_ref[...] = (jnp.dot(v_hi, wrh_ref[...], preferred_element_type=F32)
                   + (jnp.dot(v_lo, wrh_ref[...], preferred_element_type=F32)
                      + jnp.dot(v_hi, wrl_ref[...], preferred_element_type=F32))) + br_ref[...]


def _outproj_even_kernel(a_ref, f_ref, w_ref, *rest):
    y = (jnp.dot(a_ref[...], w_ref[:ATTN_WIDTH, :], preferred_element_type=F32)
         + jnp.dot(f_ref[...], w_ref[ATTN_WIDTH:, :], preferred_element_type=F32))
    _post_mixer(y, *rest)


def _outproj_odd_kernel(b_ref, z_ref, zp_ref, zn_ref, cw_ref, w_ref, *rest, rows_per_seq):
    tm = z_ref.shape[0]
    halo = zp_ref.shape[0]
    row0 = pl.program_id(0) * tm
    z = z_ref[...].astype(F32)
    prev_row = jnp.where(row0 % rows_per_seq == 0, 0.0, zp_ref[halo - 1:halo, :].astype(F32))
    next_row = jnp.where((row0 + tm) % rows_per_seq == 0, 0.0, zn_ref[0:1, :].astype(F32))
    r = lax.broadcasted_iota(jnp.int32, z.shape, 0)
    z_prev = jnp.where(r == 0, prev_row, pltpu.roll(z, 1, axis=0))
    z_next = jnp.where(r == tm - 1, next_row, pltpu.roll(z, tm - 1, axis=0))
    conv = cw_ref[0:1, :] * z_prev + cw_ref[1:2, :] * z + cw_ref[2:3, :] * z_next
    a = (b_ref[...].astype(F32) * conv).astype(BF16)
    y = jnp.dot(a, w_ref[...], preferred_element_type=F32)
    _post_mixer(y, *rest)


def _outproj(kind, mix_inputs, w_out, x2, g1, sc2, sh2, lng, lnb, wr_hi, wr_lo, br, seq_rows, mod_row, conv_w=None):
    t = x2.shape[0]
    tm = min(ROW_TILE, seq_rows)
    mod = _mod_spec(tm, seq_rows, mod_row)

    row = lambda width: pl.BlockSpec((tm, width), lambda i: (i, 0))
    const = lambda shape: pl.BlockSpec(shape, lambda i: (0,) * len(shape))
    if kind == "even":
        attn, four = mix_inputs
        kern = _outproj_even_kernel
        head_specs = [row(ATTN_WIDTH), row(FOURIER_WIDTH)]
        head_args = [attn, four]
    else:
        (bz,) = mix_inputs
        halo = 16
        last = t // halo - 1
        kern = functools.partial(_outproj_odd_kernel, rows_per_seq=seq_rows)
        head_specs = [
            pl.BlockSpec((tm, D_MODEL), lambda i: (i, 0)),
            pl.BlockSpec((tm, D_MODEL), lambda i: (i, 1)),
            pl.BlockSpec((halo, D_MODEL), lambda i: (jnp.maximum(i * (tm // halo) - 1, 0), 1)),
            pl.BlockSpec((halo, D_MODEL), lambda i: (jnp.minimum((i + 1) * (tm // halo), last), 1)),
            const((8, D_MODEL)),
        ]
        head_args = [bz, bz, bz, bz, conv_w]
    return pl.pallas_call(
        kern,
        grid=(t // tm,),
        in_specs=head_specs + [
            const((D_MODEL, D_MODEL)),
            row(D_MODEL),
            mod,
            mod,
            mod,
            const((1, D_MODEL)), const((1, D_MODEL)),
            const((D_MODEL, LANES)), const((D_MODEL, LANES)), const((1, LANES)),
        ],
        out_specs=[row(D_MODEL), row(D_MODEL), row(LANES)],
        out_shape=[
            jax.ShapeDtypeStruct((t, D_MODEL), F32),
            jax.ShapeDtypeStruct((t, D_MODEL), F32),
            jax.ShapeDtypeStruct((t, LANES), F32),
        ],
        compiler_params=_params("arbitrary"),
        name="outproj_" + kind,
    )(*head_args, w_out, x2, g1, sc2, sh2, lng, lnb, wr_hi, wr_lo, br)


def _route_kernel(lg_ref, tri_ref, idx_ref, gate_ref, cnt_ref, carry_ref):
    @pl.when(pl.program_id(0) == 0)
    def _():
        carry_ref[...] = jnp.zeros_like(carry_ref)

    work = lg_ref[...]
    lane = lax.broadcasted_iota(jnp.int32, work.shape, 1).astype(F32)
    sels, vals, ids = [], [], []
    for _ in range(TOP_K):
        m = jnp.max(work, axis=1, keepdims=True)
        first = jnp.min(jnp.where(work == m, lane, float(LANES)), axis=1, keepdims=True)
        sel = lane == first
        work = jnp.where(sel, -jnp.inf, work)
        sels.append(sel)
        vals.append(m)
        ids.append(first)
    onehot = jnp.where(sels[0] | sels[1] | sels[2] | sels[3], 1.0, 0.0)
    before = jnp.dot(tri_ref[...], onehot.astype(BF16), preferred_element_type=F32) + carry_ref[...]
    carry_ref[...] = carry_ref[...] + jnp.sum(onehot, axis=0, keepdims=True)
    cnt_ref[...] = carry_ref[...]
    exps = [jnp.exp(v - vals[0]) for v in vals]
    denom = exps[0] + exps[1] + exps[2] + exps[3]
    idx_out = jnp.zeros_like(work)
    gate_out = jnp.zeros_like(work)
    for k in range(TOP_K):
        rank = jnp.sum(jnp.where(sels[k], before, 0.0), axis=1, keepdims=True)
        idx_out = jnp.where(lane == k, ids[k], jnp.where(lane == TOP_K + k, rank, idx_out))
        gate_out = jnp.where(lane == k, exps[k] / denom, gate_out)
    idx_ref[...] = idx_out.astype(jnp.int32)
    gate_ref[...] = gate_out


def _route(logits):
    t = logits.shape[0]
    tm = ROUTE_TILE
    tri = jnp.asarray(np.tril(np.ones((tm, tm), np.float32), -1), BF16)
    row = pl.BlockSpec((tm, LANES), lambda i: (i, 0))
    return pl.pallas_call(
        _route_kernel,
        grid=(t // tm,),
        in_specs=[row, pl.BlockSpec((tm, tm), lambda i: (0, 0))],
        out_specs=[row, row, pl.BlockSpec((1, LANES), lambda i: (0, 0))],
        out_shape=[
            jax.ShapeDtypeStruct((t, LANES), jnp.int32),
            jax.ShapeDtypeStruct((t, LANES), F32),
            jax.ShapeDtypeStruct((1, LANES), F32),
        ],
        scratch_shapes=[pltpu.VMEM((1, LANES), F32)],
        compiler_params=_params("arbitrary"),
        name="route",
    )(logits, tri)


def _zero_fill(fill_start_ref, fill_len_ref, n_used_ref, buf_hbm, zero_vmem, sem_fill):
    n_blk = buf_hbm.shape[0] // EXPERT_ROWS
    zero_vmem[...] = jnp.zeros_like(zero_vmem)
    chunks = [1 << b for b in reversed(range(SUBLANES.bit_length() - 1, EXPERT_ROWS.bit_length() - 1))]

    def pads(wait):
        def body(e, carry):
            start, length = fill_start_ref[e], fill_len_ref[e]
            head = (-start) % SUBLANES
            for r in range(SUBLANES - 1):
                @pl.when(r < head)
                def _():
                    cp = pltpu.make_async_copy(zero_vmem.at[pl.ds(0, 1), :], buf_hbm.at[pl.ds(start + r, 1), :], sem_fill)
                    cp.wait() if wait else cp.start()
            start, length = start + head, length - head
            for c in chunks:
                @pl.when((length & c) != 0)
                def _():
                    row = pl.multiple_of(start + (length & (-2 * c)), SUBLANES)
                    cp = pltpu.make_async_copy(zero_vmem.at[pl.ds(0, c), :], buf_hbm.at[pl.ds(row, c), :], sem_fill)
                    cp.wait() if wait else cp.start()
            return carry

        lax.fori_loop(0, N_EXPERTS, body, 0)

    def tail(wait):
        def body(j, carry):
            row = pl.multiple_of(j * EXPERT_ROWS, EXPERT_ROWS)
            cp = pltpu.make_async_copy(zero_vmem, buf_hbm.at[pl.ds(row, EXPERT_ROWS), :], sem_fill)
            cp.wait() if wait else cp.start()
            return carry

        lax.fori_loop(n_used_ref[0], n_blk, body, 0)

    pads(False)
    tail(False)
    pads(True)
    tail(True)


def _dispatch_kernel(fill_start_ref, fill_len_ref, n_used_ref, *rest, tiles):
    v_refs = rest[:len(tiles)]
    pos_hbm, buf_hbm, idx_smem, sem_idx, sem_rows, zero_vmem, sem_fill = rest[len(tiles):]
    tm = v_refs[0].shape[0]
    per_tile = tm * TOP_K
    i = pl.program_id(0)
    fetch = pltpu.make_async_copy(pos_hbm.at[pl.ds(i * per_tile, per_tile)], idx_smem, sem_idx)
    fetch.start()

    @pl.when(i == 0)
    def _():
        _zero_fill(fill_start_ref, fill_len_ref, n_used_ref, buf_hbm, zero_vmem, sem_fill)

    fetch.wait()

    def scatter(v_ref):
        def row_copy(t, p):
            return pltpu.make_async_copy(v_ref.at[pl.ds(t, 1), :], buf_hbm.at[pl.ds(p, 1), :], sem_rows)

        def issue(j, carry):
            for u in range(DMA_UNROLL):
                t = j * DMA_UNROLL + u
                for k in range(TOP_K):
                    row_copy(t, idx_smem[t * TOP_K + k]).start(priority=k % 2)
            return carry

        def drain(j, carry):
            for _ in range(DMA_UNROLL * TOP_K):
                row_copy(0, 0).wait()
            return carry

        lax.fori_loop(0, tm // DMA_UNROLL, issue, 0)
        lax.fori_loop(0, tm // DMA_UNROLL, drain, 0)

    first = 0
    for v_ref, n_tiles in zip(v_refs, tiles):
        @pl.when((i >= first) & (i < first + n_tiles))
        def _():
            scatter(v_ref)

        first += n_tiles


def _dispatch(fill_start, fill_len, n_used, v_parts, pos_flat, n_rows):
    tm = DISPATCH_TILE
    tiles = tuple(v.shape[0] // tm for v in v_parts)
    firsts = [sum(tiles[:p]) for p in range(len(tiles))]
    any_spec = pl.BlockSpec(memory_space=pl.ANY)

    def part_spec(first, n_tiles):
        return pl.BlockSpec((tm, D_MODEL), lambda i, *_: (jnp.clip(i - first, 0, n_tiles - 1), 0))

    grid_spec = pltpu.PrefetchScalarGridSpec(
        num_scalar_prefetch=3,
        grid=(sum(tiles),),
        in_specs=[part_spec(f, n) for f, n in zip(firsts, tiles)] + [any_spec],
        out_specs=any_spec,
        scratch_shapes=[pltpu.SMEM((tm * TOP_K,), jnp.int32), pltpu.SemaphoreType.DMA, pltpu.SemaphoreType.DMA,
                        pltpu.VMEM((EXPERT_ROWS, D_MODEL), F32), pltpu.SemaphoreType.DMA],
    )
    return pl.pallas_call(
        functools.partial(_dispatch_kernel, tiles=tiles),
        grid_spec=grid_spec,
        out_shape=jax.ShapeDtypeStruct((n_rows, D_MODEL), F32),
        compiler_params=_params("arbitrary"),
        name="dispatch",
    )(fill_start, fill_len, n_used, *v_parts, pos_flat)


def _pair_shuffle():
    l = np.arange(LANES)
    p = np.zeros((LANES, LANES), np.float32)
    p[l, l // 2 + (LANES // 2) * (l % 2)] = 1.0
    return p


def _expert_kernel(blk_e_ref, n_used_ref, run_ref, next_ref, x_ref, bu_ref, bd_ref, perm_ref, wu_hbm, wd_hbm, o_ref,
                   wu_f32, wd_f32, wu_bf, wd_bf, sem_u, sem_d, *, layer):
    i = pl.program_id(0)
    active = i < n_used_ref[0]
    expert = blk_e_ref[i]
    new_expert = (i == 0) | (expert != blk_e_ref[jnp.maximum(i - 1, 0)])

    def fetch(e, slot):
        return (pltpu.make_async_copy(wu_hbm.at[layer, e], wu_f32.at[slot], sem_u.at[slot]),
                pltpu.make_async_copy(wd_hbm.at[layer, e], wd_f32.at[slot], sem_d.at[slot]))

    @pl.when(active & new_expert)
    def _():
        slot = run_ref[expert] % 2

        @pl.when(i == 0)
        def _():
            for cp in fetch(expert, slot):
                cp.start()

        for cp in fetch(expert, slot):
            cp.wait()
        nxt = next_ref[expert]

        @pl.when(nxt >= 0)
        def _():
            for cp in fetch(nxt, 1 - slot):
                cp.start()

        def cast(c, carry):
            r = pl.multiple_of(c * LANES, LANES)
            wu_bf[pl.ds(r, LANES), :] = wu_f32[slot, pl.ds(r, LANES), :].astype(BF16)
            wd_bf[pl.ds(r, LANES), :] = jnp.dot(perm_ref[...], wd_f32[slot, pl.ds(r, LANES), :].astype(BF16),
                                               preferred_element_type=F32).astype(BF16)
            return carry

        lax.fori_loop(0, D_MODEL // LANES, cast, 0)

    @pl.when(active)
    def _():
        h = jnp.dot(x_ref[...].astype(BF16), wu_bf[...], preferred_element_type=F32) + bu_ref[0]
        even = lax.broadcasted_iota(jnp.int32, (x_ref.shape[0], LANES), 1) % 2 == 0
        acts = []
        for k in range(D_EXPERT // LANES):
            ha = h[:, 2 * k * LANES:(2 * k + 1) * LANES]
            hb = h[:, (2 * k + 1) * LANES:(2 * k + 2) * LANES]
            glu = jnp.where(even, ha, pltpu.roll(hb, 1, axis=1))
            lin = jnp.where(even, pltpu.roll(ha, LANES - 1, axis=1), hb)
            glu = jnp.minimum(glu, SWIGLU_LIMIT)
            lin = jnp.clip(lin, -SWIGLU_LIMIT, SWIGLU_LIMIT)
            acts.append((glu * jax.nn.sigmoid(SWIGLU_ALPHA * glu) * (lin + 1.0)).astype(BF16))
        a = jnp.concatenate(acts, axis=1)
        o_ref[...] = jnp.dot(a, wd_bf[...], preferred_element_type=F32) + bd_ref[0]

    @pl.when(jnp.logical_not(active))
    def _():
        o_ref[...] = jnp.zeros_like(o_ref)


def _experts(blk_e, n_used, run_idx, next_expert, xbuf, layer, wu, bu, wd, bd):
    n_rows = xbuf.shape[0]
    n_blk = n_rows // EXPERT_ROWS
    grid_spec = pltpu.PrefetchScalarGridSpec(
        num_scalar_prefetch=4,
        grid=(n_blk,),
        in_specs=[
            pl.BlockSpec((EXPERT_ROWS, D_MODEL), lambda i, e, *_: (i, 0)),
            pl.BlockSpec((1, 1, 2 * D_EXPERT), lambda i, e, *_: (e[i], 0, 0)),
            pl.BlockSpec((1, 1, D_MODEL), lambda i, e, *_: (e[i], 0, 0)),
            pl.BlockSpec((LANES, LANES), lambda i, e, *_: (0, 0)),
            pl.BlockSpec(memory_space=pl.ANY),
            pl.BlockSpec(memory_space=pl.ANY),
        ],
        out_specs=pl.BlockSpec((EXPERT_ROWS, D_MODEL), lambda i, e, *_: (i, 0)),
        scratch_shapes=[
            pltpu.VMEM((2, D_MODEL, 2 * D_EXPERT), F32),
            pltpu.VMEM((2, D_EXPERT, D_MODEL), F32),
            pltpu.VMEM((D_MODEL, 2 * D_EXPERT), BF16),
            pltpu.VMEM((D_EXPERT, D_MODEL), BF16),
            pltpu.SemaphoreType.DMA((2,)),
            pltpu.SemaphoreType.DMA((2,)),
        ],
    )
    return pl.pallas_call(
        functools.partial(_expert_kernel, layer=layer),
        grid_spec=grid_spec,
        out_shape=jax.ShapeDtypeStruct((n_rows, D_MODEL), F32),
        compiler_params=pltpu.CompilerParams(dimension_semantics=("arbitrary",), vmem_limit_bytes=EXPERT_VMEM_LIMIT),
        name="experts",
    )(blk_e, n_used, run_idx, next_expert, xbuf, bu, bd, jnp.asarray(_pair_shuffle(), BF16), wu, wd)


def _combine_kernel(x_ref, gate_ref, g2_ref, lng_ref, lnb_ref, pos_hbm, y_hbm, o_ref, idx_smem, rows, sem_idx, sem_rows):
    tm = x_ref.shape[0]
    per_tile = tm * TOP_K
    i = pl.program_id(0)

    def row_copy(slot, t, k, p):
        return pltpu.make_async_copy(y_hbm.at[pl.ds(p, 1), :], rows.at[slot, k, pl.ds(t, 1), :], sem_rows.at[slot])

    def start_gather(tile, slot):
        fetch = pltpu.make_async_copy(pos_hbm.at[pl.ds(tile * per_tile, per_tile)], idx_smem, sem_idx)
        fetch.start()
        fetch.wait()

        def issue(j, carry):
            for u in range(DMA_UNROLL):
                t = j * DMA_UNROLL + u
                for k in range(TOP_K):
                    row_copy(slot, t, k, idx_smem[t * TOP_K + k]).start(priority=k % 2)
            return carry

        lax.fori_loop(0, tm // DMA_UNROLL, issue, 0)

    @pl.when(i == 0)
    def _():
        start_gather(0, 0)

    @pl.when(i + 1 < pl.num_programs(0))
    def _():
        start_gather(i + 1, (i + 1) % 2)

    slot = i % 2

    def drain(j, carry):
        for _ in range(DMA_UNROLL * TOP_K):
            row_copy(slot, 0, 0, 0).wait()
        return carry

    lax.fori_loop(0, tm // DMA_UNROLL, drain, 0)
    f = gate_ref[:, 0:1] * rows[slot, 0]
    for k in range(1, TOP_K):
        f = f + gate_ref[:, k:k + 1] * rows[slot, k]
    o_ref[...] = _layer_norm(DEEPNORM_ALPHA * x_ref[...] + g2_ref[0] * f, lng_ref[...], lnb_ref[...])


def _combine(x2, gates, pos_flat, ybuf, g2, lng, lnb, seq_rows, mod_row):
    t = x2.shape[0]
    tm = COMBINE_TILE
    row = pl.BlockSpec((tm, D_MODEL), lambda i: (i, 0))
    const = pl.BlockSpec((1, D_MODEL), lambda i: (0, 0))
    return pl.pallas_call(
        _combine_kernel,
        grid=(t // tm,),
        in_specs=[row, pl.BlockSpec((tm, LANES), lambda i: (i, 0)), _mod_spec(tm, seq_rows, mod_row), const, const,
                  pl.BlockSpec(memory_space=pl.ANY), pl.BlockSpec(memory_space=pl.ANY)],
        out_specs=row,
        out_shape=jax.ShapeDtypeStruct((t, D_MODEL), F32),
        scratch_shapes=[
            pltpu.SMEM((tm * TOP_K,), jnp.int32),
            pltpu.VMEM((2, TOP_K, tm, D_MODEL), F32),
            pltpu.SemaphoreType.DMA,
            pltpu.SemaphoreType.DMA((2,)),
        ],
        compiler_params=_params("arbitrary"),
        name="combine",
    )(x2, gates, g2, lng, lnb, pos_flat, ybuf)


def _moe(v_parts, logits, layer, wu, bu, wd, bd):
    t = logits.shape[0]
    n = t * TOP_K
    idx, gates, cnt = _route(logits)
    counts = cnt[0, :N_EXPERTS].astype(jnp.int32)
    padded = (counts + EXPERT_ROWS - 1) // EXPERT_ROWS * EXPERT_ROWS
    pad_ends = jnp.cumsum(padded)
    pad_starts = pad_ends - padded
    experts = jnp.arange(N_EXPERTS, dtype=jnp.int32)
    start = jnp.sum(jnp.where(idx[:, :TOP_K, None] == experts, pad_starts, 0), axis=-1)
    pos_flat = (start + idx[:, TOP_K:2 * TOP_K]).reshape(n)
    n_blk = n // EXPERT_ROWS + N_EXPERTS
    blk_start = jnp.arange(n_blk, dtype=jnp.int32) * EXPERT_ROWS
    blk_e = jnp.minimum(jnp.sum((pad_ends[None, :] <= blk_start[:, None]).astype(jnp.int32), axis=1), N_EXPERTS - 1)
    n_used = pad_ends[-1:] // EXPERT_ROWS
    nonempty = counts > 0
    run_idx = jnp.cumsum(nonempty.astype(jnp.int32)) - 1
    later = nonempty[None, :] & (experts[None, :] > experts[:, None])
    next_expert = jnp.min(jnp.where(later, experts[None, :], N_EXPERTS), axis=1)
    next_expert = jnp.where(next_expert == N_EXPERTS, -1, next_expert)
    fill_start, fill_len = pad_starts + counts, padded - counts
    xbuf = _dispatch(fill_start, fill_len, n_used, v_parts, pos_flat, n_blk * EXPERT_ROWS)
    ybuf = _experts(blk_e, n_used, run_idx, next_expert, xbuf, layer, wu, bu, wd, bd)
    return ybuf, pos_flat, gates


def _rope_tables(seq):
    rows = seq // GRID_W
    row = jnp.repeat(jnp.arange(rows), GRID_W).astype(F32)
    col = jnp.tile(jnp.arange(GRID_W), rows).astype(F32)
    n_freq = HEAD_DIM // 4
    inv = ROPE_BASE ** (-jnp.arange(n_freq, dtype=F32) / n_freq)
    ang = jnp.concatenate([row[:, None] * inv, col[:, None] * inv], -1)
    ang = jnp.concatenate([ang, ang], -1)
    sign = jnp.where(jnp.arange(HEAD_DIM) < HEAD_DIM // 2, -1.0, 1.0).astype(F32)
    cos = jnp.cos(ang)
    sin = jnp.sin(ang) * sign
    return jnp.tile(cos, (1, LANES // HEAD_DIM)), jnp.tile(sin, (1, LANES // HEAD_DIM))


def kernel(x, c, ctx, c_ctx, w_mod, b_mod, w_in_even, sink, w_out_even, w_in_odd, conv_w, w_out_odd, ln_g, ln_b,
           w_router, b_router, w_up, b_up, w_down, b_down):
    B, S, D = x.shape
    L = ctx.shape[1]
    T, TC = B * S, B * L
    cos, sin = _rope_tables(S)
    cond = jnp.concatenate([c, c_ctx[None, :], jnp.zeros((MOD_ROWS - B - 1, D), F32)], 0)
    mod = _modulation(cond, w_mod, b_mod)

    bu = b_up.reshape(DEPTH, N_EXPERTS, 1, 2 * D_EXPERT)
    bd = b_down.reshape(DEPTH, N_EXPERTS, 1, D)
    wr_hi = w_router.astype(BF16)
    wr_lo = (w_router - wr_hi.astype(F32)).astype(BF16)
    pad_r = ((0, 0), (0, 0), (0, LANES - N_EXPERTS))
    wr_hi, wr_lo = jnp.pad(wr_hi, pad_r), jnp.pad(wr_lo, pad_r)
    br = jnp.pad(b_router, ((0, 0), (0, LANES - N_EXPERTS)), constant_values=NEG_INF).reshape(DEPTH, 1, LANES)

    x2 = x.reshape(T, D)
    xc2 = ctx.reshape(TC, D)
    for l in range(DEPTH):
        even = l % 2 == 0
        j = l // 2
        ctx_after = any(m % 2 == 0 for m in range(l + 1, DEPTH))
        sh1, sc1, g1, sh2, sc2, g2 = [m.reshape(MOD_ROWS, 1, D) for m in jnp.split(mod[l], 6, axis=-1)]
        lng1, lnb1 = ln_g[l, 0].reshape(1, D), ln_b[l, 0].reshape(1, D)
        lng2, lnb2 = ln_g[l, 1].reshape(1, D), ln_b[l, 1].reshape(1, D)
        post = (g1, sc2, sh2, lng1, lnb1, wr_hi[l], wr_lo[l], br[l])
        if even:
            w_in = w_in_even[j].astype(BF16)
            w_out = w_out_even[j].astype(BF16)
            h = _inproj_even(x2, sc1, sh1, w_in, cos, sin, S, None, True)
            hc = _inproj_even(xc2, sc1, sh1, w_in, cos, sin, L, B, False)
            kcol = ATTN_WIDTH // KV_WIDTH
            attn = _win_attention(h, hc, sink[j], B, S, kcol, kcol + 1)
            four = _fourier_mix(h, B, S)
            x2, v, logits = _outproj("even", (attn, four), w_out, x2, *post, S, None)
            if ctx_after:
                attn_c = _ctx_attention(hc, sink[j], B)
                four_c = _fourier_mix(hc, B, L)
                xc2, vc, logits_c = _outproj("even", (attn_c, four_c), w_out, xc2, *post, L, B)
        else:
            w_in = w_in_odd[j].astype(BF16)
            w_out = w_out_odd[j].astype(BF16)
            cw = jnp.pad(conv_w[j], ((0, 8 - conv_w.shape[1]), (0, 0)))
            bz = _inproj_odd(x2, sc1, sh1, w_in, S, None)
            x2, v, logits = _outproj("odd", (bz,), w_out, x2, *post, S, None, conv_w=cw)
            if ctx_after:
                bzc = _inproj_odd(xc2, sc1, sh1, w_in, L, B)
                xc2, vc, logits_c = _outproj("odd", (bzc,), w_out, xc2, *post, L, B, conv_w=cw)
        if ctx_after:
            ybuf, pos, gates = _moe((v, vc), jnp.concatenate([logits, logits_c], 0), l, w_up, bu[l], w_down, bd[l])
            xc2 = _combine(xc2, gates[T:], pos[T * TOP_K:], ybuf, g2, lng2, lnb2, L, B)
        else:
            ybuf, pos, gates = _moe((v,), logits, l, w_up, bu[l], w_down, bd[l])
        x2 = _combine(x2, gates[:T], pos[:T * TOP_K], ybuf, g2, lng2, lnb2, S, None)
    return x2.reshape(B, S, D)
```

```python
import functools

import numpy as np
import jax
import jax.numpy as jnp
from jax import lax
from jax.experimental import pallas as pl
from jax.experimental.pallas import tpu as pltpu

D_MODEL = 1024
DEPTH = 4
GRID_W = 64
HEAD_DIM = 64
N_Q_HEADS = 12
N_KV_HEADS = 4
Q_PER_KV = N_Q_HEADS // N_KV_HEADS
BLOCK = 128
ROPE_BASE = 10000.0
FOURIER_GROUP_DIM = 64
FOURIER_WIDTH = 256
ATTN_WIDTH = N_Q_HEADS * HEAD_DIM
KV_WIDTH = N_KV_HEADS * HEAD_DIM
EVEN_IN_WIDTH = ATTN_WIDTH + 2 * KV_WIDTH + FOURIER_WIDTH
N_EXPERTS = 32
TOP_K = 4
D_EXPERT = D_MODEL
SWIGLU_LIMIT = 7.0
SWIGLU_ALPHA = 1.702
LN_EPS = 1e-5
NEG_INF = -1e30
DEEPNORM_ALPHA = (2 * DEPTH) ** 0.25

LANES = 128
SUBLANES = 8
TOKEN_SUBLANES = D_MODEL // LANES
ROW_TILE = 512
EXPERT_ROWS = 256
MOD_ROWS = 16
ROUTE_TILE = 512
DISPATCH_TILE = 512
COMBINE_TILE = 256
DMA_UNROLL = 8
VMEM_LIMIT = 48 * 1024 * 1024
EXPERT_VMEM_LIMIT = 56 * 1024 * 1024

F32 = jnp.float32
BF16 = jnp.bfloat16


def _params(*sem):
    return pltpu.CompilerParams(dimension_semantics=sem, vmem_limit_bytes=VMEM_LIMIT)


def _mod_spec(tm, seq_rows, mod_row):
    if mod_row is None:
        return pl.BlockSpec((1, 1, D_MODEL), lambda i: ((i * tm) // seq_rows, 0, 0))
    return pl.BlockSpec((1, 1, D_MODEL), lambda i: (mod_row, 0, 0))


def _mod_kernel(c_ref, w_ref, b_ref, o_ref):
    c = c_ref[...]
    s = (c * jax.nn.sigmoid(c)).astype(BF16)
    o_ref[0] = jnp.dot(s, w_ref[0].astype(BF16), preferred_element_type=F32) + b_ref[0]


def _modulation(cond, w_mod, b_mod):
    tn = 1536
    n = w_mod.shape[-1]
    return pl.pallas_call(
        _mod_kernel,
        grid=(DEPTH, n // tn),
        in_specs=[
            pl.BlockSpec((MOD_ROWS, D_MODEL), lambda l, j: (0, 0)),
            pl.BlockSpec((1, D_MODEL, tn), lambda l, j: (l, 0, j)),
            pl.BlockSpec((1, 1, tn), lambda l, j: (l, 0, j)),
        ],
        out_specs=pl.BlockSpec((1, MOD_ROWS, tn), lambda l, j: (l, 0, j)),
        out_shape=jax.ShapeDtypeStruct((DEPTH, MOD_ROWS, n), F32),
        compiler_params=_params("arbitrary", "arbitrary"),
        name="modulation",
    )(cond, w_mod, b_mod.reshape(DEPTH, 1, n))


def _rope_chunk(h, cos, sin_signed, first_half):
    swapped = jnp.where(first_half, pltpu.roll(h, LANES - HEAD_DIM // 2, axis=1), pltpu.roll(h, HEAD_DIM // 2, axis=1))
    return h * cos + swapped * sin_signed


def _inproj_even_kernel(x_ref, sc_ref, sh_ref, w_ref, cos_ref, sin_ref, o_ref, *, rope):
    u = (x_ref[...] * (1.0 + sc_ref[0]) + sh_ref[0]).astype(BF16)
    tn = 512
    if rope:
        cos = cos_ref[...]
        sin = sin_ref[...]
        lane = lax.broadcasted_iota(jnp.int32, cos.shape, 1)
        first_half = (lane % HEAD_DIM) < HEAD_DIM // 2
    scale = HEAD_DIM ** -0.5
    for jt in range(EVEN_IN_WIDTH // tn):
        acc = jnp.dot(u, w_ref[:, jt * tn:(jt + 1) * tn], preferred_element_type=F32)
        for k in range(tn // LANES):
            col = jt * tn + k * LANES
            h = acc[:, k * LANES:(k + 1) * LANES]
            if rope and col < ATTN_WIDTH + KV_WIDTH:
                h = _rope_chunk(h, cos, sin, first_half)
            if col < ATTN_WIDTH:
                h = h * scale
            o_ref[:, col:col + LANES] = h.astype(BF16)


def _inproj_even(x2, sc, sh, w, cos, sin, seq_rows, mod_row, rope):
    t = x2.shape[0]
    tm = min(ROW_TILE, seq_rows)
    pos_tiles = cos.shape[0] // tm
    mod = _mod_spec(tm, seq_rows, mod_row)

    return pl.pallas_call(
        functools.partial(_inproj_even_kernel, rope=rope),
        grid=(t // tm,),
        in_specs=[
            pl.BlockSpec((tm, D_MODEL), lambda i: (i, 0)),
            mod,
            mod,
            pl.BlockSpec((D_MODEL, EVEN_IN_WIDTH), lambda i: (0, 0)),
            pl.BlockSpec((tm, LANES), lambda i: (i % pos_tiles, 0)),
            pl.BlockSpec((tm, LANES), lambda i: (i % pos_tiles, 0)),
        ],
        out_specs=pl.BlockSpec((tm, EVEN_IN_WIDTH), lambda i: (i, 0)),
        out_shape=jax.ShapeDtypeStruct((t, EVEN_IN_WIDTH), BF16),
        compiler_params=_params("arbitrary"),
        name="inproj_even",
    )(x2, sc, sh, w, cos, sin)


def _attend(q_ref, k_all, v_all, valid, sink_ref, o_ref):
    rows = q_ref.shape[0]
    lane = lax.broadcasted_iota(jnp.int32, (rows, LANES), 1)
    low = lane < HEAD_DIM
    for kv in range(N_KV_HEADS):
        kv_chunk, kv_half = divmod(kv, 2)
        keep = low if kv_half == 0 else jnp.logical_not(low)
        stack, sinks = [], []
        for g in range(Q_PER_KV):
            head = kv * Q_PER_KV + g
            chunk, half = divmod(head, 2)
            qh = q_ref[:, chunk * LANES:(chunk + 1) * LANES].astype(F32)
            if half != kv_half:
                qh = pltpu.roll(qh, HEAD_DIM, axis=1)
            stack.append(jnp.where(keep, qh, 0.0).astype(BF16))
            sinks.append(jnp.full((rows, 1), sink_ref[head], F32))
        qs = jnp.concatenate(stack, axis=0)
        sk = jnp.concatenate(sinks, axis=0)
        kc = k_all[:, kv_chunk * LANES:(kv_chunk + 1) * LANES]
        vc = v_all[:, kv_chunk * LANES:(kv_chunk + 1) * LANES]
        s = lax.dot_general(qs, kc, (((1,), (1,)), ((), ())), preferred_element_type=F32)
        if valid is not None:
            s = jnp.where(valid, s, NEG_INF)
        m = jnp.maximum(sk, jnp.max(s, axis=-1, keepdims=True))
        e = jnp.exp(s - m)
        denom = jnp.exp(sk - m) + jnp.sum(e, axis=-1, keepdims=True)
        o = jnp.dot(e.astype(BF16), vc, preferred_element_type=F32) / denom
        for g in range(Q_PER_KV):
            head = kv * Q_PER_KV + g
            half = head % 2
            oh = o[g * rows:(g + 1) * rows]
            if half != kv_half:
                oh = pltpu.roll(oh, HEAD_DIM, axis=1)
            o_ref[:, head * HEAD_DIM:(head + 1) * HEAD_DIM] = oh[:, half * HEAD_DIM:(half + 1) * HEAD_DIM].astype(BF16)


def _win_attn_kernel(sink_ref, q_ref, kp_ref, kc_ref, kn_ref, vp_ref, vc_ref, vn_ref, kx_ref, vx_ref, o_ref, *, nb):
    n = pl.program_id(1)
    k_all = jnp.concatenate([kp_ref[...], kc_ref[...], kn_ref[...], kx_ref[...]], axis=0)
    v_all = jnp.concatenate([vp_ref[...], vc_ref[...], vn_ref[...], vx_ref[...]], axis=0)
    n_keys = k_all.shape[0]
    r = lax.broadcasted_iota(jnp.int32, (Q_PER_KV * BLOCK, n_keys), 0) % BLOCK
    c = lax.broadcasted_iota(jnp.int32, (Q_PER_KV * BLOCK, n_keys), 1)
    lo = jnp.where(n > 0, r, BLOCK)
    hi = jnp.where(n < nb - 1, r, -1)
    valid = (c >= lo) & ((c < 2 * BLOCK) | (c >= 3 * BLOCK) | (c - 2 * BLOCK <= hi))
    _attend(q_ref, k_all, v_all, valid, sink_ref, o_ref)


def _win_attention(h, hc, sink, batch, seq, kc_col, vc_col):
    nb = seq // BLOCK
    qcol, kcol, vcol = 0, ATTN_WIDTH // KV_WIDTH, ATTN_WIDTH // KV_WIDTH + 1

    def blk(shift, col):
        return pl.BlockSpec((BLOCK, KV_WIDTH), lambda b, n: (b * nb + jnp.clip(n + shift, 0, nb - 1), col))

    return pl.pallas_call(
        functools.partial(_win_attn_kernel, nb=nb),
        grid=(batch, nb),
        in_specs=[
            pl.BlockSpec(memory_space=pltpu.SMEM),
            pl.BlockSpec((BLOCK, ATTN_WIDTH), lambda b, n: (b * nb + n, qcol)),
            blk(-1, kcol), blk(0, kcol), blk(1, kcol),
            blk(-1, vcol), blk(0, vcol), blk(1, vcol),
            pl.BlockSpec((hc.shape[0] // batch, KV_WIDTH), lambda b, n: (b, kc_col)),
            pl.BlockSpec((hc.shape[0] // batch, KV_WIDTH), lambda b, n: (b, vc_col)),
        ],
        out_specs=pl.BlockSpec((BLOCK, ATTN_WIDTH), lambda b, n: (b * nb + n, 0)),
        out_shape=jax.ShapeDtypeStruct((batch * seq, ATTN_WIDTH), BF16),
        compiler_params=_params("arbitrary", "arbitrary"),
        name="window_attention",
    )(sink, h, h, h, h, h, h, h, hc, hc)


def _ctx_attn_kernel(sink_ref, q_ref, k_ref, v_ref, o_ref):
    _attend(q_ref, k_ref[...], v_ref[...], None, sink_ref, o_ref)


def _ctx_attention(hc, sink, batch):
    ctx_len = hc.shape[0] // batch
    kcol, vcol = ATTN_WIDTH // KV_WIDTH, ATTN_WIDTH // KV_WIDTH + 1
    return pl.pallas_call(
        _ctx_attn_kernel,
        grid=(batch,),
        in_specs=[
            pl.BlockSpec(memory_space=pltpu.SMEM),
            pl.BlockSpec((ctx_len, ATTN_WIDTH), lambda b: (b, 0)),
            pl.BlockSpec((ctx_len, KV_WIDTH), lambda b: (b, kcol)),
            pl.BlockSpec((ctx_len, KV_WIDTH), lambda b: (b, vcol)),
        ],
        out_specs=pl.BlockSpec((ctx_len, ATTN_WIDTH), lambda b: (b, 0)),
        out_shape=jax.ShapeDtypeStruct((hc.shape[0], ATTN_WIDTH), BF16),
        compiler_params=_params("arbitrary"),
        name="context_attention",
    )(sink, hc, hc, hc)


def _dft_mats(n):
    k = np.arange(n, dtype=np.int64)
    ang = 2.0 * np.pi * ((k[:, None] * k[None, :]) % n).astype(np.float64) / n
    return np.cos(ang), np.sin(ang)


def _group_dft():
    c, s = _dft_mats(FOURIER_GROUP_DIM)
    groups = FOURIER_WIDTH // FOURIER_GROUP_DIM
    eye = np.eye(groups)
    return np.concatenate([np.kron(eye, c), np.kron(eye, s)], axis=1)


def _fourier_kernel(f_ref, w1_ref, c_ref, s_ref, o_ref, ab_ref, *, scale):
    @pl.when(pl.program_id(1) == 0)
    def _():
        ab_ref[...] = jnp.dot(f_ref[...], w1_ref[...], preferred_element_type=F32).astype(BF16)

    y = (jnp.dot(c_ref[...], ab_ref[:, :FOURIER_WIDTH], preferred_element_type=F32)
         - jnp.dot(s_ref[...], ab_ref[:, FOURIER_WIDTH:], preferred_element_type=F32))
    o_ref[...] = (y * scale).astype(BF16)


def _fourier_mix(h, batch, n):
    tr = 256
    cn, sn = _dft_mats(n)
    fcol = (EVEN_IN_WIDTH - FOURIER_WIDTH) // FOURIER_WIDTH
    return pl.pallas_call(
        functools.partial(_fourier_kernel, scale=float((n * FOURIER_GROUP_DIM) ** -0.5)),
        grid=(batch, n // tr),
        in_specs=[
            pl.BlockSpec((n, FOURIER_WIDTH), lambda b, i: (b, fcol)),
            pl.BlockSpec((FOURIER_WIDTH, 2 * FOURIER_WIDTH), lambda b, i: (0, 0)),
            pl.BlockSpec((tr, n), lambda b, i: (i, 0)),
            pl.BlockSpec((tr, n), lambda b, i: (i, 0)),
        ],
        out_specs=pl.BlockSpec((tr, FOURIER_WIDTH), lambda b, i: (b * (n // tr) + i, 0)),
        out_shape=jax.ShapeDtypeStruct((batch * n, FOURIER_WIDTH), BF16),
        scratch_shapes=[pltpu.VMEM((n, 2 * FOURIER_WIDTH), BF16)],
        compiler_params=_params("arbitrary", "arbitrary"),
        name="fourier_mix",
    )(h, jnp.asarray(_group_dft(), BF16), jnp.asarray(cn, BF16), jnp.asarray(sn, BF16))


def _inproj_odd_kernel(x_ref, sc_ref, sh_ref, w_ref, o_ref):
    u = (x_ref[...] * (1.0 + sc_ref[0]) + sh_ref[0]).astype(BF16)
    tn = 512
    for jt in range(D_MODEL // tn):
        cols = slice(jt * tn, (jt + 1) * tn)
        b_gate = jnp.dot(u, w_ref[:, jt * tn:(jt + 1) * tn], preferred_element_type=F32)
        o_ref[:, cols] = b_gate.astype(BF16)
        c_gate = jnp.dot(u, w_ref[:, D_MODEL + jt * tn:D_MODEL + (jt + 1) * tn], preferred_element_type=F32)
        hh = jnp.dot(u, w_ref[:, 2 * D_MODEL + jt * tn:2 * D_MODEL + (jt + 1) * tn], preferred_element_type=F32)
        o_ref[:, D_MODEL + jt * tn:D_MODEL + (jt + 1) * tn] = (c_gate * hh).astype(BF16)


def _inproj_odd(x2, sc, sh, w, seq_rows, mod_row):
    t = x2.shape[0]
    tm = min(ROW_TILE, seq_rows)
    mod = _mod_spec(tm, seq_rows, mod_row)

    return pl.pallas_call(
        _inproj_odd_kernel,
        grid=(t // tm,),
        in_specs=[
            pl.BlockSpec((tm, D_MODEL), lambda i: (i, 0)),
            mod,
            mod,
            pl.BlockSpec((D_MODEL, 3 * D_MODEL), lambda i: (0, 0)),
        ],
        out_specs=pl.BlockSpec((tm, 2 * D_MODEL), lambda i: (i, 0)),
        out_shape=jax.ShapeDtypeStruct((t, 2 * D_MODEL), BF16),
        compiler_params=_params("arbitrary"),
        name="inproj_odd",
    )(x2, sc, sh, w)


def _layer_norm(r, g, b):
    mu = jnp.mean(r, axis=-1, keepdims=True)
    d = r - mu
    var = jnp.mean(d * d, axis=-1, keepdims=True)
    return d * lax.rsqrt(var + LN_EPS) * g + b


def _store_token_tiles(ref, rows):
    m = rows.shape[0]
    for c in range(TOKEN_SUBLANES):
        ref[pl.ds(c, m, stride=TOKEN_SUBLANES), :] = rows[:, c * LANES:(c + 1) * LANES]


def _load_token_tiles(ref, m, first_row=0):
    return jnp.concatenate([ref[pl.ds(first_row + c, m, stride=TOKEN_SUBLANES), :] for c in range(TOKEN_SUBLANES)], axis=1)


def _token_tile(ref, t):
    row = t * TOKEN_SUBLANES
    return ref.at[pl.ds(row if isinstance(row, int) else pl.multiple_of(row, TOKEN_SUBLANES), TOKEN_SUBLANES), :]


def _post_mixer(y, x_ref, g1_ref, sc2_ref, sh2_ref, lng_ref, lnb_ref, wrh_ref, wrl_ref, br_ref, xo_ref, v_ref, lg_ref):
    xn = _layer_norm(DEEPNORM_ALPHA * x_ref[...] + g1_ref[0] * y, lng_ref[...], lnb_ref[...])
    xo_ref[...] = xn
    v = xn * (1.0 + sc2_ref[0]) + sh2_ref[0]
    _store_token_tiles(v_ref, v)
    v_hi = v.astype(BF16)
    v_lo = (v - v_hi.astype(F32)).astype(BF16)
    lg_ref[...] = (jnp.dot(v_hi, wrh_ref[...], preferred_element_type=F32)
                   + (jnp.dot(v_lo, wrh_ref[...], preferred_element_type=F32)
                      + jnp.dot(v_hi, wrl_ref[...], preferred_element_type=F32))) + br_ref[...]


def _outproj_even_kernel(a_ref, f_ref, w_ref, *rest):
    y = (jnp.dot(a_ref[...], w_ref[:ATTN_WIDTH, :], preferred_element_type=F32)
         + jnp.dot(f_ref[...], w_ref[ATTN_WIDTH:, :], preferred_element_type=F32))
    _post_mixer(y, *rest)


def _outproj_odd_kernel(b_ref, z_ref, zp_ref, zn_ref, cw_ref, w_ref, *rest, rows_per_seq):
    tm = z_ref.shape[0]
    halo = zp_ref.shape[0]
    row0 = pl.program_id(0) * tm
    z = z_ref[...].astype(F32)
    prev_row = jnp.where(row0 % rows_per_seq == 0, 0.0, zp_ref[halo - 1:halo, :].astype(F32))
    next_row = jnp.where((row0 + tm) % rows_per_seq == 0, 0.0, zn_ref[0:1, :].astype(F32))
    r = lax.broadcasted_iota(jnp.int32, z.shape, 0)
    z_prev = jnp.where(r == 0, prev_row, pltpu.roll(z, 1, axis=0))
    z_next = jnp.where(r == tm - 1, next_row, pltpu.roll(z, tm - 1, axis=0))
    conv = cw_ref[0:1, :] * z_prev + cw_ref[1:2, :] * z + cw_ref[2:3, :] * z_next
    a = (b_ref[...].astype(F32) * conv).astype(BF16)
    y = jnp.dot(a, w_ref[...], preferred_element_type=F32)
    _post_mixer(y, *rest)


def _outproj(kind, mix_inputs, w_out, x2, g1, sc2, sh2, lng, lnb, wr_hi, wr_lo, br, seq_rows, mod_row, conv_w=None):
    t = x2.shape[0]
    tm = min(ROW_TILE, seq_rows)
    mod = _mod_spec(tm, seq_rows, mod_row)

    row = lambda width: pl.BlockSpec((tm, width), lambda i: (i, 0))
    const = lambda shape: pl.BlockSpec(shape, lambda i: (0,) * len(shape))
    if kind == "even":
        attn, four = mix_inputs
        kern = _outproj_even_kernel
        head_specs = [row(ATTN_WIDTH), row(FOURIER_WIDTH)]
        head_args = [attn, four]
    else:
        (bz,) = mix_inputs
        halo = 16
        last = t // halo - 1
        kern = functools.partial(_outproj_odd_kernel, rows_per_seq=seq_rows)
        head_specs = [
            pl.BlockSpec((tm, D_MODEL), lambda i: (i, 0)),
            pl.BlockSpec((tm, D_MODEL), lambda i: (i, 1)),
            pl.BlockSpec((halo, D_MODEL), lambda i: (jnp.maximum(i * (tm // halo) - 1, 0), 1)),
            pl.BlockSpec((halo, D_MODEL), lambda i: (jnp.minimum((i + 1) * (tm // halo), last), 1)),
            const((8, D_MODEL)),
        ]
        head_args = [bz, bz, bz, bz, conv_w]
    return pl.pallas_call(
        kern,
        grid=(t // tm,),
        in_specs=head_specs + [
            const((D_MODEL, D_MODEL)),
            row(D_MODEL),
            mod,
            mod,
            mod,
            const((1, D_MODEL)), const((1, D_MODEL)),
            const((D_MODEL, LANES)), const((D_MODEL, LANES)), const((1, LANES)),
        ],
        out_specs=[row(D_MODEL), pl.BlockSpec((tm * TOKEN_SUBLANES, LANES), lambda i: (i, 0)), row(LANES)],
        out_shape=[
            jax.ShapeDtypeStruct((t, D_MODEL), F32),
            jax.ShapeDtypeStruct((t * TOKEN_SUBLANES, LANES), F32),
            jax.ShapeDtypeStruct((t, LANES), F32),
        ],
        compiler_params=_params("arbitrary"),
        name="outproj_" + kind,
    )(*head_args, w_out, x2, g1, sc2, sh2, lng, lnb, wr_hi, wr_lo, br)


def _route_kernel(lg_ref, tri_ref, idx_ref, gate_ref, cnt_ref, carry_ref):
    @pl.when(pl.program_id(0) == 0)
    def _():
        carry_ref[...] = jnp.zeros_like(carry_ref)

    work = lg_ref[...]
    lane = lax.broadcasted_iota(jnp.int32, work.shape, 1).astype(F32)
    sels, vals, ids = [], [], []
    for _ in range(TOP_K):
        m = jnp.max(work, axis=1, keepdims=True)
        first = jnp.min(jnp.where(work == m, lane, float(LANES)), axis=1, keepdims=True)
        sel = lane == first
        work = jnp.where(sel, -jnp.inf, work)
        sels.append(sel)
        vals.append(m)
        ids.append(first)
    onehot = jnp.where(sels[0] | sels[1] | sels[2] | sels[3], 1.0, 0.0)
    before = jnp.dot(tri_ref[...], onehot.astype(BF16), preferred_element_type=F32) + carry_ref[...]
    carry_ref[...] = carry_ref[...] + jnp.sum(onehot, axis=0, keepdims=True)
    cnt_ref[...] = carry_ref[...]
    exps = [jnp.exp(v - vals[0]) for v in vals]
    denom = exps[0] + exps[1] + exps[2] + exps[3]
    idx_out = jnp.zeros_like(work)
    gate_out = jnp.zeros_like(work)
    for k in range(TOP_K):
        rank = jnp.sum(jnp.where(sels[k], before, 0.0), axis=1, keepdims=True)
        idx_out = jnp.where(lane == k, ids[k], jnp.where(lane == TOP_K + k, rank, idx_out))
        gate_out = jnp.where(lane == k, exps[k] / denom, gate_out)
    idx_ref[...] = idx_out.astype(jnp.int32)
    gate_ref[...] = gate_out


def _route(logits):
    t = logits.shape[0]
    tm = ROUTE_TILE
    tri = jnp.asarray(np.tril(np.ones((tm, tm), np.float32), -1), BF16)
    row = pl.BlockSpec((tm, LANES), lambda i: (i, 0))
    return pl.pallas_call(
        _route_kernel,
        grid=(t // tm,),
        in_specs=[row, pl.BlockSpec((tm, tm), lambda i: (0, 0))],
        out_specs=[row, row, pl.BlockSpec((1, LANES), lambda i: (0, 0))],
        out_shape=[
            jax.ShapeDtypeStruct((t, LANES), jnp.int32),
            jax.ShapeDtypeStruct((t, LANES), F32),
            jax.ShapeDtypeStruct((1, LANES), F32),
        ],
        scratch_shapes=[pltpu.VMEM((1, LANES), F32)],
        compiler_params=_params("arbitrary"),
        name="route",
    )(logits, tri)


def _zero_fill(fill_start_ref, fill_len_ref, n_used_ref, buf_hbm, zero_vmem, sem_fill):
    n_blk = buf_hbm.shape[0] // (EXPERT_ROWS * TOKEN_SUBLANES)
    zero_vmem[...] = jnp.zeros_like(zero_vmem)
    chunks = [1 << b for b in reversed(range(EXPERT_ROWS.bit_length() - 1))]

    def pads(wait):
        def body(e, carry):
            start, length = fill_start_ref[e], fill_len_ref[e]
            for c in chunks:
                @pl.when((length & c) != 0)
                def _():
                    row = pl.multiple_of((start + (length & (-2 * c))) * TOKEN_SUBLANES, TOKEN_SUBLANES)
                    cp = pltpu.make_async_copy(zero_vmem.at[pl.ds(0, c * TOKEN_SUBLANES), :],
                                               buf_hbm.at[pl.ds(row, c * TOKEN_SUBLANES), :], sem_fill)
                    cp.wait() if wait else cp.start()
            return carry

        lax.fori_loop(0, N_EXPERTS, body, 0)

    def tail(wait):
        def body(j, carry):
            row = pl.multiple_of(j * (EXPERT_ROWS * TOKEN_SUBLANES), EXPERT_ROWS * TOKEN_SUBLANES)
            cp = pltpu.make_async_copy(zero_vmem, buf_hbm.at[pl.ds(row, EXPERT_ROWS * TOKEN_SUBLANES), :], sem_fill)
            cp.wait() if wait else cp.start()
            return carry

        lax.fori_loop(n_used_ref[0], n_blk, body, 0)

    pads(False)
    tail(False)
    pads(True)
    tail(True)


def _dispatch_kernel(fill_start_ref, fill_len_ref, n_used_ref, *rest, tiles):
    v_refs = rest[:len(tiles)]
    pos_hbm, buf_hbm, idx_smem, sem_idx, sem_rows, zero_vmem, sem_fill = rest[len(tiles):]
    tm = v_refs[0].shape[0] // TOKEN_SUBLANES
    per_tile = tm * TOP_K
    i = pl.program_id(0)
    fetch = pltpu.make_async_copy(pos_hbm.at[pl.ds(i * per_tile, per_tile)], idx_smem, sem_idx)
    fetch.start()

    @pl.when(i == 0)
    def _():
        _zero_fill(fill_start_ref, fill_len_ref, n_used_ref, buf_hbm, zero_vmem, sem_fill)

    fetch.wait()

    def scatter(v_ref):
        def row_copy(t, p):
            return pltpu.make_async_copy(_token_tile(v_ref, t), _token_tile(buf_hbm, p), sem_rows)

        def issue(j, carry):
            for u in range(DMA_UNROLL):
                t = j * DMA_UNROLL + u
                for k in range(TOP_K):
                    row_copy(t, idx_smem[t * TOP_K + k]).start(priority=k % 2)
            return carry

        def drain(j, carry):
            for _ in range(DMA_UNROLL * TOP_K):
                row_copy(0, 0).wait()
            return carry

        lax.fori_loop(0, tm // DMA_UNROLL, issue, 0)
        lax.fori_loop(0, tm // DMA_UNROLL, drain, 0)

    first = 0
    for v_ref, n_tiles in zip(v_refs, tiles):
        @pl.when((i >= first) & (i < first + n_tiles))
        def _():
            scatter(v_ref)

        first += n_tiles


def _dispatch(fill_start, fill_len, n_used, v_parts, pos_flat, n_rows):
    tm = DISPATCH_TILE
    tiles = tuple(v.shape[0] // (tm * TOKEN_SUBLANES) for v in v_parts)
    firsts = [sum(tiles[:p]) for p in range(len(tiles))]
    any_spec = pl.BlockSpec(memory_space=pl.ANY)

    def part_spec(first, n_tiles):
        return pl.BlockSpec((tm * TOKEN_SUBLANES, LANES), lambda i, *_: (jnp.clip(i - first, 0, n_tiles - 1), 0))

    grid_spec = pltpu.PrefetchScalarGridSpec(
        num_scalar_prefetch=3,
        grid=(sum(tiles),),
        in_specs=[part_spec(f, n) for f, n in zip(firsts, tiles)] + [any_spec],
        out_specs=any_spec,
        scratch_shapes=[pltpu.SMEM((tm * TOP_K,), jnp.int32), pltpu.SemaphoreType.DMA, pltpu.SemaphoreType.DMA,
                        pltpu.VMEM((EXPERT_ROWS * TOKEN_SUBLANES, LANES), F32), pltpu.SemaphoreType.DMA],
    )
    return pl.pallas_call(
        functools.partial(_dispatch_kernel, tiles=tiles),
        grid_spec=grid_spec,
        out_shape=jax.ShapeDtypeStruct((n_rows * TOKEN_SUBLANES, LANES), F32),
        compiler_params=_params("arbitrary"),
        name="dispatch",
    )(fill_start, fill_len, n_used, *v_parts, pos_flat)


def _pair_shuffle():
    l = np.arange(LANES)
    p = np.zeros((LANES, LANES), np.float32)
    p[l, l // 2 + (LANES // 2) * (l % 2)] = 1.0
    return p


def _expert_kernel(blk_e_ref, n_used_ref, run_ref, next_ref, x_ref, bu_ref, bd_ref, perm_ref, wu_hbm, wd_hbm, o_ref,
                   wu_f32, wd_f32, wu_bf, wd_bf, sem_u, sem_d, *, layer):
    i = pl.program_id(0)
    active = i < n_used_ref[0]
    expert = blk_e_ref[i]
    new_expert = (i == 0) | (expert != blk_e_ref[jnp.maximum(i - 1, 0)])

    def fetch(e, slot):
        return (pltpu.make_async_copy(wu_hbm.at[layer, e], wu_f32.at[slot], sem_u.at[slot]),
                pltpu.make_async_copy(wd_hbm.at[layer, e], wd_f32.at[slot], sem_d.at[slot]))

    @pl.when(active & new_expert)
    def _():
        slot = run_ref[expert] % 2

        @pl.when(i == 0)
        def _():
            for cp in fetch(expert, slot):
                cp.start()

        for cp in fetch(expert, slot):
            cp.wait()
        nxt = next_ref[expert]

        @pl.when(nxt >= 0)
        def _():
            for cp in fetch(nxt, 1 - slot):
                cp.start()

        def cast(c, carry):
            r = pl.multiple_of(c * LANES, LANES)
            wu_bf[pl.ds(r, LANES), :] = wu_f32[slot, pl.ds(r, LANES), :].astype(BF16)
            wd_bf[pl.ds(r, LANES), :] = jnp.dot(perm_ref[...], wd_f32[slot, pl.ds(r, LANES), :].astype(BF16),
                                               preferred_element_type=F32).astype(BF16)
            return carry

        lax.fori_loop(0, D_MODEL // LANES, cast, 0)

    @pl.when(active)
    def _():
        h = jnp.dot(_load_token_tiles(x_ref, EXPERT_ROWS).astype(BF16), wu_bf[...], preferred_element_type=F32) + bu_ref[0]
        even = lax.broadcasted_iota(jnp.int32, (EXPERT_ROWS, LANES), 1) % 2 == 0
        acts = []
        for k in range(D_EXPERT // LANES):
            ha = h[:, 2 * k * LANES:(2 * k + 1) * LANES]
            hb = h[:, (2 * k + 1) * LANES:(2 * k + 2) * LANES]
            glu = jnp.where(even, ha, pltpu.roll(hb, 1, axis=1))
            lin = jnp.where(even, pltpu.roll(ha, LANES - 1, axis=1), hb)
            glu = jnp.minimum(glu, SWIGLU_LIMIT)
            lin = jnp.clip(lin, -SWIGLU_LIMIT, SWIGLU_LIMIT)
            acts.append((glu * jax.nn.sigmoid(SWIGLU_ALPHA * glu) * (lin + 1.0)).astype(BF16))
        a = jnp.concatenate(acts, axis=1)
        _store_token_tiles(o_ref, jnp.dot(a, wd_bf[...], preferred_element_type=F32) + bd_ref[0])

    @pl.when(jnp.logical_not(active))
    def _():
        o_ref[...] = jnp.zeros_like(o_ref)


def _experts(blk_e, n_used, run_idx, next_expert, xbuf, layer, wu, bu, wd, bd):
    n_blk = xbuf.shape[0] // (EXPERT_ROWS * TOKEN_SUBLANES)
    grid_spec = pltpu.PrefetchScalarGridSpec(
        num_scalar_prefetch=4,
        grid=(n_blk,),
        in_specs=[
            pl.BlockSpec((EXPERT_ROWS * TOKEN_SUBLANES, LANES), lambda i, e, *_: (i, 0)),
            pl.BlockSpec((1, 1, 2 * D_EXPERT), lambda i, e, *_: (e[i], 0, 0)),
            pl.BlockSpec((1, 1, D_MODEL), lambda i, e, *_: (e[i], 0, 0)),
            pl.BlockSpec((LANES, LANES), lambda i, e, *_: (0, 0)),
            pl.BlockSpec(memory_space=pl.ANY),
            pl.BlockSpec(memory_space=pl.ANY),
        ],
        out_specs=pl.BlockSpec((EXPERT_ROWS * TOKEN_SUBLANES, LANES), lambda i, e, *_: (i, 0)),
        scratch_shapes=[
            pltpu.VMEM((2, D_MODEL, 2 * D_EXPERT), F32),
            pltpu.VMEM((2, D_EXPERT, D_MODEL), F32),
            pltpu.VMEM((D_MODEL, 2 * D_EXPERT), BF16),
            pltpu.VMEM((D_EXPERT, D_MODEL), BF16),
            pltpu.SemaphoreType.DMA((2,)),
            pltpu.SemaphoreType.DMA((2,)),
        ],
    )
    return pl.pallas_call(
        functools.partial(_expert_kernel, layer=layer),
        grid_spec=grid_spec,
        out_shape=jax.ShapeDtypeStruct(xbuf.shape, F32),
        compiler_params=pltpu.CompilerParams(dimension_semantics=("arbitrary",), vmem_limit_bytes=EXPERT_VMEM_LIMIT),
        name="experts",
    )(blk_e, n_used, run_idx, next_expert, xbuf, bu, bd, jnp.asarray(_pair_shuffle(), BF16), wu, wd)


def _combine_kernel(x_ref, gate_ref, g2_ref, lng_ref, lnb_ref, pos_hbm, y_hbm, o_ref, idx_smem, rows, sem_idx, sem_rows):
    tm = x_ref.shape[0]
    per_tile = tm * TOP_K
    i = pl.program_id(0)

    def row_copy(slot, t, k, p):
        return pltpu.make_async_copy(_token_tile(y_hbm, p), _token_tile(rows, (slot * TOP_K + k) * tm + t), sem_rows.at[slot])

    def start_gather(tile, slot):
        fetch = pltpu.make_async_copy(pos_hbm.at[pl.ds(tile * per_tile, per_tile)], idx_smem, sem_idx)
        fetch.start()
        fetch.wait()

        def issue(j, carry):
            for u in range(DMA_UNROLL):
                t = j * DMA_UNROLL + u
                for k in range(TOP_K):
                    row_copy(slot, t, k, idx_smem[t * TOP_K + k]).start(priority=k % 2)
            return carry

        lax.fori_loop(0, tm // DMA_UNROLL, issue, 0)

    @pl.when(i == 0)
    def _():
        start_gather(0, 0)

    @pl.when(i + 1 < pl.num_programs(0))
    def _():
        start_gather(i + 1, (i + 1) % 2)

    slot = i % 2

    def drain(j, carry):
        for _ in range(DMA_UNROLL * TOP_K):
            row_copy(slot, 0, 0, 0).wait()
        return carry

    lax.fori_loop(0, tm // DMA_UNROLL, drain, 0)
    chunks = []
    for c in range(TOKEN_SUBLANES):
        def part(k):
            first = pl.multiple_of((slot * TOP_K + k) * (tm * TOKEN_SUBLANES), tm * TOKEN_SUBLANES) + c
            return gate_ref[:, k:k + 1] * rows[pl.ds(first, tm, stride=TOKEN_SUBLANES), :]

        fc = part(0)
        for k in range(1, TOP_K):
            fc = fc + part(k)
        chunks.append(fc)
    f = jnp.concatenate(chunks, axis=1)
    o_ref[...] = _layer_norm(DEEPNORM_ALPHA * x_ref[...] + g2_ref[0] * f, lng_ref[...], lnb_ref[...])


def _combine(x2, gates, pos_flat, ybuf, g2, lng, lnb, seq_rows, mod_row):
    t = x2.shape[0]
    tm = COMBINE_TILE
    row = pl.BlockSpec((tm, D_MODEL), lambda i: (i, 0))
    const = pl.BlockSpec((1, D_MODEL), lambda i: (0, 0))
    return pl.pallas_call(
        _combine_kernel,
        grid=(t // tm,),
        in_specs=[row, pl.BlockSpec((tm, LANES), lambda i: (i, 0)), _mod_spec(tm, seq_rows, mod_row), const, const,
                  pl.BlockSpec(memory_space=pl.ANY), pl.BlockSpec(memory_space=pl.ANY)],
        out_specs=row,
        out_shape=jax.ShapeDtypeStruct((t, D_MODEL), F32),
        scratch_shapes=[
            pltpu.SMEM((tm * TOP_K,), jnp.int32),
            pltpu.VMEM((2 * TOP_K * tm * TOKEN_SUBLANES, LANES), F32),
            pltpu.SemaphoreType.DMA,
            pltpu.SemaphoreType.DMA((2,)),
        ],
        compiler_params=_params("arbitrary"),
        name="combine",
    )(x2, gates, g2, lng, lnb, pos_flat, ybuf)


def _moe(v_parts, logits, layer, wu, bu, wd, bd):
    t = logits.shape[0]
    n = t * TOP_K
    idx, gates, cnt = _route(logits)
    counts = cnt[0, :N_EXPERTS].astype(jnp.int32)
    padded = (counts + EXPERT_ROWS - 1) // EXPERT_ROWS * EXPERT_ROWS
    pad_ends = jnp.cumsum(padded)
    pad_starts = pad_ends - padded
    experts = jnp.arange(N_EXPERTS, dtype=jnp.int32)
    start = jnp.sum(jnp.where(idx[:, :TOP_K, None] == experts, pad_starts, 0), axis=-1)
    pos_flat = (start + idx[:, TOP_K:2 * TOP_K]).reshape(n)
    n_blk = n // EXPERT_ROWS + N_EXPERTS
    blk_start = jnp.arange(n_blk, dtype=jnp.int32) * EXPERT_ROWS
    blk_e = jnp.minimum(jnp.sum((pad_ends[None, :] <= blk_start[:, None]).astype(jnp.int32), axis=1), N_EXPERTS - 1)
    n_used = pad_ends[-1:] // EXPERT_ROWS
    nonempty = counts > 0
    run_idx = jnp.cumsum(nonempty.astype(jnp.int32)) - 1
    later = nonempty[None, :] & (experts[None, :] > experts[:, None])
    next_expert = jnp.min(jnp.where(later, experts[None, :], N_EXPERTS), axis=1)
    next_expert = jnp.where(next_expert == N_EXPERTS, -1, next_expert)
    fill_start, fill_len = pad_starts + counts, padded - counts
    xbuf = _dispatch(fill_start, fill_len, n_used, v_parts, pos_flat, n_blk * EXPERT_ROWS)
    ybuf = _experts(blk_e, n_used, run_idx, next_expert, xbuf, layer, wu, bu, wd, bd)
    return ybuf, pos_flat, gates


def _rope_tables(seq):
    rows = seq // GRID_W
    row = jnp.repeat(jnp.arange(rows), GRID_W).astype(F32)
    col = jnp.tile(jnp.arange(GRID_W), rows).astype(F32)
    n_freq = HEAD_DIM // 4
    inv = ROPE_BASE ** (-jnp.arange(n_freq, dtype=F32) / n_freq)
    ang = jnp.concatenate([row[:, None] * inv, col[:, None] * inv], -1)
    ang = jnp.concatenate([ang, ang], -1)
    sign = jnp.where(jnp.arange(HEAD_DIM) < HEAD_DIM // 2, -1.0, 1.0).astype(F32)
    cos = jnp.cos(ang)
    sin = jnp.sin(ang) * sign
    return jnp.tile(cos, (1, LANES // HEAD_DIM)), jnp.tile(sin, (1, LANES // HEAD_DIM))


def kernel(x, c, ctx, c_ctx, w_mod, b_mod, w_in_even, sink, w_out_even, w_in_odd, conv_w, w_out_odd, ln_g, ln_b,
           w_router, b_router, w_up, b_up, w_down, b_down):
    B, S, D = x.shape
    L = ctx.shape[1]
    T, TC = B * S, B * L
    cos, sin = _rope_tables(S)
    cond = jnp.concatenate([c, c_ctx[None, :], jnp.zeros((MOD_ROWS - B - 1, D), F32)], 0)
    mod = _modulation(cond, w_mod, b_mod)

    bu = b_up.reshape(DEPTH, N_EXPERTS, 1, 2 * D_EXPERT)
    bd = b_down.reshape(DEPTH, N_EXPERTS, 1, D)
    wr_hi = w_router.astype(BF16)
    wr_lo = (w_router - wr_hi.astype(F32)).astype(BF16)
    pad_r = ((0, 0), (0, 0), (0, LANES - N_EXPERTS))
    wr_hi, wr_lo = jnp.pad(wr_hi, pad_r), jnp.pad(wr_lo, pad_r)
    br = jnp.pad(b_router, ((0, 0), (0, LANES - N_EXPERTS)), constant_values=NEG_INF).reshape(DEPTH, 1, LANES)

    x2 = x.reshape(T, D)
    xc2 = ctx.reshape(TC, D)
    for l in range(DEPTH):
        even = l % 2 == 0
        j = l // 2
        ctx_after = any(m % 2 == 0 for m in range(l + 1, DEPTH))
        sh1, sc1, g1, sh2, sc2, g2 = [m.reshape(MOD_ROWS, 1, D) for m in jnp.split(mod[l], 6, axis=-1)]
        lng1, lnb1 = ln_g[l, 0].reshape(1, D), ln_b[l, 0].reshape(1, D)
        lng2, lnb2 = ln_g[l, 1].reshape(1, D), ln_b[l, 1].reshape(1, D)
        post = (g1, sc2, sh2, lng1, lnb1, wr_hi[l], wr_lo[l], br[l])
        if even:
            w_in = w_in_even[j].astype(BF16)
            w_out = w_out_even[j].astype(BF16)
            h = _inproj_even(x2, sc1, sh1, w_in, cos, sin, S, None, True)
            hc = _inproj_even(xc2, sc1, sh1, w_in, cos, sin, L, B, False)
            kcol = ATTN_WIDTH // KV_WIDTH
            attn = _win_attention(h, hc, sink[j], B, S, kcol, kcol + 1)
            four = _fourier_mix(h, B, S)
            x2, v, logits = _outproj("even", (attn, four), w_out, x2, *post, S, None)
            if ctx_after:
                attn_c = _ctx_attention(hc, sink[j], B)
                four_c = _fourier_mix(hc, B, L)
                xc2, vc, logits_c = _outproj("even", (attn_c, four_c), w_out, xc2, *post, L, B)
        else:
            w_in = w_in_odd[j].astype(BF16)
            w_out = w_out_odd[j].astype(BF16)
            cw = jnp.pad(conv_w[j], ((0, 8 - conv_w.shape[1]), (0, 0)))
            bz = _inproj_odd(x2, sc1, sh1, w_in, S, None)
            x2, v, logits = _outproj("odd", (bz,), w_out, x2, *post, S, None, conv_w=cw)
            if ctx_after:
                bzc = _inproj_odd(xc2, sc1, sh1, w_in, L, B)
                xc2, vc, logits_c = _outproj("odd", (bzc,), w_out, xc2, *post, L, B, conv_w=cw)
        if ctx_after:
            ybuf, pos, gates = _moe((v, vc), jnp.concatenate([logits, logits_c], 0), l, w_up, bu[l], w_down, bd[l])
            xc2 = _combine(xc2, gates[T:], pos[T * TOP_K:], ybuf, g2, lng2, lnb2, L, B)
        else:
            ybuf, pos, gates = _moe((v,), logits, l, w_up, bu[l], w_down, bd[l])
        x2 = _combine(x2, gates[:T], pos[:T * TOP_K], ybuf, g2, lng2, lnb2, S, None)
    return x2.reshape(B, S, D)
```

```python
import functools

import numpy as np
import jax
import jax.numpy as jnp
from jax import lax
from jax.experimental import pallas as pl
from jax.experimental.pallas import tpu as pltpu

D_MODEL = 1024
DEPTH = 4
GRID_W = 64
HEAD_DIM = 64
N_Q_HEADS = 12
N_KV_HEADS = 4
Q_PER_KV = N_Q_HEADS // N_KV_HEADS
BLOCK = 128
ROPE_BASE = 10000.0
FOURIER_GROUP_DIM = 64
FOURIER_WIDTH = 256
ATTN_WIDTH = N_Q_HEADS * HEAD_DIM
KV_WIDTH = N_KV_HEADS * HEAD_DIM
EVEN_IN_WIDTH = ATTN_WIDTH + 2 * KV_WIDTH + FOURIER_WIDTH
N_EXPERTS = 32
TOP_K = 4
D_EXPERT = D_MODEL
SWIGLU_LIMIT = 7.0
SWIGLU_ALPHA = 1.702
LN_EPS = 1e-5
NEG_INF = -1e30
DEEPNORM_ALPHA = (2 * DEPTH) ** 0.25

LANES = 128
SUBLANES = 8
TOKEN_SUBLANES = D_MODEL // LANES
ROW_TILE = 512
EXPERT_ROWS = 512
MOD_ROWS = 16
ROUTE_TILE = 512
DISPATCH_TILE = 1024
COMBINE_TILE = 512
DMA_UNROLL = 8
VMEM_LIMIT = 48 * 1024 * 1024
EXPERT_VMEM_LIMIT = 56 * 1024 * 1024

F32 = jnp.float32
BF16 = jnp.bfloat16


def _params(*sem):
    return pltpu.CompilerParams(dimension_semantics=sem, vmem_limit_bytes=VMEM_LIMIT)


def _mod_spec(tm, seq_rows, mod_row):
    if mod_row is None:
        return pl.BlockSpec((1, 1, D_MODEL), lambda i: ((i * tm) // seq_rows, 0, 0))
    return pl.BlockSpec((1, 1, D_MODEL), lambda i: (mod_row, 0, 0))


def _mod_kernel(c_ref, w_ref, b_ref, o_ref):
    c = c_ref[...]
    s = (c * jax.nn.sigmoid(c)).astype(BF16)
    o_ref[0] = jnp.dot(s, w_ref[0].astype(BF16), preferred_element_type=F32) + b_ref[0]


def _modulation(cond, w_mod, b_mod):
    tn = 1536
    n = w_mod.shape[-1]
    return pl.pallas_call(
        _mod_kernel,
        grid=(DEPTH, n // tn),
        in_specs=[
            pl.BlockSpec((MOD_ROWS, D_MODEL), lambda l, j: (0, 0)),
            pl.BlockSpec((1, D_MODEL, tn), lambda l, j: (l, 0, j)),
            pl.BlockSpec((1, 1, tn), lambda l, j: (l, 0, j)),
        ],
        out_specs=pl.BlockSpec((1, MOD_ROWS, tn), lambda l, j: (l, 0, j)),
        out_shape=jax.ShapeDtypeStruct((DEPTH, MOD_ROWS, n), F32),
        compiler_params=_params("arbitrary", "arbitrary"),
        name="modulation",
    )(cond, w_mod, b_mod.reshape(DEPTH, 1, n))


def _rope_chunk(h, cos, sin_signed, first_half):
    swapped = jnp.where(first_half, pltpu.roll(h, LANES - HEAD_DIM // 2, axis=1), pltpu.roll(h, HEAD_DIM // 2, axis=1))
    return h * cos + swapped * sin_signed


def _inproj_even_kernel(x_ref, sc_ref, sh_ref, w_ref, cos_ref, sin_ref, o_ref, *, rope):
    u = (x_ref[...] * (1.0 + sc_ref[0]) + sh_ref[0]).astype(BF16)
    tn = 512
    if rope:
        cos = cos_ref[...]
        sin = sin_ref[...]
        lane = lax.broadcasted_iota(jnp.int32, cos.shape, 1)
        first_half = (lane % HEAD_DIM) < HEAD_DIM // 2
    scale = HEAD_DIM ** -0.5
    for jt in range(EVEN_IN_WIDTH // tn):
        acc = jnp.dot(u, w_ref[:, jt * tn:(jt + 1) * tn], preferred_element_type=F32)
        for k in range(tn // LANES):
            col = jt * tn + k * LANES
            h = acc[:, k * LANES:(k + 1) * LANES]
            if rope and col < ATTN_WIDTH + KV_WIDTH:
                h = _rope_chunk(h, cos, sin, first_half)
            if col < ATTN_WIDTH:
                h = h * scale
            o_ref[:, col:col + LANES] = h.astype(BF16)


def _inproj_even(x2, sc, sh, w, cos, sin, seq_rows, mod_row, rope):
    t = x2.shape[0]
    tm = min(ROW_TILE, seq_rows)
    pos_tiles = cos.shape[0] // tm
    mod = _mod_spec(tm, seq_rows, mod_row)

    return pl.pallas_call(
        functools.partial(_inproj_even_kernel, rope=rope),
        grid=(t // tm,),
        in_specs=[
            pl.BlockSpec((tm, D_MODEL), lambda i: (i, 0)),
            mod,
            mod,
            pl.BlockSpec((D_MODEL, EVEN_IN_WIDTH), lambda i: (0, 0)),
            pl.BlockSpec((tm, LANES), lambda i: (i % pos_tiles, 0)),
            pl.BlockSpec((tm, LANES), lambda i: (i % pos_tiles, 0)),
        ],
        out_specs=pl.BlockSpec((tm, EVEN_IN_WIDTH), lambda i: (i, 0)),
        out_shape=jax.ShapeDtypeStruct((t, EVEN_IN_WIDTH), BF16),
        compiler_params=_params("arbitrary"),
        name="inproj_even",
    )(x2, sc, sh, w, cos, sin)


def _attend(q_ref, k_all, v_all, valid, sink_ref, o_ref):
    rows = q_ref.shape[0]
    lane = lax.broadcasted_iota(jnp.int32, (rows, LANES), 1)
    low = lane < HEAD_DIM
    for kv in range(N_KV_HEADS):
        kv_chunk, kv_half = divmod(kv, 2)
        keep = low if kv_half == 0 else jnp.logical_not(low)
        stack, sinks = [], []
        for g in range(Q_PER_KV):
            head = kv * Q_PER_KV + g
            chunk, half = divmod(head, 2)
            qh = q_ref[:, chunk * LANES:(chunk + 1) * LANES].astype(F32)
            if half != kv_half:
                qh = pltpu.roll(qh, HEAD_DIM, axis=1)
            stack.append(jnp.where(keep, qh, 0.0).astype(BF16))
            sinks.append(jnp.full((rows, 1), sink_ref[head], F32))
        qs = jnp.concatenate(stack, axis=0)
        sk = jnp.concatenate(sinks, axis=0)
        kc = k_all[:, kv_chunk * LANES:(kv_chunk + 1) * LANES]
        vc = v_all[:, kv_chunk * LANES:(kv_chunk + 1) * LANES]
        s = lax.dot_general(qs, kc, (((1,), (1,)), ((), ())), preferred_element_type=F32)
        if valid is not None:
            s = jnp.concatenate([s[:, j * BLOCK:(j + 1) * BLOCK] if ok is None
                                 else jnp.where(ok, s[:, j * BLOCK:(j + 1) * BLOCK], NEG_INF)
                                 for j, ok in enumerate(valid)], axis=1)
        m = jnp.maximum(sk, jnp.max(s, axis=-1, keepdims=True))
        e = jnp.exp(s - m)
        denom = jnp.exp(sk - m) + jnp.sum(e, axis=-1, keepdims=True)
        o = jnp.dot(e.astype(BF16), vc, preferred_element_type=F32) / denom
        for g in range(Q_PER_KV):
            head = kv * Q_PER_KV + g
            half = head % 2
            oh = o[g * rows:(g + 1) * rows]
            if half != kv_half:
                oh = pltpu.roll(oh, HEAD_DIM, axis=1)
            o_ref[:, head * HEAD_DIM:(head + 1) * HEAD_DIM] = oh[:, half * HEAD_DIM:(half + 1) * HEAD_DIM].astype(BF16)


def _win_attn_kernel(sink_ref, q_ref, kp_ref, kc_ref, kn_ref, vp_ref, vc_ref, vn_ref, kx_ref, vx_ref, o_ref, *, nb):
    n = pl.program_id(1)
    k_all = jnp.concatenate([kp_ref[...], kc_ref[...], kn_ref[...], kx_ref[...]], axis=0)
    v_all = jnp.concatenate([vp_ref[...], vc_ref[...], vn_ref[...], vx_ref[...]], axis=0)
    r = lax.broadcasted_iota(jnp.int32, (Q_PER_KV * BLOCK, BLOCK), 0) % BLOCK
    c = lax.broadcasted_iota(jnp.int32, (Q_PER_KV * BLOCK, BLOCK), 1)
    in_prev = c >= jnp.where(n > 0, r, BLOCK)
    in_next = c <= jnp.where(n < nb - 1, r, -1)
    valid = [in_prev, None, in_next] + [None] * (kx_ref.shape[0] // BLOCK)
    _attend(q_ref, k_all, v_all, valid, sink_ref, o_ref)


def _win_attention(h, hc, sink, batch, seq, kc_col, vc_col):
    nb = seq // BLOCK
    qcol, kcol, vcol = 0, ATTN_WIDTH // KV_WIDTH, ATTN_WIDTH // KV_WIDTH + 1

    def blk(shift, col):
        return pl.BlockSpec((BLOCK, KV_WIDTH), lambda b, n: (b * nb + jnp.clip(n + shift, 0, nb - 1), col))

    return pl.pallas_call(
        functools.partial(_win_attn_kernel, nb=nb),
        grid=(batch, nb),
        in_specs=[
            pl.BlockSpec(memory_space=pltpu.SMEM),
            pl.BlockSpec((BLOCK, ATTN_WIDTH), lambda b, n: (b * nb + n, qcol)),
            blk(-1, kcol), blk(0, kcol), blk(1, kcol),
            blk(-1, vcol), blk(0, vcol), blk(1, vcol),
            pl.BlockSpec((hc.shape[0] // batch, KV_WIDTH), lambda b, n: (b, kc_col)),
            pl.BlockSpec((hc.shape[0] // batch, KV_WIDTH), lambda b, n: (b, vc_col)),
        ],
        out_specs=pl.BlockSpec((BLOCK, ATTN_WIDTH), lambda b, n: (b * nb + n, 0)),
        out_shape=jax.ShapeDtypeStruct((batch * seq, ATTN_WIDTH), BF16),
        compiler_params=_params("arbitrary", "arbitrary"),
        name="window_attention",
    )(sink, h, h, h, h, h, h, h, hc, hc)


def _ctx_attn_kernel(sink_ref, q_ref, k_ref, v_ref, o_ref):
    _attend(q_ref, k_ref[...], v_ref[...], None, sink_ref, o_ref)


def _ctx_attention(hc, sink, batch):
    ctx_len = hc.shape[0] // batch
    kcol, vcol = ATTN_WIDTH // KV_WIDTH, ATTN_WIDTH // KV_WIDTH + 1
    return pl.pallas_call(
        _ctx_attn_kernel,
        grid=(batch,),
        in_specs=[
            pl.BlockSpec(memory_space=pltpu.SMEM),
            pl.BlockSpec((ctx_len, ATTN_WIDTH), lambda b: (b, 0)),
            pl.BlockSpec((ctx_len, KV_WIDTH), lambda b: (b, kcol)),
            pl.BlockSpec((ctx_len, KV_WIDTH), lambda b: (b, vcol)),
        ],
        out_specs=pl.BlockSpec((ctx_len, ATTN_WIDTH), lambda b: (b, 0)),
        out_shape=jax.ShapeDtypeStruct((hc.shape[0], ATTN_WIDTH), BF16),
        compiler_params=_params("arbitrary"),
        name="context_attention",
    )(sink, hc, hc, hc)


def _dft_mats(n):
    k = np.arange(n, dtype=np.int64)
    ang = 2.0 * np.pi * ((k[:, None] * k[None, :]) % n).astype(np.float64) / n
    return np.cos(ang), np.sin(ang)


def _group_dft():
    c, s = _dft_mats(FOURIER_GROUP_DIM)
    groups = FOURIER_WIDTH // FOURIER_GROUP_DIM
    eye = np.eye(groups)
    return np.concatenate([np.kron(eye, c), np.kron(eye, s)], axis=1)


def _fourier_kernel(f_ref, w1_ref, c_ref, s_ref, o_ref, ab_ref, *, scale):
    @pl.when(pl.program_id(1) == 0)
    def _():
        ab_ref[...] = jnp.dot(f_ref[...], w1_ref[...], preferred_element_type=F32).astype(BF16)

    y = (jnp.dot(c_ref[...], ab_ref[:, :FOURIER_WIDTH], preferred_element_type=F32)
         - jnp.dot(s_ref[...], ab_ref[:, FOURIER_WIDTH:], preferred_element_type=F32))
    o_ref[...] = (y * scale).astype(BF16)


def _fourier_mix(h, batch, n):
    tr = 256
    cn, sn = _dft_mats(n)
    fcol = (EVEN_IN_WIDTH - FOURIER_WIDTH) // FOURIER_WIDTH
    return pl.pallas_call(
        functools.partial(_fourier_kernel, scale=float((n * FOURIER_GROUP_DIM) ** -0.5)),
        grid=(batch, n // tr),
        in_specs=[
            pl.BlockSpec((n, FOURIER_WIDTH), lambda b, i: (b, fcol)),
            pl.BlockSpec((FOURIER_WIDTH, 2 * FOURIER_WIDTH), lambda b, i: (0, 0)),
            pl.BlockSpec((tr, n), lambda b, i: (i, 0)),
            pl.BlockSpec((tr, n), lambda b, i: (i, 0)),
        ],
        out_specs=pl.BlockSpec((tr, FOURIER_WIDTH), lambda b, i: (b * (n // tr) + i, 0)),
        out_shape=jax.ShapeDtypeStruct((batch * n, FOURIER_WIDTH), BF16),
        scratch_shapes=[pltpu.VMEM((n, 2 * FOURIER_WIDTH), BF16)],
        compiler_params=_params("arbitrary", "arbitrary"),
        name="fourier_mix",
    )(h, jnp.asarray(_group_dft(), BF16), jnp.asarray(cn, BF16), jnp.asarray(sn, BF16))


def _inproj_odd_kernel(x_ref, sc_ref, sh_ref, w_ref, o_ref):
    u = (x_ref[...] * (1.0 + sc_ref[0]) + sh_ref[0]).astype(BF16)
    tn = 512
    for jt in range(D_MODEL // tn):
        cols = slice(jt * tn, (jt + 1) * tn)
        b_gate = jnp.dot(u, w_ref[:, jt * tn:(jt + 1) * tn], preferred_element_type=F32)
        o_ref[:, cols] = b_gate.astype(BF16)
        c_gate = jnp.dot(u, w_ref[:, D_MODEL + jt * tn:D_MODEL + (jt + 1) * tn], preferred_element_type=F32)
        hh = jnp.dot(u, w_ref[:, 2 * D_MODEL + jt * tn:2 * D_MODEL + (jt + 1) * tn], preferred_element_type=F32)
        o_ref[:, D_MODEL + jt * tn:D_MODEL + (jt + 1) * tn] = (c_gate * hh).astype(BF16)


def _inproj_odd(x2, sc, sh, w, seq_rows, mod_row):
    t = x2.shape[0]
    tm = min(ROW_TILE, seq_rows)
    mod = _mod_spec(tm, seq_rows, mod_row)

    return pl.pallas_call(
        _inproj_odd_kernel,
        grid=(t // tm,),
        in_specs=[
            pl.BlockSpec((tm, D_MODEL), lambda i: (i, 0)),
            mod,
            mod,
            pl.BlockSpec((D_MODEL, 3 * D_MODEL), lambda i: (0, 0)),
        ],
        out_specs=pl.BlockSpec((tm, 2 * D_MODEL), lambda i: (i, 0)),
        out_shape=jax.ShapeDtypeStruct((t, 2 * D_MODEL), BF16),
        compiler_params=_params("arbitrary"),
        name="inproj_odd",
    )(x2, sc, sh, w)


def _layer_norm(r, g, b):
    mu = jnp.mean(r, axis=-1, keepdims=True)
    d = r - mu
    var = jnp.mean(d * d, axis=-1, keepdims=True)
    return d * lax.rsqrt(var + LN_EPS) * g + b


def _store_token_tiles(ref, rows):
    m = rows.shape[0]
    for c in range(TOKEN_SUBLANES):
        ref[pl.ds(c, m, stride=TOKEN_SUBLANES), :] = rows[:, c * LANES:(c + 1) * LANES]


def _load_token_tiles(ref, m, first_row=0):
    return jnp.concatenate([ref[pl.ds(first_row + c, m, stride=TOKEN_SUBLANES), :] for c in range(TOKEN_SUBLANES)], axis=1)


def _token_tile(ref, t):
    row = t * TOKEN_SUBLANES
    return ref.at[pl.ds(row if isinstance(row, int) else pl.multiple_of(row, TOKEN_SUBLANES), TOKEN_SUBLANES), :]


def _post_mixer(y, x_ref, g1_ref, sc2_ref, sh2_ref, lng_ref, lnb_ref, wrh_ref, wrl_ref, br_ref, xo_ref, v_ref, lg_ref):
    xn = _layer_norm(DEEPNORM_ALPHA * x_ref[...] + g1_ref[0] * y, lng_ref[...], lnb_ref[...])
    xo_ref[...] = xn
    v = xn * (1.0 + sc2_ref[0]) + sh2_ref[0]
    _store_token_tiles(v_ref, v)
    v_hi = v.astype(BF16)
    v_lo = (v - v_hi.astype(F32)).astype(BF16)
    lg_ref[...] = (jnp.dot(v_hi, wrh_ref[...], preferred_element_type=F32)
                   + (jnp.dot(v_lo, wrh_ref[...], preferred_element_type=F32)
                      + jnp.dot(v_hi, wrl_ref[...], preferred_element_type=F32))) + br_ref[...]


def _outproj_even_kernel(a_ref, f_ref, w_ref, *rest):
    y = (jnp.dot(a_ref[...], w_ref[:ATTN_WIDTH, :], preferred_element_type=F32)
         + jnp.dot(f_ref[...], w_ref[ATTN_WIDTH:, :], preferred_element_type=F32))
    _post_mixer(y, *rest)


def _outproj_odd_kernel(b_ref, z_ref, zp_ref, zn_ref, cw_ref, w_ref, *rest, rows_per_seq):
    tm = z_ref.shape[0]
    halo = zp_ref.shape[0]
    row0 = pl.program_id(0) * tm
    z = z_ref[...].astype(F32)
    prev_row = jnp.where(row0 % rows_per_seq == 0, 0.0, zp_ref[halo - 1:halo, :].astype(F32))
    next_row = jnp.where((row0 + tm) % rows_per_seq == 0, 0.0, zn_ref[0:1, :].astype(F32))
    r = lax.broadcasted_iota(jnp.int32, z.shape, 0)
    z_prev = jnp.where(r == 0, prev_row, pltpu.roll(z, 1, axis=0))
    z_next = jnp.where(r == tm - 1, next_row, pltpu.roll(z, tm - 1, axis=0))
    conv = cw_ref[0:1, :] * z_prev + cw_ref[1:2, :] * z + cw_ref[2:3, :] * z_next
    a = (b_ref[...].astype(F32) * conv).astype(BF16)
    y = jnp.dot(a, w_ref[...], preferred_element_type=F32)
    _post_mixer(y, *rest)


def _outproj(kind, mix_inputs, w_out, x2, g1, sc2, sh2, lng, lnb, wr_hi, wr_lo, br, seq_rows, mod_row, conv_w=None):
    t = x2.shape[0]
    tm = min(ROW_TILE, seq_rows)
    mod = _mod_spec(tm, seq_rows, mod_row)

    row = lambda width: pl.BlockSpec((tm, width), lambda i: (i, 0))
    const = lambda shape: pl.BlockSpec(shape, lambda i: (0,) * len(shape))
    if kind == "even":
        attn, four = mix_inputs
        kern = _outproj_even_kernel
        head_specs = [row(ATTN_WIDTH), row(FOURIER_WIDTH)]
        head_args = [attn, four]
    else:
        (bz,) = mix_inputs
        halo = 16
        last = t // halo - 1
        kern = functools.partial(_outproj_odd_kernel, rows_per_seq=seq_rows)
        head_specs = [
            pl.BlockSpec((tm, D_MODEL), lambda i: (i, 0)),
            pl.BlockSpec((tm, D_MODEL), lambda i: (i, 1)),
            pl.BlockSpec((halo, D_MODEL), lambda i: (jnp.maximum(i * (tm // halo) - 1, 0), 1)),
            pl.BlockSpec((halo, D_MODEL), lambda i: (jnp.minimum((i + 1) * (tm // halo), last), 1)),
            const((8, D_MODEL)),
        ]
        head_args = [bz, bz, bz, bz, conv_w]
    return pl.pallas_call(
        kern,
        grid=(t // tm,),
        in_specs=head_specs + [
            const((D_MODEL, D_MODEL)),
            row(D_MODEL),
            mod,
            mod,
            mod,
            const((1, D_MODEL)), const((1, D_MODEL)),
            const((D_MODEL, LANES)), const((D_MODEL, LANES)), const((1, LANES)),
        ],
        out_specs=[row(D_MODEL), pl.BlockSpec((tm * TOKEN_SUBLANES, LANES), lambda i: (i, 0)), row(LANES)],
        out_shape=[
            jax.ShapeDtypeStruct((t, D_MODEL), F32),
            jax.ShapeDtypeStruct((t * TOKEN_SUBLANES, LANES), F32),
            jax.ShapeDtypeStruct((t, LANES), F32),
        ],
        compiler_params=_params("arbitrary"),
        name="outproj_" + kind,
    )(*head_args, w_out, x2, g1, sc2, sh2, lng, lnb, wr_hi, wr_lo, br)


def _route_kernel(lg_ref, tri_ref, idx_ref, gate_ref, cnt_ref, carry_ref):
    @pl.when(pl.program_id(0) == 0)
    def _():
        carry_ref[...] = jnp.zeros_like(carry_ref)

    work = lg_ref[...]
    lane = lax.broadcasted_iota(jnp.int32, work.shape, 1).astype(F32)
    sels, vals, ids = [], [], []
    for _ in range(TOP_K):
        m = jnp.max(work, axis=1, keepdims=True)
        first = jnp.min(jnp.where(work == m, lane, float(LANES)), axis=1, keepdims=True)
        sel = lane == first
        work = jnp.where(sel, -jnp.inf, work)
        sels.append(sel)
        vals.append(m)
        ids.append(first)
    onehot = jnp.where(sels[0] | sels[1] | sels[2] | sels[3], 1.0, 0.0)
    before = jnp.dot(tri_ref[...], onehot.astype(BF16), preferred_element_type=F32) + carry_ref[...]
    carry_ref[...] = carry_ref[...] + jnp.sum(onehot, axis=0, keepdims=True)
    cnt_ref[...] = carry_ref[...]
    exps = [jnp.exp(v - vals[0]) for v in vals]
    denom = exps[0] + exps[1] + exps[2] + exps[3]
    idx_out = jnp.zeros_like(work)
    gate_out = jnp.zeros_like(work)
    for k in range(TOP_K):
        rank = jnp.sum(jnp.where(sels[k], before, 0.0), axis=1, keepdims=True)
        idx_out = jnp.where(lane == k, ids[k], jnp.where(lane == TOP_K + k, rank, idx_out))
        gate_out = jnp.where(lane == k, exps[k] / denom, gate_out)
    idx_ref[...] = idx_out.astype(jnp.int32)
    gate_ref[...] = gate_out


def _route(logits):
    t = logits.shape[0]
    tm = ROUTE_TILE
    tri = jnp.asarray(np.tril(np.ones((tm, tm), np.float32), -1), BF16)
    row = pl.BlockSpec((tm, LANES), lambda i: (i, 0))
    return pl.pallas_call(
        _route_kernel,
        grid=(t // tm,),
        in_specs=[row, pl.BlockSpec((tm, tm), lambda i: (0, 0))],
        out_specs=[row, row, pl.BlockSpec((1, LANES), lambda i: (0, 0))],
        out_shape=[
            jax.ShapeDtypeStruct((t, LANES), jnp.int32),
            jax.ShapeDtypeStruct((t, LANES), F32),
            jax.ShapeDtypeStruct((1, LANES), F32),
        ],
        scratch_shapes=[pltpu.VMEM((1, LANES), F32)],
        compiler_params=_params("arbitrary"),
        name="route",
    )(logits, tri)


def _zero_fill(fill_start_ref, fill_len_ref, n_used_ref, buf_hbm, zero_vmem, sem_fill):
    n_blk = buf_hbm.shape[0] // (EXPERT_ROWS * TOKEN_SUBLANES)
    zero_vmem[...] = jnp.zeros_like(zero_vmem)
    chunks = [1 << b for b in reversed(range(EXPERT_ROWS.bit_length() - 1))]

    def pads(wait):
        def body(e, carry):
            start, length = fill_start_ref[e], fill_len_ref[e]
            for c in chunks:
                @pl.when((length & c) != 0)
                def _():
                    row = pl.multiple_of((start + (length & (-2 * c))) * TOKEN_SUBLANES, TOKEN_SUBLANES)
                    cp = pltpu.make_async_copy(zero_vmem.at[pl.ds(0, c * TOKEN_SUBLANES), :],
                                               buf_hbm.at[pl.ds(row, c * TOKEN_SUBLANES), :], sem_fill)
                    cp.wait() if wait else cp.start()
            return carry

        lax.fori_loop(0, N_EXPERTS, body, 0)

    def tail(wait):
        def body(j, carry):
            row = pl.multiple_of(j * (EXPERT_ROWS * TOKEN_SUBLANES), EXPERT_ROWS * TOKEN_SUBLANES)
            cp = pltpu.make_async_copy(zero_vmem, buf_hbm.at[pl.ds(row, EXPERT_ROWS * TOKEN_SUBLANES), :], sem_fill)
            cp.wait() if wait else cp.start()
            return carry

        lax.fori_loop(n_used_ref[0], n_blk, body, 0)

    pads(False)
    tail(False)
    pads(True)
    tail(True)


def _dispatch_kernel(fill_start_ref, fill_len_ref, n_used_ref, *rest, tiles):
    v_refs = rest[:len(tiles)]
    pos_hbm, buf_hbm, idx_smem, sem_idx, sem_rows, zero_vmem, sem_fill = rest[len(tiles):]
    tm = v_refs[0].shape[0] // TOKEN_SUBLANES
    per_tile = tm * TOP_K
    i = pl.program_id(0)
    fetch = pltpu.make_async_copy(pos_hbm.at[pl.ds(i * per_tile, per_tile)], idx_smem, sem_idx)
    fetch.start()

    @pl.when(i == 0)
    def _():
        _zero_fill(fill_start_ref, fill_len_ref, n_used_ref, buf_hbm, zero_vmem, sem_fill)

    fetch.wait()

    def scatter(v_ref):
        def row_copy(t, p):
            return pltpu.make_async_copy(_token_tile(v_ref, t), _token_tile(buf_hbm, p), sem_rows)

        def issue(j, carry):
            for u in range(DMA_UNROLL):
                t = j * DMA_UNROLL + u
                for k in range(TOP_K):
                    row_copy(t, idx_smem[t * TOP_K + k]).start(priority=k % 2)
            return carry

        def drain(j, carry):
            for _ in range(DMA_UNROLL * TOP_K):
                row_copy(0, 0).wait()
            return carry

        lax.fori_loop(0, tm // DMA_UNROLL, issue, 0)
        lax.fori_loop(0, tm // DMA_UNROLL, drain, 0)

    first = 0
    for v_ref, n_tiles in zip(v_refs, tiles):
        @pl.when((i >= first) & (i < first + n_tiles))
        def _():
            scatter(v_ref)

        first += n_tiles


def _dispatch(fill_start, fill_len, n_used, v_parts, pos_flat, n_rows):
    tm = DISPATCH_TILE
    tiles = tuple(v.shape[0] // (tm * TOKEN_SUBLANES) for v in v_parts)
    firsts = [sum(tiles[:p]) for p in range(len(tiles))]
    any_spec = pl.BlockSpec(memory_space=pl.ANY)

    def part_spec(first, n_tiles):
        return pl.BlockSpec((tm * TOKEN_SUBLANES, LANES), lambda i, *_: (jnp.clip(i - first, 0, n_tiles - 1), 0))

    grid_spec = pltpu.PrefetchScalarGridSpec(
        num_scalar_prefetch=3,
        grid=(sum(tiles),),
        in_specs=[part_spec(f, n) for f, n in zip(firsts, tiles)] + [any_spec],
        out_specs=any_spec,
        scratch_shapes=[pltpu.SMEM((tm * TOP_K,), jnp.int32), pltpu.SemaphoreType.DMA, pltpu.SemaphoreType.DMA,
                        pltpu.VMEM((EXPERT_ROWS * TOKEN_SUBLANES, LANES), F32), pltpu.SemaphoreType.DMA],
    )
    return pl.pallas_call(
        functools.partial(_dispatch_kernel, tiles=tiles),
        grid_spec=grid_spec,
        out_shape=jax.ShapeDtypeStruct((n_rows * TOKEN_SUBLANES, LANES), F32),
        compiler_params=_params("arbitrary"),
        name="dispatch",
    )(fill_start, fill_len, n_used, *v_parts, pos_flat)


def _pair_shuffle():
    l = np.arange(LANES)
    p = np.zeros((LANES, LANES), np.float32)
    p[l, l // 2 + (LANES // 2) * (l % 2)] = 1.0
    return p


def _expert_kernel(blk_e_ref, n_used_ref, run_ref, next_ref, x_ref, bu_ref, bd_ref, perm_ref, wu_hbm, wd_hbm, o_ref,
                   wu_f32, wd_f32, wu_bf, wd_bf, sem_u, sem_d, *, layer):
    i = pl.program_id(0)
    active = i < n_used_ref[0]
    expert = blk_e_ref[i]
    new_expert = (i == 0) | (expert != blk_e_ref[jnp.maximum(i - 1, 0)])

    def fetch(e, slot):
        return (pltpu.make_async_copy(wu_hbm.at[layer, e], wu_f32.at[slot], sem_u.at[slot]),
                pltpu.make_async_copy(wd_hbm.at[layer, e], wd_f32.at[slot], sem_d.at[slot]))

    @pl.when(active & new_expert)
    def _():
        slot = run_ref[expert] % 2

        @pl.when(i == 0)
        def _():
            for cp in fetch(expert, slot):
                cp.start()

        for cp in fetch(expert, slot):
            cp.wait()
        nxt = next_ref[expert]

        @pl.when(nxt >= 0)
        def _():
            for cp in fetch(nxt, 1 - slot):
                cp.start()

        def cast(c, carry):
            r = pl.multiple_of(c * LANES, LANES)
            wu_bf[pl.ds(r, LANES), :] = wu_f32[slot, pl.ds(r, LANES), :].astype(BF16)
            wd_bf[pl.ds(r, LANES), :] = jnp.dot(perm_ref[...], wd_f32[slot, pl.ds(r, LANES), :].astype(BF16),
                                               preferred_element_type=F32).astype(BF16)
            return carry

        lax.fori_loop(0, D_MODEL // LANES, cast, 0)

    @pl.when(active)
    def _():
        h = jnp.dot(_load_token_tiles(x_ref, EXPERT_ROWS).astype(BF16), wu_bf[...], preferred_element_type=F32) + bu_ref[0]
        even = lax.broadcasted_iota(jnp.int32, (EXPERT_ROWS, LANES), 1) % 2 == 0
        acts = []
        for k in range(D_EXPERT // LANES):
            ha = h[:, 2 * k * LANES:(2 * k + 1) * LANES]
            hb = h[:, (2 * k + 1) * LANES:(2 * k + 2) * LANES]
            glu = jnp.where(even, ha, pltpu.roll(hb, 1, axis=1))
            lin = jnp.where(even, pltpu.roll(ha, LANES - 1, axis=1), hb)
            glu = jnp.minimum(glu, SWIGLU_LIMIT)
            lin = jnp.clip(lin, -SWIGLU_LIMIT, SWIGLU_LIMIT)
            acts.append((glu * jax.nn.sigmoid(SWIGLU_ALPHA * glu) * (lin + 1.0)).astype(BF16))
        a = jnp.concatenate(acts, axis=1)
        _store_token_tiles(o_ref, jnp.dot(a, wd_bf[...], preferred_element_type=F32) + bd_ref[0])

    @pl.when(jnp.logical_not(active))
    def _():
        o_ref[...] = jnp.zeros_like(o_ref)


def _experts(blk_e, n_used, run_idx, next_expert, xbuf, layer, wu, bu, wd, bd):
    n_blk = xbuf.shape[0] // (EXPERT_ROWS * TOKEN_SUBLANES)
    grid_spec = pltpu.PrefetchScalarGridSpec(
        num_scalar_prefetch=4,
        grid=(n_blk,),
        in_specs=[
            pl.BlockSpec((EXPERT_ROWS * TOKEN_SUBLANES, LANES), lambda i, e, *_: (i, 0)),
            pl.BlockSpec((1, 1, 2 * D_EXPERT), lambda i, e, *_: (e[i], 0, 0)),
            pl.BlockSpec((1, 1, D_MODEL), lambda i, e, *_: (e[i], 0, 0)),
            pl.BlockSpec((LANES, LANES), lambda i, e, *_: (0, 0)),
            pl.BlockSpec(memory_space=pl.ANY),
            pl.BlockSpec(memory_space=pl.ANY),
        ],
        out_specs=pl.BlockSpec((EXPERT_ROWS * TOKEN_SUBLANES, LANES), lambda i, e, *_: (i, 0)),
        scratch_shapes=[
            pltpu.VMEM((2, D_MODEL, 2 * D_EXPERT), F32),
            pltpu.VMEM((2, D_EXPERT, D_MODEL), F32),
            pltpu.VMEM((D_MODEL, 2 * D_EXPERT), BF16),
            pltpu.VMEM((D_EXPERT, D_MODEL), BF16),
            pltpu.SemaphoreType.DMA((2,)),
            pltpu.SemaphoreType.DMA((2,)),
        ],
    )
    return pl.pallas_call(
        functools.partial(_expert_kernel, layer=layer),
        grid_spec=grid_spec,
        out_shape=jax.ShapeDtypeStruct(xbuf.shape, F32),
        compiler_params=pltpu.CompilerParams(dimension_semantics=("arbitrary",), vmem_limit_bytes=EXPERT_VMEM_LIMIT),
        name="experts",
    )(blk_e, n_used, run_idx, next_expert, xbuf, bu, bd, jnp.asarray(_pair_shuffle(), BF16), wu, wd)


def _combine_kernel(x_ref, gate_ref, g2_ref, lng_ref, lnb_ref, pos_hbm, y_hbm, o_ref, idx_smem, rows, sem_idx, sem_rows):
    tm = x_ref.shape[0]
    per_tile = tm * TOP_K
    i = pl.program_id(0)

    def row_copy(slot, t, k, p):
        return pltpu.make_async_copy(_token_tile(y_hbm, p), _token_tile(rows, (slot * TOP_K + k) * tm + t), sem_rows.at[slot])

    def start_gather(tile, slot):
        fetch = pltpu.make_async_copy(pos_hbm.at[pl.ds(tile * per_tile, per_tile)], idx_smem, sem_idx)
        fetch.start()
        fetch.wait()

        def issue(j, carry):
            for u in range(DMA_UNROLL):
                t = j * DMA_UNROLL + u
                for k in range(TOP_K):
                    row_copy(slot, t, k, idx_smem[t * TOP_K + k]).start(priority=k % 2)
            return carry

        lax.fori_loop(0, tm // DMA_UNROLL, issue, 0)

    @pl.when(i == 0)
    def _():
        start_gather(0, 0)

    @pl.when(i + 1 < pl.num_programs(0))
    def _():
        start_gather(i + 1, (i + 1) % 2)

    slot = i % 2

    def drain(j, carry):
        for _ in range(DMA_UNROLL * TOP_K):
            row_copy(slot, 0, 0, 0).wait()
        return carry

    lax.fori_loop(0, tm // DMA_UNROLL, drain, 0)
    chunks = []
    for c in range(TOKEN_SUBLANES):
        def part(k):
            first = pl.multiple_of((slot * TOP_K + k) * (tm * TOKEN_SUBLANES), tm * TOKEN_SUBLANES) + c
            return gate_ref[:, k:k + 1] * rows[pl.ds(first, tm, stride=TOKEN_SUBLANES), :]

        fc = part(0)
        for k in range(1, TOP_K):
            fc = fc + part(k)
        chunks.append(fc)
    f = jnp.concatenate(chunks, axis=1)
    o_ref[...] = _layer_norm(DEEPNORM_ALPHA * x_ref[...] + g2_ref[0] * f, lng_ref[...], lnb_ref[...])


def _combine(x2, gates, pos_flat, ybuf, g2, lng, lnb, seq_rows, mod_row):
    t = x2.shape[0]
    tm = COMBINE_TILE
    row = pl.BlockSpec((tm, D_MODEL), lambda i: (i, 0))
    const = pl.BlockSpec((1, D_MODEL), lambda i: (0, 0))
    return pl.pallas_call(
        _combine_kernel,
        grid=(t // tm,),
        in_specs=[row, pl.BlockSpec((tm, LANES), lambda i: (i, 0)), _mod_spec(tm, seq_rows, mod_row), const, const,
                  pl.BlockSpec(memory_space=pl.ANY), pl.BlockSpec(memory_space=pl.ANY)],
        out_specs=row,
        out_shape=jax.ShapeDtypeStruct((t, D_MODEL), F32),
        scratch_shapes=[
            pltpu.SMEM((tm * TOP_K,), jnp.int32),
            pltpu.VMEM((2 * TOP_K * tm * TOKEN_SUBLANES, LANES), F32),
            pltpu.SemaphoreType.DMA,
            pltpu.SemaphoreType.DMA((2,)),
        ],
        compiler_params=_params("arbitrary"),
        name="combine",
    )(x2, gates, g2, lng, lnb, pos_flat, ybuf)


def _moe(v_parts, logits, layer, wu, bu, wd, bd):
    t = logits.shape[0]
    n = t * TOP_K
    idx, gates, cnt = _route(logits)
    counts = cnt[0, :N_EXPERTS].astype(jnp.int32)
    padded = (counts + EXPERT_ROWS - 1) // EXPERT_ROWS * EXPERT_ROWS
    pad_ends = jnp.cumsum(padded)
    pad_starts = pad_ends - padded
    experts = jnp.arange(N_EXPERTS, dtype=jnp.int32)
    start = jnp.sum(jnp.where(idx[:, :TOP_K, None] == experts, pad_starts, 0), axis=-1)
    pos_flat = (start + idx[:, TOP_K:2 * TOP_K]).reshape(n)
    n_blk = n // EXPERT_ROWS + N_EXPERTS
    blk_start = jnp.arange(n_blk, dtype=jnp.int32) * EXPERT_ROWS
    blk_e = jnp.minimum(jnp.sum((pad_ends[None, :] <= blk_start[:, None]).astype(jnp.int32), axis=1), N_EXPERTS - 1)
    n_used = pad_ends[-1:] // EXPERT_ROWS
    nonempty = counts > 0
    run_idx = jnp.cumsum(nonempty.astype(jnp.int32)) - 1
    later = nonempty[None, :] & (experts[None, :] > experts[:, None])
    next_expert = jnp.min(jnp.where(later, experts[None, :], N_EXPERTS), axis=1)
    next_expert = jnp.where(next_expert == N_EXPERTS, -1, next_expert)
    fill_start, fill_len = pad_starts + counts, padded - counts
    xbuf = _dispatch(fill_start, fill_len, n_used, v_parts, pos_flat, n_blk * EXPERT_ROWS)
    ybuf = _experts(blk_e, n_used, run_idx, next_expert, xbuf, layer, wu, bu, wd, bd)
    return ybuf, pos_flat, gates


def _rope_tables(seq):
    rows = seq // GRID_W
    row = jnp.repeat(jnp.arange(rows), GRID_W).astype(F32)
    col = jnp.tile(jnp.arange(GRID_W), rows).astype(F32)
    n_freq = HEAD_DIM // 4
    inv = ROPE_BASE ** (-jnp.arange(n_freq, dtype=F32) / n_freq)
    ang = jnp.concatenate([row[:, None] * inv, col[:, None] * inv], -1)
    ang = jnp.concatenate([ang, ang], -1)
    sign = jnp.where(jnp.arange(HEAD_DIM) < HEAD_DIM // 2, -1.0, 1.0).astype(F32)
    cos = jnp.cos(ang)
    sin = jnp.sin(ang) * sign
    return jnp.tile(cos, (1, LANES // HEAD_DIM)), jnp.tile(sin, (1, LANES // HEAD_DIM))


def kernel(x, c, ctx, c_ctx, w_mod, b_mod, w_in_even, sink, w_out_even, w_in_odd, conv_w, w_out_odd, ln_g, ln_b,
           w_router, b_router, w_up, b_up, w_down, b_down):
    B, S, D = x.shape
    L = ctx.shape[1]
    T, TC = B * S, B * L
    cos, sin = _rope_tables(S)
    cond = jnp.concatenate([c, c_ctx[None, :], jnp.zeros((MOD_ROWS - B - 1, D), F32)], 0)
    mod = _modulation(cond, w_mod, b_mod)

    bu = b_up.reshape(DEPTH, N_EXPERTS, 1, 2 * D_EXPERT)
    bd = b_down.reshape(DEPTH, N_EXPERTS, 1, D)
    wr_hi = w_router.astype(BF16)
    wr_lo = (w_router - wr_hi.astype(F32)).astype(BF16)
    pad_r = ((0, 0), (0, 0), (0, LANES - N_EXPERTS))
    wr_hi, wr_lo = jnp.pad(wr_hi, pad_r), jnp.pad(wr_lo, pad_r)
    br = jnp.pad(b_router, ((0, 0), (0, LANES - N_EXPERTS)), constant_values=NEG_INF).reshape(DEPTH, 1, LANES)

    x2 = x.reshape(T, D)
    xc2 = ctx.reshape(TC, D)
    for l in range(DEPTH):
        even = l % 2 == 0
        j = l // 2
        ctx_after = any(m % 2 == 0 for m in range(l + 1, DEPTH))
        sh1, sc1, g1, sh2, sc2, g2 = [m.reshape(MOD_ROWS, 1, D) for m in jnp.split(mod[l], 6, axis=-1)]
        lng1, lnb1 = ln_g[l, 0].reshape(1, D), ln_b[l, 0].reshape(1, D)
        lng2, lnb2 = ln_g[l, 1].reshape(1, D), ln_b[l, 1].reshape(1, D)
        post = (g1, sc2, sh2, lng1, lnb1, wr_hi[l], wr_lo[l], br[l])
        if even:
            w_in = w_in_even[j].astype(BF16)
            w_out = w_out_even[j].astype(BF16)
            h = _inproj_even(x2, sc1, sh1, w_in, cos, sin, S, None, True)
            hc = _inproj_even(xc2, sc1, sh1, w_in, cos, sin, L, B, False)
            kcol = ATTN_WIDTH // KV_WIDTH
            attn = _win_attention(h, hc, sink[j], B, S, kcol, kcol + 1)
            four = _fourier_mix(h, B, S)
            x2, v, logits = _outproj("even", (attn, four), w_out, x2, *post, S, None)
            if ctx_after:
                attn_c = _ctx_attention(hc, sink[j], B)
                four_c = _fourier_mix(hc, B, L)
                xc2, vc, logits_c = _outproj("even", (attn_c, four_c), w_out, xc2, *post, L, B)
        else:
            w_in = w_in_odd[j].astype(BF16)
            w_out = w_out_odd[j].astype(BF16)
            cw = jnp.pad(conv_w[j], ((0, 8 - conv_w.shape[1]), (0, 0)))
            bz = _inproj_odd(x2, sc1, sh1, w_in, S, None)
            x2, v, logits = _outproj("odd", (bz,), w_out, x2, *post, S, None, conv_w=cw)
            if ctx_after:
                bzc = _inproj_odd(xc2, sc1, sh1, w_in, L, B)
                xc2, vc, logits_c = _outproj("odd", (bzc,), w_out, xc2, *post, L, B, conv_w=cw)
        if ctx_after:
            ybuf, pos, gates = _moe((v, vc), jnp.concatenate([logits, logits_c], 0), l, w_up, bu[l], w_down, bd[l])
            xc2 = _combine(xc2, gates[T:], pos[T * TOP_K:], ybuf, g2, lng2, lnb2, L, B)
        else:
            ybuf, pos, gates = _moe((v,), logits, l, w_up, bu[l], w_down, bd[l])
        x2 = _combine(x2, gates[:T], pos[:T * TOP_K], ybuf, g2, lng2, lnb2, S, None)
    return x2.reshape(B, S, D)
```

```python
import functools

import numpy as np
import jax
import jax.numpy as jnp
from jax import lax
from jax.experimental import pallas as pl
from jax.experimental.pallas import tpu as pltpu

D_MODEL = 1024
DEPTH = 4
GRID_W = 64
HEAD_DIM = 64
N_Q_HEADS = 12
N_KV_HEADS = 4
Q_PER_KV = N_Q_HEADS // N_KV_HEADS
BLOCK = 128
ROPE_BASE = 10000.0
FOURIER_GROUP_DIM = 64
FOURIER_WIDTH = 256
ATTN_WIDTH = N_Q_HEADS * HEAD_DIM
KV_WIDTH = N_KV_HEADS * HEAD_DIM
EVEN_IN_WIDTH = ATTN_WIDTH + 2 * KV_WIDTH + FOURIER_WIDTH
N_EXPERTS = 32
TOP_K = 4
D_EXPERT = D_MODEL
SWIGLU_LIMIT = 7.0
SWIGLU_ALPHA = 1.702
LN_EPS = 1e-5
NEG_INF = -1e30
DEEPNORM_ALPHA = (2 * DEPTH) ** 0.25

LANES = 128
SUBLANES = 8
TOKEN_SUBLANES = D_MODEL // LANES
ROW_TILE = 512
EXPERT_ROWS = 512
MOD_ROWS = 16
ROUTE_TILE = 512
DISPATCH_TILE = 1024
COMBINE_TILE = 512
DMA_UNROLL = 8
VMEM_LIMIT = 48 * 1024 * 1024
EXPERT_VMEM_LIMIT = 56 * 1024 * 1024

F32 = jnp.float32
BF16 = jnp.bfloat16


def _params(*sem):
    return pltpu.CompilerParams(dimension_semantics=sem, vmem_limit_bytes=VMEM_LIMIT)


def _mod_spec(tm, seq_rows, mod_row):
    if mod_row is None:
        return pl.BlockSpec((1, 1, D_MODEL), lambda i: ((i * tm) // seq_rows, 0, 0))
    return pl.BlockSpec((1, 1, D_MODEL), lambda i: (mod_row, 0, 0))


def _mod_kernel(c_ref, w_ref, b_ref, o_ref):
    c = c_ref[...]
    s = (c * jax.nn.sigmoid(c)).astype(BF16)
    o_ref[0] = jnp.dot(s, w_ref[0].astype(BF16), preferred_element_type=F32) + b_ref[0]


def _modulation(cond, w_mod, b_mod):
    tn = 1536
    n = w_mod.shape[-1]
    return pl.pallas_call(
        _mod_kernel,
        grid=(DEPTH, n // tn),
        in_specs=[
            pl.BlockSpec((MOD_ROWS, D_MODEL), lambda l, j: (0, 0)),
            pl.BlockSpec((1, D_MODEL, tn), lambda l, j: (l, 0, j)),
            pl.BlockSpec((1, 1, tn), lambda l, j: (l, 0, j)),
        ],
        out_specs=pl.BlockSpec((1, MOD_ROWS, tn), lambda l, j: (l, 0, j)),
        out_shape=jax.ShapeDtypeStruct((DEPTH, MOD_ROWS, n), F32),
        compiler_params=_params("arbitrary", "arbitrary"),
        name="modulation",
    )(cond, w_mod, b_mod.reshape(DEPTH, 1, n))


def _rope_chunk(h, cos, sin_signed, first_half):
    swapped = jnp.where(first_half, pltpu.roll(h, LANES - HEAD_DIM // 2, axis=1), pltpu.roll(h, HEAD_DIM // 2, axis=1))
    return h * cos + swapped * sin_signed


def _inproj_even_kernel(x_ref, sc_ref, sh_ref, w_ref, cos_ref, sin_ref, o_ref, *, rope):
    u = (x_ref[...] * (1.0 + sc_ref[0]) + sh_ref[0]).astype(BF16)
    tn = 512
    if rope:
        cos = cos_ref[...]
        sin = sin_ref[...]
        lane = lax.broadcasted_iota(jnp.int32, cos.shape, 1)
        first_half = (lane % HEAD_DIM) < HEAD_DIM // 2
    scale = HEAD_DIM ** -0.5
    for jt in range(EVEN_IN_WIDTH // tn):
        acc = jnp.dot(u, w_ref[:, jt * tn:(jt + 1) * tn], preferred_element_type=F32)
        for k in range(tn // LANES):
            col = jt * tn + k * LANES
            h = acc[:, k * LANES:(k + 1) * LANES]
            if rope and col < ATTN_WIDTH + KV_WIDTH:
                h = _rope_chunk(h, cos, sin, first_half)
            if col < ATTN_WIDTH:
                h = h * scale
            o_ref[:, col:col + LANES] = h.astype(BF16)


def _inproj_even(x2, sc, sh, w, cos, sin, seq_rows, mod_row, rope):
    t = x2.shape[0]
    tm = min(ROW_TILE, seq_rows)
    pos_tiles = cos.shape[0] // tm
    mod = _mod_spec(tm, seq_rows, mod_row)

    return pl.pallas_call(
        functools.partial(_inproj_even_kernel, rope=rope),
        grid=(t // tm,),
        in_specs=[
            pl.BlockSpec((tm, D_MODEL), lambda i: (i, 0)),
            mod,
            mod,
            pl.BlockSpec((D_MODEL, EVEN_IN_WIDTH), lambda i: (0, 0)),
            pl.BlockSpec((tm, LANES), lambda i: (i % pos_tiles, 0)),
            pl.BlockSpec((tm, LANES), lambda i: (i % pos_tiles, 0)),
        ],
        out_specs=pl.BlockSpec((tm, EVEN_IN_WIDTH), lambda i: (i, 0)),
        out_shape=jax.ShapeDtypeStruct((t, EVEN_IN_WIDTH), BF16),
        compiler_params=_params("arbitrary"),
        name="inproj_even",
    )(x2, sc, sh, w, cos, sin)


def _attend(q_ref, k_all, v_all, valid, sink_ref, o_ref):
    rows = q_ref.shape[0]
    lane = lax.broadcasted_iota(jnp.int32, (rows, LANES), 1)
    low = lane < HEAD_DIM
    scores, sks = [], []
    for kv in range(N_KV_HEADS):
        kv_chunk, kv_half = divmod(kv, 2)
        keep = low if kv_half == 0 else jnp.logical_not(low)
        stack, sinks = [], []
        for g in range(Q_PER_KV):
            head = kv * Q_PER_KV + g
            chunk, half = divmod(head, 2)
            qh = q_ref[:, chunk * LANES:(chunk + 1) * LANES].astype(F32)
            if half != kv_half:
                qh = pltpu.roll(qh, HEAD_DIM, axis=1)
            stack.append(jnp.where(keep, qh, 0.0).astype(BF16))
            sinks.append(jnp.full((rows, 1), sink_ref[head], F32))
        qs = jnp.concatenate(stack, axis=0)
        sks.append(jnp.concatenate(sinks, axis=0))
        kc = k_all[:, kv_chunk * LANES:(kv_chunk + 1) * LANES]
        s = lax.dot_general(qs, kc, (((1,), (1,)), ((), ())), preferred_element_type=F32)
        if valid is not None:
            s = jnp.concatenate([s[:, j * BLOCK:(j + 1) * BLOCK] if ok is None
                                 else jnp.where(ok, s[:, j * BLOCK:(j + 1) * BLOCK], NEG_INF)
                                 for j, ok in enumerate(valid)], axis=1)
        scores.append(s)
    probs, sink_terms = [], []
    for kv in range(N_KV_HEADS):
        m = jnp.maximum(sks[kv], jnp.max(scores[kv], axis=-1, keepdims=True))
        probs.append(jnp.exp(scores[kv] - m).astype(BF16))
        sink_terms.append(jnp.exp(sks[kv] - m))
    key_lane = lax.broadcasted_iota(jnp.int32, (v_all.shape[0], LANES), 1)
    for kv in range(N_KV_HEADS):
        kv_chunk, kv_half = divmod(kv, 2)
        vc = v_all[:, kv_chunk * LANES:(kv_chunk + 1) * LANES]
        own = (key_lane < HEAD_DIM) if kv_half == 0 else (key_lane >= HEAD_DIM)
        o = jnp.dot(probs[kv], jnp.where(own, vc, jnp.ones_like(vc)), preferred_element_type=F32)
        other = (1 - kv_half) * HEAD_DIM
        o = o / (o[:, other:other + 1] + sink_terms[kv])
        for g in range(Q_PER_KV):
            head = kv * Q_PER_KV + g
            half = head % 2
            oh = o[g * rows:(g + 1) * rows]
            if half != kv_half:
                oh = pltpu.roll(oh, HEAD_DIM, axis=1)
            o_ref[:, head * HEAD_DIM:(head + 1) * HEAD_DIM] = oh[:, half * HEAD_DIM:(half + 1) * HEAD_DIM].astype(BF16)


def _win_attn_kernel(sink_ref, q_ref, kp_ref, kc_ref, kn_ref, vp_ref, vc_ref, vn_ref, kx_ref, vx_ref, o_ref, *, nb):
    n = pl.program_id(1)
    k_all = jnp.concatenate([kp_ref[...], kc_ref[...], kn_ref[...], kx_ref[...]], axis=0)
    v_all = jnp.concatenate([vp_ref[...], vc_ref[...], vn_ref[...], vx_ref[...]], axis=0)
    r = lax.broadcasted_iota(jnp.int32, (Q_PER_KV * BLOCK, BLOCK), 0) % BLOCK
    c = lax.broadcasted_iota(jnp.int32, (Q_PER_KV * BLOCK, BLOCK), 1)
    in_prev = c >= jnp.where(n > 0, r, BLOCK)
    in_next = c <= jnp.where(n < nb - 1, r, -1)
    valid = [in_prev, None, in_next] + [None] * (kx_ref.shape[0] // BLOCK)
    _attend(q_ref, k_all, v_all, valid, sink_ref, o_ref)


def _win_attention(h, hc, sink, batch, seq, kc_col, vc_col):
    nb = seq // BLOCK
    qcol, kcol, vcol = 0, ATTN_WIDTH // KV_WIDTH, ATTN_WIDTH // KV_WIDTH + 1

    def blk(shift, col):
        return pl.BlockSpec((BLOCK, KV_WIDTH), lambda b, n: (b * nb + jnp.clip(n + shift, 0, nb - 1), col))

    return pl.pallas_call(
        functools.partial(_win_attn_kernel, nb=nb),
        grid=(batch, nb),
        in_specs=[
            pl.BlockSpec(memory_space=pltpu.SMEM),
            pl.BlockSpec((BLOCK, ATTN_WIDTH), lambda b, n: (b * nb + n, qcol)),
            blk(-1, kcol), blk(0, kcol), blk(1, kcol),
            blk(-1, vcol), blk(0, vcol), blk(1, vcol),
            pl.BlockSpec((hc.shape[0] // batch, KV_WIDTH), lambda b, n: (b, kc_col)),
            pl.BlockSpec((hc.shape[0] // batch, KV_WIDTH), lambda b, n: (b, vc_col)),
        ],
        out_specs=pl.BlockSpec((BLOCK, ATTN_WIDTH), lambda b, n: (b * nb + n, 0)),
        out_shape=jax.ShapeDtypeStruct((batch * seq, ATTN_WIDTH), BF16),
        compiler_params=_params("arbitrary", "arbitrary"),
        name="window_attention",
    )(sink, h, h, h, h, h, h, h, hc, hc)


def _ctx_attn_kernel(sink_ref, q_ref, k_ref, v_ref, o_ref):
    _attend(q_ref, k_ref[...], v_ref[...], None, sink_ref, o_ref)


def _ctx_attention(hc, sink, batch):
    ctx_len = hc.shape[0] // batch
    kcol, vcol = ATTN_WIDTH // KV_WIDTH, ATTN_WIDTH // KV_WIDTH + 1
    return pl.pallas_call(
        _ctx_attn_kernel,
        grid=(batch,),
        in_specs=[
            pl.BlockSpec(memory_space=pltpu.SMEM),
            pl.BlockSpec((ctx_len, ATTN_WIDTH), lambda b: (b, 0)),
            pl.BlockSpec((ctx_len, KV_WIDTH), lambda b: (b, kcol)),
            pl.BlockSpec((ctx_len, KV_WIDTH), lambda b: (b, vcol)),
        ],
        out_specs=pl.BlockSpec((ctx_len, ATTN_WIDTH), lambda b: (b, 0)),
        out_shape=jax.ShapeDtypeStruct((hc.shape[0], ATTN_WIDTH), BF16),
        compiler_params=_params("arbitrary"),
        name="context_attention",
    )(sink, hc, hc, hc)


def _dft_mats(n):
    k = np.arange(n, dtype=np.int64)
    ang = 2.0 * np.pi * ((k[:, None] * k[None, :]) % n).astype(np.float64) / n
    return np.cos(ang), np.sin(ang)


def _group_dft():
    c, s = _dft_mats(FOURIER_GROUP_DIM)
    groups = FOURIER_WIDTH // FOURIER_GROUP_DIM
    eye = np.eye(groups)
    return np.concatenate([np.kron(eye, c), np.kron(eye, s)], axis=1)


def _fourier_kernel(f_ref, w1_ref, c_ref, s_ref, o_ref, ab_ref, *, scale):
    @pl.when(pl.program_id(1) == 0)
    def _():
        ab_ref[...] = jnp.dot(f_ref[...], w1_ref[...], preferred_element_type=F32).astype(BF16)

    y = (jnp.dot(c_ref[...], ab_ref[:, :FOURIER_WIDTH], preferred_element_type=F32)
         - jnp.dot(s_ref[...], ab_ref[:, FOURIER_WIDTH:], preferred_element_type=F32))
    o_ref[...] = (y * scale).astype(BF16)


def _fourier_mix(h, batch, n):
    tr = 256
    cn, sn = _dft_mats(n)
    fcol = (EVEN_IN_WIDTH - FOURIER_WIDTH) // FOURIER_WIDTH
    return pl.pallas_call(
        functools.partial(_fourier_kernel, scale=float((n * FOURIER_GROUP_DIM) ** -0.5)),
        grid=(batch, n // tr),
        in_specs=[
            pl.BlockSpec((n, FOURIER_WIDTH), lambda b, i: (b, fcol)),
            pl.BlockSpec((FOURIER_WIDTH, 2 * FOURIER_WIDTH), lambda b, i: (0, 0)),
            pl.BlockSpec((tr, n), lambda b, i: (i, 0)),
            pl.BlockSpec((tr, n), lambda b, i: (i, 0)),
        ],
        out_specs=pl.BlockSpec((tr, FOURIER_WIDTH), lambda b, i: (b * (n // tr) + i, 0)),
        out_shape=jax.ShapeDtypeStruct((batch * n, FOURIER_WIDTH), BF16),
        scratch_shapes=[pltpu.VMEM((n, 2 * FOURIER_WIDTH), BF16)],
        compiler_params=_params("arbitrary", "arbitrary"),
        name="fourier_mix",
    )(h, jnp.asarray(_group_dft(), BF16), jnp.asarray(cn, BF16), jnp.asarray(sn, BF16))


def _inproj_odd_kernel(x_ref, sc_ref, sh_ref, w_ref, o_ref):
    u = (x_ref[...] * (1.0 + sc_ref[0]) + sh_ref[0]).astype(BF16)
    tn = 512
    for jt in range(D_MODEL // tn):
        cols = slice(jt * tn, (jt + 1) * tn)
        b_gate = jnp.dot(u, w_ref[:, jt * tn:(jt + 1) * tn], preferred_element_type=F32)
        o_ref[:, cols] = b_gate.astype(BF16)
        c_gate = jnp.dot(u, w_ref[:, D_MODEL + jt * tn:D_MODEL + (jt + 1) * tn], preferred_element_type=F32)
        hh = jnp.dot(u, w_ref[:, 2 * D_MODEL + jt * tn:2 * D_MODEL + (jt + 1) * tn], preferred_element_type=F32)
        o_ref[:, D_MODEL + jt * tn:D_MODEL + (jt + 1) * tn] = (c_gate * hh).astype(BF16)


def _inproj_odd(x2, sc, sh, w, seq_rows, mod_row):
    t = x2.shape[0]
    tm = min(ROW_TILE, seq_rows)
    mod = _mod_spec(tm, seq_rows, mod_row)

    return pl.pallas_call(
        _inproj_odd_kernel,
        grid=(t // tm,),
        in_specs=[
            pl.BlockSpec((tm, D_MODEL), lambda i: (i, 0)),
            mod,
            mod,
            pl.BlockSpec((D_MODEL, 3 * D_MODEL), lambda i: (0, 0)),
        ],
        out_specs=pl.BlockSpec((tm, 2 * D_MODEL), lambda i: (i, 0)),
        out_shape=jax.ShapeDtypeStruct((t, 2 * D_MODEL), BF16),
        compiler_params=_params("arbitrary"),
        name="inproj_odd",
    )(x2, sc, sh, w)


def _layer_norm(r, g, b):
    mu = jnp.mean(r, axis=-1, keepdims=True)
    d = r - mu
    var = jnp.mean(d * d, axis=-1, keepdims=True)
    return d * lax.rsqrt(var + LN_EPS) * g + b


def _store_token_tiles(ref, rows):
    m = rows.shape[0]
    for c in range(TOKEN_SUBLANES):
        ref[pl.ds(c, m, stride=TOKEN_SUBLANES), :] = rows[:, c * LANES:(c + 1) * LANES]


def _load_token_tiles(ref, m, first_row=0):
    return jnp.concatenate([ref[pl.ds(first_row + c, m, stride=TOKEN_SUBLANES), :] for c in range(TOKEN_SUBLANES)], axis=1)


def _token_tile(ref, t):
    row = t * TOKEN_SUBLANES
    return ref.at[pl.ds(row if isinstance(row, int) else pl.multiple_of(row, TOKEN_SUBLANES), TOKEN_SUBLANES), :]


def _post_mixer(y, x_ref, g1_ref, sc2_ref, sh2_ref, lng_ref, lnb_ref, wrh_ref, wrl_ref, br_ref, xo_ref, v_ref, lg_ref):
    xn = _layer_norm(DEEPNORM_ALPHA * x_ref[...] + g1_ref[0] * y, lng_ref[...], lnb_ref[...])
    xo_ref[...] = xn
    v = xn * (1.0 + sc2_ref[0]) + sh2_ref[0]
    _store_token_tiles(v_ref, v)
    v_hi = v.astype(BF16)
    v_lo = (v - v_hi.astype(F32)).astype(BF16)
    lg_ref[...] = (jnp.dot(v_hi, wrh_ref[...], preferred_element_type=F32)
                   + (jnp.dot(v_lo, wrh_ref[...], preferred_element_type=F32)
                      + jnp.dot(v_hi, wrl_ref[...], preferred_element_type=F32))) + br_ref[...]


def _outproj_even_kernel(a_ref, f_ref, w_ref, *rest):
    y = (jnp.dot(a_ref[...], w_ref[:ATTN_WIDTH, :], preferred_element_type=F32)
         + jnp.dot(f_ref[...], w_ref[ATTN_WIDTH:, :], preferred_element_type=F32))
    _post_mixer(y, *rest)


def _outproj_odd_kernel(b_ref, z_ref, zp_ref, zn_ref, cw_ref, w_ref, *rest, rows_per_seq):
    tm = z_ref.shape[0]
    halo = zp_ref.shape[0]
    row0 = pl.program_id(0) * tm
    z = z_ref[...].astype(F32)
    prev_row = jnp.where(row0 % rows_per_seq == 0, 0.0, zp_ref[halo - 1:halo, :].astype(F32))
    next_row = jnp.where((row0 + tm) % rows_per_seq == 0, 0.0, zn_ref[0:1, :].astype(F32))
    r = lax.broadcasted_iota(jnp.int32, z.shape, 0)
    z_prev = jnp.where(r == 0, prev_row, pltpu.roll(z, 1, axis=0))
    z_next = jnp.where(r == tm - 1, next_row, pltpu.roll(z, tm - 1, axis=0))
    conv = cw_ref[0:1, :] * z_prev + cw_ref[1:2, :] * z + cw_ref[2:3, :] * z_next
    a = (b_ref[...].astype(F32) * conv).astype(BF16)
    y = jnp.dot(a, w_ref[...], preferred_element_type=F32)
    _post_mixer(y, *rest)


def _outproj(kind, mix_inputs, w_out, x2, g1, sc2, sh2, lng, lnb, wr_hi, wr_lo, br, seq_rows, mod_row, conv_w=None):
    t = x2.shape[0]
    tm = min(ROW_TILE, seq_rows)
    mod = _mod_spec(tm, seq_rows, mod_row)

    row = lambda width: pl.BlockSpec((tm, width), lambda i: (i, 0))
    const = lambda shape: pl.BlockSpec(shape, lambda i: (0,) * len(shape))
    if kind == "even":
        attn, four = mix_inputs
        kern = _outproj_even_kernel
        head_specs = [row(ATTN_WIDTH), row(FOURIER_WIDTH)]
        head_args = [attn, four]
    else:
        (bz,) = mix_inputs
        halo = 16
        last = t // halo - 1
        kern = functools.partial(_outproj_odd_kernel, rows_per_seq=seq_rows)
        head_specs = [
            pl.BlockSpec((tm, D_MODEL), lambda i: (i, 0)),
            pl.BlockSpec((tm, D_MODEL), lambda i: (i, 1)),
            pl.BlockSpec((halo, D_MODEL), lambda i: (jnp.maximum(i * (tm // halo) - 1, 0), 1)),
            pl.BlockSpec((halo, D_MODEL), lambda i: (jnp.minimum((i + 1) * (tm // halo), last), 1)),
            const((8, D_MODEL)),
        ]
        head_args = [bz, bz, bz, bz, conv_w]
    return pl.pallas_call(
        kern,
        grid=(t // tm,),
        in_specs=head_specs + [
            const((D_MODEL, D_MODEL)),
            row(D_MODEL),
            mod,
            mod,
            mod,
            const((1, D_MODEL)), const((1, D_MODEL)),
            const((D_MODEL, LANES)), const((D_MODEL, LANES)), const((1, LANES)),
        ],
        out_specs=[row(D_MODEL), pl.BlockSpec((tm * TOKEN_SUBLANES, LANES), lambda i: (i, 0)), row(LANES)],
        out_shape=[
            jax.ShapeDtypeStruct((t, D_MODEL), F32),
            jax.ShapeDtypeStruct((t * TOKEN_SUBLANES, LANES), F32),
            jax.ShapeDtypeStruct((t, LANES), F32),
        ],
        compiler_params=_params("arbitrary"),
        name="outproj_" + kind,
    )(*head_args, w_out, x2, g1, sc2, sh2, lng, lnb, wr_hi, wr_lo, br)


def _route_kernel(lg_ref, tri_ref, idx_ref, gate_ref, cnt_ref, carry_ref):
    @pl.when(pl.program_id(0) == 0)
    def _():
        carry_ref[...] = jnp.zeros_like(carry_ref)

    work = lg_ref[...]
    lane = lax.broadcasted_iota(jnp.int32, work.shape, 1).astype(F32)
    sels, vals, ids = [], [], []
    for _ in range(TOP_K):
        m = jnp.max(work, axis=1, keepdims=True)
        first = jnp.min(jnp.where(work == m, lane, float(LANES)), axis=1, keepdims=True)
        sel = lane == first
        work = jnp.where(sel, -jnp.inf, work)
        sels.append(sel)
        vals.append(m)
        ids.append(first)
    onehot = jnp.where(sels[0] | sels[1] | sels[2] | sels[3], 1.0, 0.0)
    before = jnp.dot(tri_ref[...], onehot.astype(BF16), preferred_element_type=F32) + carry_ref[...]
    carry_ref[...] = carry_ref[...] + jnp.sum(onehot, axis=0, keepdims=True)
    cnt_ref[...] = carry_ref[...]
    exps = [jnp.exp(v - vals[0]) for v in vals]
    denom = exps[0] + exps[1] + exps[2] + exps[3]
    idx_out = jnp.zeros_like(work)
    gate_out = jnp.zeros_like(work)
    for k in range(TOP_K):
        rank = jnp.sum(jnp.where(sels[k], before, 0.0), axis=1, keepdims=True)
        idx_out = jnp.where(lane == k, ids[k], jnp.where(lane == TOP_K + k, rank, idx_out))
        gate_out = jnp.where(lane == k, exps[k] / denom, gate_out)
    idx_ref[...] = idx_out.astype(jnp.int32)
    gate_ref[...] = gate_out


def _route(logits):
    t = logits.shape[0]
    tm = ROUTE_TILE
    tri = jnp.asarray(np.tril(np.ones((tm, tm), np.float32), -1), BF16)
    row = pl.BlockSpec((tm, LANES), lambda i: (i, 0))
    return pl.pallas_call(
        _route_kernel,
        grid=(t // tm,),
        in_specs=[row, pl.BlockSpec((tm, tm), lambda i: (0, 0))],
        out_specs=[row, row, pl.BlockSpec((1, LANES), lambda i: (0, 0))],
        out_shape=[
            jax.ShapeDtypeStruct((t, LANES), jnp.int32),
            jax.ShapeDtypeStruct((t, LANES), F32),
            jax.ShapeDtypeStruct((1, LANES), F32),
        ],
        scratch_shapes=[pltpu.VMEM((1, LANES), F32)],
        compiler_params=_params("arbitrary"),
        name="route",
    )(logits, tri)


def _zero_fill(fill_start_ref, fill_len_ref, n_used_ref, buf_hbm, zero_vmem, sem_fill):
    n_blk = buf_hbm.shape[0] // (EXPERT_ROWS * TOKEN_SUBLANES)
    zero_vmem[...] = jnp.zeros_like(zero_vmem)
    chunks = [1 << b for b in reversed(range(EXPERT_ROWS.bit_length() - 1))]

    def pads(wait):
        def body(e, carry):
            start, length = fill_start_ref[e], fill_len_ref[e]
            for c in chunks:
                @pl.when((length & c) != 0)
                def _():
                    row = pl.multiple_of((start + (length & (-2 * c))) * TOKEN_SUBLANES, TOKEN_SUBLANES)
                    cp = pltpu.make_async_copy(zero_vmem.at[pl.ds(0, c * TOKEN_SUBLANES), :],
                                               buf_hbm.at[pl.ds(row, c * TOKEN_SUBLANES), :], sem_fill)
                    cp.wait() if wait else cp.start()
            return carry

        lax.fori_loop(0, N_EXPERTS, body, 0)

    def tail(wait):
        def body(j, carry):
            row = pl.multiple_of(j * (EXPERT_ROWS * TOKEN_SUBLANES), EXPERT_ROWS * TOKEN_SUBLANES)
            cp = pltpu.make_async_copy(zero_vmem, buf_hbm.at[pl.ds(row, EXPERT_ROWS * TOKEN_SUBLANES), :], sem_fill)
            cp.wait() if wait else cp.start()
            return carry

        lax.fori_loop(n_used_ref[0], n_blk, body, 0)

    pads(False)
    tail(False)
    pads(True)
    tail(True)


def _dispatch_kernel(fill_start_ref, fill_len_ref, n_used_ref, *rest, tiles):
    v_refs = rest[:len(tiles)]
    pos_hbm, buf_hbm, idx_smem, sem_idx, sem_rows, zero_vmem, sem_fill = rest[len(tiles):]
    tm = v_refs[0].shape[0] // TOKEN_SUBLANES
    per_tile = tm * TOP_K
    i = pl.program_id(0)
    fetch = pltpu.make_async_copy(pos_hbm.at[pl.ds(i * per_tile, per_tile)], idx_smem, sem_idx)
    fetch.start()

    @pl.when(i == 0)
    def _():
        _zero_fill(fill_start_ref, fill_len_ref, n_used_ref, buf_hbm, zero_vmem, sem_fill)

    fetch.wait()

    def scatter(v_ref):
        def row_copy(t, p):
            return pltpu.make_async_copy(_token_tile(v_ref, t), _token_tile(buf_hbm, p), sem_rows)

        def issue(j, carry):
            for u in range(DMA_UNROLL):
                t = j * DMA_UNROLL + u
                for k in range(TOP_K):
                    row_copy(t, idx_smem[t * TOP_K + k]).start(priority=k % 2)
            return carry

        def drain(j, carry):
            for _ in range(DMA_UNROLL * TOP_K):
                row_copy(0, 0).wait()
            return carry

        lax.fori_loop(0, tm // DMA_UNROLL, issue, 0)
        lax.fori_loop(0, tm // DMA_UNROLL, drain, 0)

    first = 0
    for v_ref, n_tiles in zip(v_refs, tiles):
        @pl.when((i >= first) & (i < first + n_tiles))
        def _():
            scatter(v_ref)

        first += n_tiles


def _dispatch(fill_start, fill_len, n_used, v_parts, pos_flat, n_rows):
    tm = DISPATCH_TILE
    tiles = tuple(v.shape[0] // (tm * TOKEN_SUBLANES) for v in v_parts)
    firsts = [sum(tiles[:p]) for p in range(len(tiles))]
    any_spec = pl.BlockSpec(memory_space=pl.ANY)

    def part_spec(first, n_tiles):
        return pl.BlockSpec((tm * TOKEN_SUBLANES, LANES), lambda i, *_: (jnp.clip(i - first, 0, n_tiles - 1), 0))

    grid_spec = pltpu.PrefetchScalarGridSpec(
        num_scalar_prefetch=3,
        grid=(sum(tiles),),
        in_specs=[part_spec(f, n) for f, n in zip(firsts, tiles)] + [any_spec],
        out_specs=any_spec,
        scratch_shapes=[pltpu.SMEM((tm * TOP_K,), jnp.int32), pltpu.SemaphoreType.DMA, pltpu.SemaphoreType.DMA,
                        pltpu.VMEM((EXPERT_ROWS * TOKEN_SUBLANES, LANES), F32), pltpu.SemaphoreType.DMA],
    )
    return pl.pallas_call(
        functools.partial(_dispatch_kernel, tiles=tiles),
        grid_spec=grid_spec,
        out_shape=jax.ShapeDtypeStruct((n_rows * TOKEN_SUBLANES, LANES), F32),
        compiler_params=_params("arbitrary"),
        name="dispatch",
    )(fill_start, fill_len, n_used, *v_parts, pos_flat)


def _pair_shuffle():
    l = np.arange(LANES)
    p = np.zeros((LANES, LANES), np.float32)
    p[l, l // 2 + (LANES // 2) * (l % 2)] = 1.0
    return p


def _expert_kernel(blk_e_ref, n_used_ref, run_ref, next_ref, x_ref, bu_ref, bd_ref, perm_ref, wu_hbm, wd_hbm, o_ref,
                   wu_f32, wd_f32, wu_bf, wd_bf, sem_u, sem_d, *, layer):
    i = pl.program_id(0)
    active = i < n_used_ref[0]
    expert = blk_e_ref[i]
    new_expert = (i == 0) | (expert != blk_e_ref[jnp.maximum(i - 1, 0)])

    def fetch(e, slot):
        return (pltpu.make_async_copy(wu_hbm.at[layer, e], wu_f32.at[slot], sem_u.at[slot]),
                pltpu.make_async_copy(wd_hbm.at[layer, e], wd_f32.at[slot], sem_d.at[slot]))

    @pl.when(active & new_expert)
    def _():
        slot = run_ref[expert] % 2

        @pl.when(i == 0)
        def _():
            for cp in fetch(expert, slot):
                cp.start()

        for cp in fetch(expert, slot):
            cp.wait()
        nxt = next_ref[expert]

        @pl.when(nxt >= 0)
        def _():
            for cp in fetch(nxt, 1 - slot):
                cp.start()

        def cast(c, carry):
            r = pl.multiple_of(c * LANES, LANES)
            wu_bf[pl.ds(r, LANES), :] = wu_f32[slot, pl.ds(r, LANES), :].astype(BF16)
            wd_bf[pl.ds(r, LANES), :] = jnp.dot(perm_ref[...], wd_f32[slot, pl.ds(r, LANES), :].astype(BF16),
                                               preferred_element_type=F32).astype(BF16)
            return carry

        lax.fori_loop(0, D_MODEL // LANES, cast, 0)

    @pl.when(active)
    def _():
        h = jnp.dot(_load_token_tiles(x_ref, EXPERT_ROWS).astype(BF16), wu_bf[...], preferred_element_type=F32) + bu_ref[0]
        even = lax.broadcasted_iota(jnp.int32, (EXPERT_ROWS, LANES), 1) % 2 == 0
        acts = []
        for k in range(D_EXPERT // LANES):
            ha = h[:, 2 * k * LANES:(2 * k + 1) * LANES]
            hb = h[:, (2 * k + 1) * LANES:(2 * k + 2) * LANES]
            glu = jnp.where(even, ha, pltpu.roll(hb, 1, axis=1))
            lin = jnp.where(even, pltpu.roll(ha, LANES - 1, axis=1), hb)
            glu = jnp.minimum(glu, SWIGLU_LIMIT)
            lin = jnp.clip(lin, -SWIGLU_LIMIT, SWIGLU_LIMIT)
            acts.append((glu * jax.nn.sigmoid(SWIGLU_ALPHA * glu) * (lin + 1.0)).astype(BF16))
        a = jnp.concatenate(acts, axis=1)
        _store_token_tiles(o_ref, jnp.dot(a, wd_bf[...], preferred_element_type=F32) + bd_ref[0])

    @pl.when(jnp.logical_not(active))
    def _():
        o_ref[...] = jnp.zeros_like(o_ref)


def _experts(blk_e, n_used, run_idx, next_expert, xbuf, layer, wu, bu, wd, bd):
    n_blk = xbuf.shape[0] // (EXPERT_ROWS * TOKEN_SUBLANES)
    grid_spec = pltpu.PrefetchScalarGridSpec(
        num_scalar_prefetch=4,
        grid=(n_blk,),
        in_specs=[
            pl.BlockSpec((EXPERT_ROWS * TOKEN_SUBLANES, LANES), lambda i, e, *_: (i, 0)),
            pl.BlockSpec((1, 1, 2 * D_EXPERT), lambda i, e, *_: (e[i], 0, 0)),
            pl.BlockSpec((1, 1, D_MODEL), lambda i, e, *_: (e[i], 0, 0)),
            pl.BlockSpec((LANES, LANES), lambda i, e, *_: (0, 0)),
            pl.BlockSpec(memory_space=pl.ANY),
            pl.BlockSpec(memory_space=pl.ANY),
        ],
        out_specs=pl.BlockSpec((EXPERT_ROWS * TOKEN_SUBLANES, LANES), lambda i, e, *_: (i, 0)),
        scratch_shapes=[
            pltpu.VMEM((2, D_MODEL, 2 * D_EXPERT), F32),
            pltpu.VMEM((2, D_EXPERT, D_MODEL), F32),
            pltpu.VMEM((D_MODEL, 2 * D_EXPERT), BF16),
            pltpu.VMEM((D_EXPERT, D_MODEL), BF16),
            pltpu.SemaphoreType.DMA((2,)),
            pltpu.SemaphoreType.DMA((2,)),
        ],
    )
    return pl.pallas_call(
        functools.partial(_expert_kernel, layer=layer),
        grid_spec=grid_spec,
        out_shape=jax.ShapeDtypeStruct(xbuf.shape, F32),
        compiler_params=pltpu.CompilerParams(dimension_semantics=("arbitrary",), vmem_limit_bytes=EXPERT_VMEM_LIMIT),
        name="experts",
    )(blk_e, n_used, run_idx, next_expert, xbuf, bu, bd, jnp.asarray(_pair_shuffle(), BF16), wu, wd)


def _combine_kernel(x_ref, gate_ref, g2_ref, lng_ref, lnb_ref, pos_hbm, y_hbm, o_ref, idx_smem, rows, sem_idx, sem_rows):
    tm = x_ref.shape[0]
    per_tile = tm * TOP_K
    i = pl.program_id(0)

    def row_copy(slot, t, k, p):
        return pltpu.make_async_copy(_token_tile(y_hbm, p), _token_tile(rows, (slot * TOP_K + k) * tm + t), sem_rows.at[slot])

    def start_gather(tile, slot):
        fetch = pltpu.make_async_copy(pos_hbm.at[pl.ds(tile * per_tile, per_tile)], idx_smem, sem_idx)
        fetch.start()
        fetch.wait()

        def issue(j, carry):
            for u in range(DMA_UNROLL):
                t = j * DMA_UNROLL + u
                for k in range(TOP_K):
                    row_copy(slot, t, k, idx_smem[t * TOP_K + k]).start(priority=k % 2)
            return carry

        lax.fori_loop(0, tm // DMA_UNROLL, issue, 0)

    @pl.when(i == 0)
    def _():
        start_gather(0, 0)

    @pl.when(i + 1 < pl.num_programs(0))
    def _():
        start_gather(i + 1, (i + 1) % 2)

    slot = i % 2

    def drain(j, carry):
        for _ in range(DMA_UNROLL * TOP_K):
            row_copy(slot, 0, 0, 0).wait()
        return carry

    lax.fori_loop(0, tm // DMA_UNROLL, drain, 0)
    chunks = []
    for c in range(TOKEN_SUBLANES):
        def part(k):
            first = pl.multiple_of((slot * TOP_K + k) * (tm * TOKEN_SUBLANES), tm * TOKEN_SUBLANES) + c
            return gate_ref[:, k:k + 1] * rows[pl.ds(first, tm, stride=TOKEN_SUBLANES), :]

        fc = part(0)
        for k in range(1, TOP_K):
            fc = fc + part(k)
        chunks.append(fc)
    f = jnp.concatenate(chunks, axis=1)
    o_ref[...] = _layer_norm(DEEPNORM_ALPHA * x_ref[...] + g2_ref[0] * f, lng_ref[...], lnb_ref[...])


def _combine(x2, gates, pos_flat, ybuf, g2, lng, lnb, seq_rows, mod_row):
    t = x2.shape[0]
    tm = COMBINE_TILE
    row = pl.BlockSpec((tm, D_MODEL), lambda i: (i, 0))
    const = pl.BlockSpec((1, D_MODEL), lambda i: (0, 0))
    return pl.pallas_call(
        _combine_kernel,
        grid=(t // tm,),
        in_specs=[row, pl.BlockSpec((tm, LANES), lambda i: (i, 0)), _mod_spec(tm, seq_rows, mod_row), const, const,
                  pl.BlockSpec(memory_space=pl.ANY), pl.BlockSpec(memory_space=pl.ANY)],
        out_specs=row,
        out_shape=jax.ShapeDtypeStruct((t, D_MODEL), F32),
        scratch_shapes=[
            pltpu.SMEM((tm * TOP_K,), jnp.int32),
            pltpu.VMEM((2 * TOP_K * tm * TOKEN_SUBLANES, LANES), F32),
            pltpu.SemaphoreType.DMA,
            pltpu.SemaphoreType.DMA((2,)),
        ],
        compiler_params=_params("arbitrary"),
        name="combine",
    )(x2, gates, g2, lng, lnb, pos_flat, ybuf)


def _moe(v_parts, logits, layer, wu, bu, wd, bd):
    t = logits.shape[0]
    n = t * TOP_K
    idx, gates, cnt = _route(logits)
    counts = cnt[0, :N_EXPERTS].astype(jnp.int32)
    padded = (counts + EXPERT_ROWS - 1) // EXPERT_ROWS * EXPERT_ROWS
    pad_ends = jnp.cumsum(padded)
    pad_starts = pad_ends - padded
    experts = jnp.arange(N_EXPERTS, dtype=jnp.int32)
    start = jnp.sum(jnp.where(idx[:, :TOP_K, None] == experts, pad_starts, 0), axis=-1)
    pos_flat = (start + idx[:, TOP_K:2 * TOP_K]).reshape(n)
    n_blk = n // EXPERT_ROWS + N_EXPERTS
    blk_start = jnp.arange(n_blk, dtype=jnp.int32) * EXPERT_ROWS
    blk_e = jnp.minimum(jnp.sum((pad_ends[None, :] <= blk_start[:, None]).astype(jnp.int32), axis=1), N_EXPERTS - 1)
    n_used = pad_ends[-1:] // EXPERT_ROWS
    nonempty = counts > 0
    run_idx = jnp.cumsum(nonempty.astype(jnp.int32)) - 1
    later = nonempty[None, :] & (experts[None, :] > experts[:, None])
    next_expert = jnp.min(jnp.where(later, experts[None, :], N_EXPERTS), axis=1)
    next_expert = jnp.where(next_expert == N_EXPERTS, -1, next_expert)
    fill_start, fill_len = pad_starts + counts, padded - counts
    xbuf = _dispatch(fill_start, fill_len, n_used, v_parts, pos_flat, n_blk * EXPERT_ROWS)
    ybuf = _experts(blk_e, n_used, run_idx, next_expert, xbuf, layer, wu, bu, wd, bd)
    return ybuf, pos_flat, gates


def _rope_tables(seq):
    rows = seq // GRID_W
    row = jnp.repeat(jnp.arange(rows), GRID_W).astype(F32)
    col = jnp.tile(jnp.arange(GRID_W), rows).astype(F32)
    n_freq = HEAD_DIM // 4
    inv = ROPE_BASE ** (-jnp.arange(n_freq, dtype=F32) / n_freq)
    ang = jnp.concatenate([row[:, None] * inv, col[:, None] * inv], -1)
    ang = jnp.concatenate([ang, ang], -1)
    sign = jnp.where(jnp.arange(HEAD_DIM) < HEAD_DIM // 2, -1.0, 1.0).astype(F32)
    cos = jnp.cos(ang)
    sin = jnp.sin(ang) * sign
    return jnp.tile(cos, (1, LANES // HEAD_DIM)), jnp.tile(sin, (1, LANES // HEAD_DIM))


def kernel(x, c, ctx, c_ctx, w_mod, b_mod, w_in_even, sink, w_out_even, w_in_odd, conv_w, w_out_odd, ln_g, ln_b,
           w_router, b_router, w_up, b_up, w_down, b_down):
    B, S, D = x.shape
    L = ctx.shape[1]
    T, TC = B * S, B * L
    cos, sin = _rope_tables(S)
    cond = jnp.concatenate([c, c_ctx[None, :], jnp.zeros((MOD_ROWS - B - 1, D), F32)], 0)
    mod = _modulation(cond, w_mod, b_mod)

    bu = b_up.reshape(DEPTH, N_EXPERTS, 1, 2 * D_EXPERT)
    bd = b_down.reshape(DEPTH, N_EXPERTS, 1, D)
    wr_hi = w_router.astype(BF16)
    wr_lo = (w_router - wr_hi.astype(F32)).astype(BF16)
    pad_r = ((0, 0), (0, 0), (0, LANES - N_EXPERTS))
    wr_hi, wr_lo = jnp.pad(wr_hi, pad_r), jnp.pad(wr_lo, pad_r)
    br = jnp.pad(b_router, ((0, 0), (0, LANES - N_EXPERTS)), constant_values=NEG_INF).reshape(DEPTH, 1, LANES)

    x2 = x.reshape(T, D)
    xc2 = ctx.reshape(TC, D)
    for l in range(DEPTH):
        even = l % 2 == 0
        j = l // 2
        ctx_after = any(m % 2 == 0 for m in range(l + 1, DEPTH))
        sh1, sc1, g1, sh2, sc2, g2 = [m.reshape(MOD_ROWS, 1, D) for m in jnp.split(mod[l], 6, axis=-1)]
        lng1, lnb1 = ln_g[l, 0].reshape(1, D), ln_b[l, 0].reshape(1, D)
        lng2, lnb2 = ln_g[l, 1].reshape(1, D), ln_b[l, 1].reshape(1, D)
        post = (g1, sc2, sh2, lng1, lnb1, wr_hi[l], wr_lo[l], br[l])
        if even:
            w_in = w_in_even[j].astype(BF16)
            w_out = w_out_even[j].astype(BF16)
            h = _inproj_even(x2, sc1, sh1, w_in, cos, sin, S, None, True)
            hc = _inproj_even(xc2, sc1, sh1, w_in, cos, sin, L, B, False)
            kcol = ATTN_WIDTH // KV_WIDTH
            attn = _win_attention(h, hc, sink[j], B, S, kcol, kcol + 1)
            four = _fourier_mix(h, B, S)
            x2, v, logits = _outproj("even", (attn, four), w_out, x2, *post, S, None)
            if ctx_after:
                attn_c = _ctx_attention(hc, sink[j], B)
                four_c = _fourier_mix(hc, B, L)
                xc2, vc, logits_c = _outproj("even", (attn_c, four_c), w_out, xc2, *post, L, B)
        else:
            w_in = w_in_odd[j].astype(BF16)
            w_out = w_out_odd[j].astype(BF16)
            cw = jnp.pad(conv_w[j], ((0, 8 - conv_w.shape[1]), (0, 0)))
            bz = _inproj_odd(x2, sc1, sh1, w_in, S, None)
            x2, v, logits = _outproj("odd", (bz,), w_out, x2, *post, S, None, conv_w=cw)
            if ctx_after:
                bzc = _inproj_odd(xc2, sc1, sh1, w_in, L, B)
                xc2, vc, logits_c = _outproj("odd", (bzc,), w_out, xc2, *post, L, B, conv_w=cw)
        if ctx_after:
            ybuf, pos, gates = _moe((v, vc), jnp.concatenate([logits, logits_c], 0), l, w_up, bu[l], w_down, bd[l])
            xc2 = _combine(xc2, gates[T:], pos[T * TOP_K:], ybuf, g2, lng2, lnb2, L, B)
        else:
            ybuf, pos, gates = _moe((v,), logits, l, w_up, bu[l], w_down, bd[l])
        x2 = _combine(x2, gates[:T], pos[:T * TOP_K], ybuf, g2, lng2, lnb2, S, None)
    return x2.reshape(B, S, D)
```

```python
import functools

import numpy as np
import jax
import jax.numpy as jnp
from jax import lax
from jax.experimental import pallas as pl
from jax.experimental.pallas import tpu as pltpu

D_MODEL = 1024
DEPTH = 4
GRID_W = 64
HEAD_DIM = 64
N_Q_HEADS = 12
N_KV_HEADS = 4
Q_PER_KV = N_Q_HEADS // N_KV_HEADS
BLOCK = 128
ROPE_BASE = 10000.0
FOURIER_GROUP_DIM = 64
FOURIER_WIDTH = 256
ATTN_WIDTH = N_Q_HEADS * HEAD_DIM
KV_WIDTH = N_KV_HEADS * HEAD_DIM
EVEN_IN_WIDTH = ATTN_WIDTH + 2 * KV_WIDTH + FOURIER_WIDTH
N_EXPERTS = 32
TOP_K = 4
D_EXPERT = D_MODEL
SWIGLU_LIMIT = 7.0
SWIGLU_ALPHA = 1.702
LN_EPS = 1e-5
NEG_INF = -1e30
DEEPNORM_ALPHA = (2 * DEPTH) ** 0.25

LANES = 128
SUBLANES = 8
TOKEN_SUBLANES = D_MODEL // LANES
ROW_TILE = 1024
EXPERT_ROWS = 512
MOD_ROWS = 16
ROUTE_TILE = 512
DISPATCH_TILE = 1024
COMBINE_TILE = 512
DMA_UNROLL = 8
VMEM_LIMIT = 48 * 1024 * 1024
EXPERT_VMEM_LIMIT = 56 * 1024 * 1024

F32 = jnp.float32
BF16 = jnp.bfloat16


def _params(*sem):
    return pltpu.CompilerParams(dimension_semantics=sem, vmem_limit_bytes=VMEM_LIMIT)


def _mod_spec(tm, seq_rows, mod_row):
    if mod_row is None:
        return pl.BlockSpec((1, 1, D_MODEL), lambda i: ((i * tm) // seq_rows, 0, 0))
    return pl.BlockSpec((1, 1, D_MODEL), lambda i: (mod_row, 0, 0))


def _mod_kernel(c_ref, w_ref, b_ref, o_ref):
    c = c_ref[...]
    s = (c * jax.nn.sigmoid(c)).astype(BF16)
    o_ref[0] = jnp.dot(s, w_ref[0].astype(BF16), preferred_element_type=F32) + b_ref[0]


def _modulation(cond, w_mod, b_mod):
    tn = 1536
    n = w_mod.shape[-1]
    return pl.pallas_call(
        _mod_kernel,
        grid=(DEPTH, n // tn),
        in_specs=[
            pl.BlockSpec((MOD_ROWS, D_MODEL), lambda l, j: (0, 0)),
            pl.BlockSpec((1, D_MODEL, tn), lambda l, j: (l, 0, j)),
            pl.BlockSpec((1, 1, tn), lambda l, j: (l, 0, j)),
        ],
        out_specs=pl.BlockSpec((1, MOD_ROWS, tn), lambda l, j: (l, 0, j)),
        out_shape=jax.ShapeDtypeStruct((DEPTH, MOD_ROWS, n), F32),
        compiler_params=_params("arbitrary", "arbitrary"),
        name="modulation",
    )(cond, w_mod, b_mod.reshape(DEPTH, 1, n))


def _rope_chunk(h, cos, sin_signed, first_half):
    swapped = jnp.where(first_half, pltpu.roll(h, LANES - HEAD_DIM // 2, axis=1), pltpu.roll(h, HEAD_DIM // 2, axis=1))
    return h * cos + swapped * sin_signed


def _inproj_even_kernel(x_ref, sc_ref, sh_ref, w_ref, cos_ref, sin_ref, o_ref, *, rope):
    u = (x_ref[...] * (1.0 + sc_ref[0]) + sh_ref[0]).astype(BF16)
    tn = 512
    if rope:
        cos = cos_ref[...]
        sin = sin_ref[...]
        lane = lax.broadcasted_iota(jnp.int32, cos.shape, 1)
        first_half = (lane % HEAD_DIM) < HEAD_DIM // 2
    scale = HEAD_DIM ** -0.5
    for jt in range(EVEN_IN_WIDTH // tn):
        acc = jnp.dot(u, w_ref[:, jt * tn:(jt + 1) * tn], preferred_element_type=F32)
        for k in range(tn // LANES):
            col = jt * tn + k * LANES
            h = acc[:, k * LANES:(k + 1) * LANES]
            if rope and col < ATTN_WIDTH + KV_WIDTH:
                h = _rope_chunk(h, cos, sin, first_half)
            if col < ATTN_WIDTH:
                h = h * scale
            o_ref[:, col:col + LANES] = h.astype(BF16)


def _inproj_even(x2, sc, sh, w, cos, sin, seq_rows, mod_row, rope):
    t = x2.shape[0]
    tm = min(ROW_TILE, seq_rows)
    pos_tiles = cos.shape[0] // tm
    mod = _mod_spec(tm, seq_rows, mod_row)

    return pl.pallas_call(
        functools.partial(_inproj_even_kernel, rope=rope),
        grid=(t // tm,),
        in_specs=[
            pl.BlockSpec((tm, D_MODEL), lambda i: (i, 0)),
            mod,
            mod,
            pl.BlockSpec((D_MODEL, EVEN_IN_WIDTH), lambda i: (0, 0)),
            pl.BlockSpec((tm, LANES), lambda i: (i % pos_tiles, 0)),
            pl.BlockSpec((tm, LANES), lambda i: (i % pos_tiles, 0)),
        ],
        out_specs=pl.BlockSpec((tm, EVEN_IN_WIDTH), lambda i: (i, 0)),
        out_shape=jax.ShapeDtypeStruct((t, EVEN_IN_WIDTH), BF16),
        compiler_params=_params("arbitrary"),
        name="inproj_even",
    )(x2, sc, sh, w, cos, sin)


def _attend(q_ref, k_all, v_all, valid, sink_ref, o_ref):
    rows = q_ref.shape[0]
    lane = lax.broadcasted_iota(jnp.int32, (rows, LANES), 1)
    low = lane < HEAD_DIM
    scores, sks = [], []
    for kv in range(N_KV_HEADS):
        kv_chunk, kv_half = divmod(kv, 2)
        keep = low if kv_half == 0 else jnp.logical_not(low)
        stack, sinks = [], []
        for g in range(Q_PER_KV):
            head = kv * Q_PER_KV + g
            chunk, half = divmod(head, 2)
            qh = q_ref[:, chunk * LANES:(chunk + 1) * LANES].astype(F32)
            if half != kv_half:
                qh = pltpu.roll(qh, HEAD_DIM, axis=1)
            stack.append(jnp.where(keep, qh, 0.0).astype(BF16))
            sinks.append(jnp.full((rows, 1), sink_ref[head], F32))
        qs = jnp.concatenate(stack, axis=0)
        sks.append(jnp.concatenate(sinks, axis=0))
        kc = k_all[:, kv_chunk * LANES:(kv_chunk + 1) * LANES]
        s = lax.dot_general(qs, kc, (((1,), (1,)), ((), ())), preferred_element_type=F32)
        if valid is not None:
            s = jnp.concatenate([s[:, j * BLOCK:(j + 1) * BLOCK] if ok is None
                                 else jnp.where(ok, s[:, j * BLOCK:(j + 1) * BLOCK], NEG_INF)
                                 for j, ok in enumerate(valid)], axis=1)
        scores.append(s)
    probs, sink_terms = [], []
    for kv in range(N_KV_HEADS):
        m = jnp.maximum(sks[kv], jnp.max(scores[kv], axis=-1, keepdims=True))
        probs.append(jnp.exp(scores[kv] - m).astype(BF16))
        sink_terms.append(jnp.exp(sks[kv] - m))
    key_lane = lax.broadcasted_iota(jnp.int32, (v_all.shape[0], LANES), 1)
    for kv in range(N_KV_HEADS):
        kv_chunk, kv_half = divmod(kv, 2)
        vc = v_all[:, kv_chunk * LANES:(kv_chunk + 1) * LANES]
        own = (key_lane < HEAD_DIM) if kv_half == 0 else (key_lane >= HEAD_DIM)
        o = jnp.dot(probs[kv], jnp.where(own, vc, jnp.ones_like(vc)), preferred_element_type=F32)
        other = (1 - kv_half) * HEAD_DIM
        o = o / (o[:, other:other + 1] + sink_terms[kv])
        for g in range(Q_PER_KV):
            head = kv * Q_PER_KV + g
            half = head % 2
            oh = o[g * rows:(g + 1) * rows]
            if half != kv_half:
                oh = pltpu.roll(oh, HEAD_DIM, axis=1)
            o_ref[:, head * HEAD_DIM:(head + 1) * HEAD_DIM] = oh[:, half * HEAD_DIM:(half + 1) * HEAD_DIM].astype(BF16)


def _win_attn_kernel(sink_ref, q_ref, kp_ref, kc_ref, kn_ref, vp_ref, vc_ref, vn_ref, kx_ref, vx_ref, o_ref, *, nb):
    n = pl.program_id(1)
    k_all = jnp.concatenate([kp_ref[...], kc_ref[...], kn_ref[...], kx_ref[...]], axis=0)
    v_all = jnp.concatenate([vp_ref[...], vc_ref[...], vn_ref[...], vx_ref[...]], axis=0)
    r = lax.broadcasted_iota(jnp.int32, (Q_PER_KV * BLOCK, BLOCK), 0) % BLOCK
    c = lax.broadcasted_iota(jnp.int32, (Q_PER_KV * BLOCK, BLOCK), 1)
    in_prev = c >= jnp.where(n > 0, r, BLOCK)
    in_next = c <= jnp.where(n < nb - 1, r, -1)
    valid = [in_prev, None, in_next] + [None] * (kx_ref.shape[0] // BLOCK)
    _attend(q_ref, k_all, v_all, valid, sink_ref, o_ref)


def _win_attention(h, hc, sink, batch, seq, kc_col, vc_col):
    nb = seq // BLOCK
    qcol, kcol, vcol = 0, ATTN_WIDTH // KV_WIDTH, ATTN_WIDTH // KV_WIDTH + 1

    def blk(shift, col):
        return pl.BlockSpec((BLOCK, KV_WIDTH), lambda b, n: (b * nb + jnp.clip(n + shift, 0, nb - 1), col))

    return pl.pallas_call(
        functools.partial(_win_attn_kernel, nb=nb),
        grid=(batch, nb),
        in_specs=[
            pl.BlockSpec(memory_space=pltpu.SMEM),
            pl.BlockSpec((BLOCK, ATTN_WIDTH), lambda b, n: (b * nb + n, qcol)),
            blk(-1, kcol), blk(0, kcol), blk(1, kcol),
            blk(-1, vcol), blk(0, vcol), blk(1, vcol),
            pl.BlockSpec((hc.shape[0] // batch, KV_WIDTH), lambda b, n: (b, kc_col)),
            pl.BlockSpec((hc.shape[0] // batch, KV_WIDTH), lambda b, n: (b, vc_col)),
        ],
        out_specs=pl.BlockSpec((BLOCK, ATTN_WIDTH), lambda b, n: (b * nb + n, 0)),
        out_shape=jax.ShapeDtypeStruct((batch * seq, ATTN_WIDTH), BF16),
        compiler_params=_params("arbitrary", "arbitrary"),
        name="window_attention",
    )(sink, h, h, h, h, h, h, h, hc, hc)


def _ctx_attn_kernel(sink_ref, q_ref, k_ref, v_ref, o_ref):
    _attend(q_ref, k_ref[...], v_ref[...], None, sink_ref, o_ref)


def _ctx_attention(hc, sink, batch):
    ctx_len = hc.shape[0] // batch
    kcol, vcol = ATTN_WIDTH // KV_WIDTH, ATTN_WIDTH // KV_WIDTH + 1
    return pl.pallas_call(
        _ctx_attn_kernel,
        grid=(batch,),
        in_specs=[
            pl.BlockSpec(memory_space=pltpu.SMEM),
            pl.BlockSpec((ctx_len, ATTN_WIDTH), lambda b: (b, 0)),
            pl.BlockSpec((ctx_len, KV_WIDTH), lambda b: (b, kcol)),
            pl.BlockSpec((ctx_len, KV_WIDTH), lambda b: (b, vcol)),
        ],
        out_specs=pl.BlockSpec((ctx_len, ATTN_WIDTH), lambda b: (b, 0)),
        out_shape=jax.ShapeDtypeStruct((hc.shape[0], ATTN_WIDTH), BF16),
        compiler_params=_params("arbitrary"),
        name="context_attention",
    )(sink, hc, hc, hc)


def _dft_mats(n):
    k = np.arange(n, dtype=np.int64)
    ang = 2.0 * np.pi * ((k[:, None] * k[None, :]) % n).astype(np.float64) / n
    return np.cos(ang), np.sin(ang)


def _group_dft():
    c, s = _dft_mats(FOURIER_GROUP_DIM)
    groups = FOURIER_WIDTH // FOURIER_GROUP_DIM
    eye = np.eye(groups)
    return np.concatenate([np.kron(eye, c), np.kron(eye, s)], axis=1)


def _fourier_kernel(f_ref, w1_ref, c_ref, s_ref, o_ref, ab_ref, *, scale):
    b = pl.program_id(1)

    @pl.when(pl.program_id(0) == 0)
    def _():
        ab_ref[b] = jnp.dot(f_ref[...], w1_ref[...], preferred_element_type=F32).astype(BF16)

    y = (jnp.dot(c_ref[...], ab_ref[b, :, :FOURIER_WIDTH], preferred_element_type=F32)
         - jnp.dot(s_ref[...], ab_ref[b, :, FOURIER_WIDTH:], preferred_element_type=F32))
    o_ref[...] = (y * scale).astype(BF16)


def _fourier_mix(h, batch, n):
    tr = 256
    cn, sn = _dft_mats(n)
    fcol = (EVEN_IN_WIDTH - FOURIER_WIDTH) // FOURIER_WIDTH
    return pl.pallas_call(
        functools.partial(_fourier_kernel, scale=float((n * FOURIER_GROUP_DIM) ** -0.5)),
        grid=(n // tr, batch),
        in_specs=[
            pl.BlockSpec((n, FOURIER_WIDTH), lambda i, b: (jnp.where(i == 0, b, batch - 1), fcol)),
            pl.BlockSpec((FOURIER_WIDTH, 2 * FOURIER_WIDTH), lambda i, b: (0, 0)),
            pl.BlockSpec((tr, n), lambda i, b: (i, 0)),
            pl.BlockSpec((tr, n), lambda i, b: (i, 0)),
        ],
        out_specs=pl.BlockSpec((tr, FOURIER_WIDTH), lambda i, b: (b * (n // tr) + i, 0)),
        out_shape=jax.ShapeDtypeStruct((batch * n, FOURIER_WIDTH), BF16),
        scratch_shapes=[pltpu.VMEM((batch, n, 2 * FOURIER_WIDTH), BF16)],
        compiler_params=_params("arbitrary", "arbitrary"),
        name="fourier_mix",
    )(h, jnp.asarray(_group_dft(), BF16), jnp.asarray(cn, BF16), jnp.asarray(sn, BF16))


def _inproj_odd_kernel(x_ref, sc_ref, sh_ref, w_ref, o_ref):
    u = (x_ref[...] * (1.0 + sc_ref[0]) + sh_ref[0]).astype(BF16)
    tn = 512
    for jt in range(D_MODEL // tn):
        cols = slice(jt * tn, (jt + 1) * tn)
        b_gate = jnp.dot(u, w_ref[:, jt * tn:(jt + 1) * tn], preferred_element_type=F32)
        o_ref[:, cols] = b_gate.astype(BF16)
        c_gate = jnp.dot(u, w_ref[:, D_MODEL + jt * tn:D_MODEL + (jt + 1) * tn], preferred_element_type=F32)
        hh = jnp.dot(u, w_ref[:, 2 * D_MODEL + jt * tn:2 * D_MODEL + (jt + 1) * tn], preferred_element_type=F32)
        o_ref[:, D_MODEL + jt * tn:D_MODEL + (jt + 1) * tn] = (c_gate * hh).astype(BF16)


def _inproj_odd(x2, sc, sh, w, seq_rows, mod_row):
    t = x2.shape[0]
    tm = min(ROW_TILE, seq_rows)
    mod = _mod_spec(tm, seq_rows, mod_row)

    return pl.pallas_call(
        _inproj_odd_kernel,
        grid=(t // tm,),
        in_specs=[
            pl.BlockSpec((tm, D_MODEL), lambda i: (i, 0)),
            mod,
            mod,
            pl.BlockSpec((D_MODEL, 3 * D_MODEL), lambda i: (0, 0)),
        ],
        out_specs=pl.BlockSpec((tm, 2 * D_MODEL), lambda i: (i, 0)),
        out_shape=jax.ShapeDtypeStruct((t, 2 * D_MODEL), BF16),
        compiler_params=_params("arbitrary"),
        name="inproj_odd",
    )(x2, sc, sh, w)


def _layer_norm(r, g, b):
    mu = jnp.mean(r, axis=-1, keepdims=True)
    d = r - mu
    var = jnp.mean(d * d, axis=-1, keepdims=True)
    return d * lax.rsqrt(var + LN_EPS) * g + b


def _store_token_tiles(ref, rows):
    m = rows.shape[0]
    for c in range(TOKEN_SUBLANES):
        ref[pl.ds(c, m, stride=TOKEN_SUBLANES), :] = rows[:, c * LANES:(c + 1) * LANES]


def _load_token_tiles(ref, m, first_row=0):
    return jnp.concatenate([ref[pl.ds(first_row + c, m, stride=TOKEN_SUBLANES), :] for c in range(TOKEN_SUBLANES)], axis=1)


def _token_tile(ref, t):
    row = t * TOKEN_SUBLANES
    return ref.at[pl.ds(row if isinstance(row, int) else pl.multiple_of(row, TOKEN_SUBLANES), TOKEN_SUBLANES), :]


def _post_mixer(y, x_ref, g1_ref, sc2_ref, sh2_ref, lng_ref, lnb_ref, wrh_ref, wrl_ref, br_ref, xo_ref, v_ref, lg_ref):
    xn = _layer_norm(DEEPNORM_ALPHA * x_ref[...] + g1_ref[0] * y, lng_ref[...], lnb_ref[...])
    xo_ref[...] = xn
    v = xn * (1.0 + sc2_ref[0]) + sh2_ref[0]
    _store_token_tiles(v_ref, v)
    v_hi = v.astype(BF16)
    v_lo = (v - v_hi.astype(F32)).astype(BF16)
    lg_ref[...] = (jnp.dot(v_hi, wrh_ref[...], preferred_element_type=F32)
                   + (jnp.dot(v_lo, wrh_ref[...], preferred_element_type=F32)
                      + jnp.dot(v_hi, wrl_ref[...], preferred_element_type=F32))) + br_ref[...]


def _outproj_even_kernel(a_ref, f_ref, w_ref, *rest):
    y = (jnp.dot(a_ref[...], w_ref[:ATTN_WIDTH, :], preferred_element_type=F32)
         + jnp.dot(f_ref[...], w_ref[ATTN_WIDTH:, :], preferred_element_type=F32))
    _post_mixer(y, *rest)


def _outproj_odd_kernel(b_ref, z_ref, zp_ref, zn_ref, cw_ref, w_ref, *rest, rows_per_seq):
    tm = z_ref.shape[0]
    halo = zp_ref.shape[0]
    row0 = pl.program_id(0) * tm
    z = z_ref[...].astype(F32)
    prev_row = jnp.where(row0 % rows_per_seq == 0, 0.0, zp_ref[halo - 1:halo, :].astype(F32))
    next_row = jnp.where((row0 + tm) % rows_per_seq == 0, 0.0, zn_ref[0:1, :].astype(F32))
    r = lax.broadcasted_iota(jnp.int32, z.shape, 0)
    z_prev = jnp.where(r == 0, prev_row, pltpu.roll(z, 1, axis=0))
    z_next = jnp.where(r == tm - 1, next_row, pltpu.roll(z, tm - 1, axis=0))
    conv = cw_ref[0:1, :] * z_prev + cw_ref[1:2, :] * z + cw_ref[2:3, :] * z_next
    a = (b_ref[...].astype(F32) * conv).astype(BF16)
    y = jnp.dot(a, w_ref[...], preferred_element_type=F32)
    _post_mixer(y, *rest)


def _outproj(kind, mix_inputs, w_out, x2, g1, sc2, sh2, lng, lnb, wr_hi, wr_lo, br, seq_rows, mod_row, conv_w=None):
    t = x2.shape[0]
    tm = min(ROW_TILE, seq_rows)
    mod = _mod_spec(tm, seq_rows, mod_row)

    row = lambda width: pl.BlockSpec((tm, width), lambda i: (i, 0))
    const = lambda shape: pl.BlockSpec(shape, lambda i: (0,) * len(shape))
    if kind == "even":
        attn, four = mix_inputs
        kern = _outproj_even_kernel
        head_specs = [row(ATTN_WIDTH), row(FOURIER_WIDTH)]
        head_args = [attn, four]
    else:
        (bz,) = mix_inputs
        halo = 16
        last = t // halo - 1
        kern = functools.partial(_outproj_odd_kernel, rows_per_seq=seq_rows)
        head_specs = [
            pl.BlockSpec((tm, D_MODEL), lambda i: (i, 0)),
            pl.BlockSpec((tm, D_MODEL), lambda i: (i, 1)),
            pl.BlockSpec((halo, D_MODEL), lambda i: (jnp.maximum(i * (tm // halo) - 1, 0), 1)),
            pl.BlockSpec((halo, D_MODEL), lambda i: (jnp.minimum((i + 1) * (tm // halo), last), 1)),
            const((8, D_MODEL)),
        ]
        head_args = [bz, bz, bz, bz, conv_w]
    return pl.pallas_call(
        kern,
        grid=(t // tm,),
        in_specs=head_specs + [
            const((D_MODEL, D_MODEL)),
            row(D_MODEL),
            mod,
            mod,
            mod,
            const((1, D_MODEL)), const((1, D_MODEL)),
            const((D_MODEL, LANES)), const((D_MODEL, LANES)), const((1, LANES)),
        ],
        out_specs=[row(D_MODEL), pl.BlockSpec((tm * TOKEN_SUBLANES, LANES), lambda i: (i, 0)), row(LANES)],
        out_shape=[
            jax.ShapeDtypeStruct((t, D_MODEL), F32),
            jax.ShapeDtypeStruct((t * TOKEN_SUBLANES, LANES), F32),
            jax.ShapeDtypeStruct((t, LANES), F32),
        ],
        compiler_params=_params("arbitrary"),
        name="outproj_" + kind,
    )(*head_args, w_out, x2, g1, sc2, sh2, lng, lnb, wr_hi, wr_lo, br)


def _route_kernel(lg_ref, tri_ref, idx_ref, gate_ref, cnt_ref, carry_ref):
    @pl.when(pl.program_id(0) == 0)
    def _():
        carry_ref[...] = jnp.zeros_like(carry_ref)

    work = lg_ref[...]
    lane = lax.broadcasted_iota(jnp.int32, work.shape, 1).astype(F32)
    sels, vals, ids = [], [], []
    for _ in range(TOP_K):
        m = jnp.max(work, axis=1, keepdims=True)
        first = jnp.min(jnp.where(work == m, lane, float(LANES)), axis=1, keepdims=True)
        sel = lane == first
        work = jnp.where(sel, -jnp.inf, work)
        sels.append(sel)
        vals.append(m)
        ids.append(first)
    onehot = jnp.where(sels[0] | sels[1] | sels[2] | sels[3], 1.0, 0.0)
    before = jnp.dot(tri_ref[...], onehot.astype(BF16), preferred_element_type=F32) + carry_ref[...]
    carry_ref[...] = carry_ref[...] + jnp.sum(onehot, axis=0, keepdims=True)
    cnt_ref[...] = carry_ref[...]
    exps = [jnp.exp(v - vals[0]) for v in vals]
    denom = exps[0] + exps[1] + exps[2] + exps[3]
    idx_out = jnp.zeros_like(work)
    gate_out = jnp.zeros_like(work)
    for k in range(TOP_K):
        rank = jnp.sum(jnp.where(sels[k], before, 0.0), axis=1, keepdims=True)
        idx_out = jnp.where(lane == k, ids[k], jnp.where(lane == TOP_K + k, rank, idx_out))
        gate_out = jnp.where(lane == k, exps[k] / denom, gate_out)
    idx_ref[...] = idx_out.astype(jnp.int32)
    gate_ref[...] = gate_out


def _route(logits):
    t = logits.shape[0]
    tm = ROUTE_TILE
    tri = jnp.asarray(np.tril(np.ones((tm, tm), np.float32), -1), BF16)
    row = pl.BlockSpec((tm, LANES), lambda i: (i, 0))
    return pl.pallas_call(
        _route_kernel,
        grid=(t // tm,),
        in_specs=[row, pl.BlockSpec((tm, tm), lambda i: (0, 0))],
        out_specs=[row, row, pl.BlockSpec((1, LANES), lambda i: (0, 0))],
        out_shape=[
            jax.ShapeDtypeStruct((t, LANES), jnp.int32),
            jax.ShapeDtypeStruct((t, LANES), F32),
            jax.ShapeDtypeStruct((1, LANES), F32),
        ],
        scratch_shapes=[pltpu.VMEM((1, LANES), F32)],
        compiler_params=_params("arbitrary"),
        name="route",
    )(logits, tri)


def _zero_fill(fill_start_ref, fill_len_ref, n_used_ref, buf_hbm, zero_vmem, sem_fill):
    n_blk = buf_hbm.shape[0] // (EXPERT_ROWS * TOKEN_SUBLANES)
    zero_vmem[...] = jnp.zeros_like(zero_vmem)
    chunks = [1 << b for b in reversed(range(EXPERT_ROWS.bit_length() - 1))]

    def pads(wait):
        def body(e, carry):
            start, length = fill_start_ref[e], fill_len_ref[e]
            for c in chunks:
                @pl.when((length & c) != 0)
                def _():
                    row = pl.multiple_of((start + (length & (-2 * c))) * TOKEN_SUBLANES, TOKEN_SUBLANES)
                    cp = pltpu.make_async_copy(zero_vmem.at[pl.ds(0, c * TOKEN_SUBLANES), :],
                                               buf_hbm.at[pl.ds(row, c * TOKEN_SUBLANES), :], sem_fill)
                    cp.wait() if wait else cp.start()
            return carry

        lax.fori_loop(0, N_EXPERTS, body, 0)

    def tail(wait):
        def body(j, carry):
            row = pl.multiple_of(j * (EXPERT_ROWS * TOKEN_SUBLANES), EXPERT_ROWS * TOKEN_SUBLANES)
            cp = pltpu.make_async_copy(zero_vmem, buf_hbm.at[pl.ds(row, EXPERT_ROWS * TOKEN_SUBLANES), :], sem_fill)
            cp.wait() if wait else cp.start()
            return carry

        lax.fori_loop(n_used_ref[0], n_blk, body, 0)

    pads(False)
    tail(False)
    pads(True)
    tail(True)


def _dispatch_kernel(fill_start_ref, fill_len_ref, n_used_ref, *rest, tiles):
    v_refs = rest[:len(tiles)]
    pos_hbm, buf_hbm, idx_smem, sem_idx, sem_rows, zero_vmem, sem_fill = rest[len(tiles):]
    tm = v_refs[0].shape[0] // TOKEN_SUBLANES
    per_tile = tm * TOP_K
    i = pl.program_id(0)
    fetch = pltpu.make_async_copy(pos_hbm.at[pl.ds(i * per_tile, per_tile)], idx_smem, sem_idx)
    fetch.start()

    @pl.when(i == 0)
    def _():
        _zero_fill(fill_start_ref, fill_len_ref, n_used_ref, buf_hbm, zero_vmem, sem_fill)

    fetch.wait()

    def scatter(v_ref):
        def row_copy(t, p):
            return pltpu.make_async_copy(_token_tile(v_ref, t), _token_tile(buf_hbm, p), sem_rows)

        def issue(j, carry):
            for u in range(DMA_UNROLL):
                t = j * DMA_UNROLL + u
                for k in range(TOP_K):
                    row_copy(t, idx_smem[t * TOP_K + k]).start(priority=k % 2)
            return carry

        def drain(j, carry):
            for _ in range(DMA_UNROLL * TOP_K):
                row_copy(0, 0).wait()
            return carry

        lax.fori_loop(0, tm // DMA_UNROLL, issue, 0)
        lax.fori_loop(0, tm // DMA_UNROLL, drain, 0)

    first = 0
    for v_ref, n_tiles in zip(v_refs, tiles):
        @pl.when((i >= first) & (i < first + n_tiles))
        def _():
            scatter(v_ref)

        first += n_tiles


def _dispatch(fill_start, fill_len, n_used, v_parts, pos_flat, n_rows):
    tm = DISPATCH_TILE
    tiles = tuple(v.shape[0] // (tm * TOKEN_SUBLANES) for v in v_parts)
    firsts = [sum(tiles[:p]) for p in range(len(tiles))]
    any_spec = pl.BlockSpec(memory_space=pl.ANY)

    def part_spec(first, n_tiles):
        return pl.BlockSpec((tm * TOKEN_SUBLANES, LANES), lambda i, *_: (jnp.clip(i - first, 0, n_tiles - 1), 0))

    grid_spec = pltpu.PrefetchScalarGridSpec(
        num_scalar_prefetch=3,
        grid=(sum(tiles),),
        in_specs=[part_spec(f, n) for f, n in zip(firsts, tiles)] + [any_spec],
        out_specs=any_spec,
        scratch_shapes=[pltpu.SMEM((tm * TOP_K,), jnp.int32), pltpu.SemaphoreType.DMA, pltpu.SemaphoreType.DMA,
                        pltpu.VMEM((EXPERT_ROWS * TOKEN_SUBLANES, LANES), F32), pltpu.SemaphoreType.DMA],
    )
    return pl.pallas_call(
        functools.partial(_dispatch_kernel, tiles=tiles),
        grid_spec=grid_spec,
        out_shape=jax.ShapeDtypeStruct((n_rows * TOKEN_SUBLANES, LANES), F32),
        compiler_params=_params("arbitrary"),
        name="dispatch",
    )(fill_start, fill_len, n_used, *v_parts, pos_flat)


def _pair_shuffle():
    l = np.arange(LANES)
    p = np.zeros((LANES, LANES), np.float32)
    p[l, l // 2 + (LANES // 2) * (l % 2)] = 1.0
    return p


def _expert_kernel(blk_e_ref, n_used_ref, run_ref, next_ref, x_ref, bu_ref, bd_ref, perm_ref, wu_hbm, wd_hbm, o_ref,
                   wu_f32, wd_f32, wu_bf, wd_bf, sem_u, sem_d, *, layer):
    i = pl.program_id(0)
    active = i < n_used_ref[0]
    expert = blk_e_ref[i]
    new_expert = (i == 0) | (expert != blk_e_ref[jnp.maximum(i - 1, 0)])

    def fetch(e, slot):
        return (pltpu.make_async_copy(wu_hbm.at[layer, e], wu_f32.at[slot], sem_u.at[slot]),
                pltpu.make_async_copy(wd_hbm.at[layer, e], wd_f32.at[slot], sem_d.at[slot]))

    @pl.when(active & new_expert)
    def _():
        slot = run_ref[expert] % 2

        @pl.when(i == 0)
        def _():
            for cp in fetch(expert, slot):
                cp.start()

        for cp in fetch(expert, slot):
            cp.wait()
        nxt = next_ref[expert]

        @pl.when(nxt >= 0)
        def _():
            for cp in fetch(nxt, 1 - slot):
                cp.start()

        def cast(c, carry):
            r = pl.multiple_of(c * LANES, LANES)
            wu_bf[pl.ds(r, LANES), :] = wu_f32[slot, pl.ds(r, LANES), :].astype(BF16)
            wd_bf[pl.ds(r, LANES), :] = jnp.dot(perm_ref[...], wd_f32[slot, pl.ds(r, LANES), :].astype(BF16),
                                               preferred_element_type=F32).astype(BF16)
            return carry

        lax.fori_loop(0, D_MODEL // LANES, cast, 0)

    @pl.when(active)
    def _():
        h = jnp.dot(_load_token_tiles(x_ref, EXPERT_ROWS).astype(BF16), wu_bf[...], preferred_element_type=F32) + bu_ref[0]
        even = lax.broadcasted_iota(jnp.int32, (EXPERT_ROWS, LANES), 1) % 2 == 0
        acts = []
        for k in range(D_EXPERT // LANES):
            ha = h[:, 2 * k * LANES:(2 * k + 1) * LANES]
            hb = h[:, (2 * k + 1) * LANES:(2 * k + 2) * LANES]
            glu = jnp.where(even, ha, pltpu.roll(hb, 1, axis=1))
            lin = jnp.where(even, pltpu.roll(ha, LANES - 1, axis=1), hb)
            glu = jnp.minimum(glu, SWIGLU_LIMIT)
            lin = jnp.clip(lin, -SWIGLU_LIMIT, SWIGLU_LIMIT)
            acts.append((glu * jax.nn.sigmoid(SWIGLU_ALPHA * glu) * (lin + 1.0)).astype(BF16))
        a = jnp.concatenate(acts, axis=1)
        _store_token_tiles(o_ref, jnp.dot(a, wd_bf[...], preferred_element_type=F32) + bd_ref[0])

    @pl.when(jnp.logical_not(active))
    def _():
        o_ref[...] = jnp.zeros_like(o_ref)


def _experts(blk_e, n_used, run_idx, next_expert, xbuf, layer, wu, bu, wd, bd):
    n_blk = xbuf.shape[0] // (EXPERT_ROWS * TOKEN_SUBLANES)
    grid_spec = pltpu.PrefetchScalarGridSpec(
        num_scalar_prefetch=4,
        grid=(n_blk,),
        in_specs=[
            pl.BlockSpec((EXPERT_ROWS * TOKEN_SUBLANES, LANES), lambda i, e, *_: (i, 0)),
            pl.BlockSpec((1, 1, 2 * D_EXPERT), lambda i, e, *_: (e[i], 0, 0)),
            pl.BlockSpec((1, 1, D_MODEL), lambda i, e, *_: (e[i], 0, 0)),
            pl.BlockSpec((LANES, LANES), lambda i, e, *_: (0, 0)),
            pl.BlockSpec(memory_space=pl.ANY),
            pl.BlockSpec(memory_space=pl.ANY),
        ],
        out_specs=pl.BlockSpec((EXPERT_ROWS * TOKEN_SUBLANES, LANES), lambda i, e, *_: (i, 0)),
        scratch_shapes=[
            pltpu.VMEM((2, D_MODEL, 2 * D_EXPERT), F32),
            pltpu.VMEM((2, D_EXPERT, D_MODEL), F32),
            pltpu.VMEM((D_MODEL, 2 * D_EXPERT), BF16),
            pltpu.VMEM((D_EXPERT, D_MODEL), BF16),
            pltpu.SemaphoreType.DMA((2,)),
            pltpu.SemaphoreType.DMA((2,)),
        ],
    )
    return pl.pallas_call(
        functools.partial(_expert_kernel, layer=layer),
        grid_spec=grid_spec,
        out_shape=jax.ShapeDtypeStruct(xbuf.shape, F32),
        compiler_params=pltpu.CompilerParams(dimension_semantics=("arbitrary",), vmem_limit_bytes=EXPERT_VMEM_LIMIT),
        name="experts",
    )(blk_e, n_used, run_idx, next_expert, xbuf, bu, bd, jnp.asarray(_pair_shuffle(), BF16), wu, wd)


def _combine_kernel(x_ref, gate_ref, g2_ref, lng_ref, lnb_ref, pos_hbm, y_hbm, o_ref, idx_smem, rows, sem_idx, sem_rows):
    tm = x_ref.shape[0]
    per_tile = tm * TOP_K
    i = pl.program_id(0)

    def row_copy(slot, t, k, p):
        return pltpu.make_async_copy(_token_tile(y_hbm, p), _token_tile(rows, (slot * TOP_K + k) * tm + t), sem_rows.at[slot])

    def start_gather(tile, slot):
        fetch = pltpu.make_async_copy(pos_hbm.at[pl.ds(tile * per_tile, per_tile)], idx_smem, sem_idx)
        fetch.start()
        fetch.wait()

        def issue(j, carry):
            for u in range(DMA_UNROLL):
                t = j * DMA_UNROLL + u
                for k in range(TOP_K):
                    row_copy(slot, t, k, idx_smem[t * TOP_K + k]).start(priority=k % 2)
            return carry

        lax.fori_loop(0, tm // DMA_UNROLL, issue, 0)

    @pl.when(i == 0)
    def _():
        start_gather(0, 0)

    @pl.when(i + 1 < pl.num_programs(0))
    def _():
        start_gather(i + 1, (i + 1) % 2)

    slot = i % 2

    def drain(j, carry):
        for _ in range(DMA_UNROLL * TOP_K):
            row_copy(slot, 0, 0, 0).wait()
        return carry

    lax.fori_loop(0, tm // DMA_UNROLL, drain, 0)
    chunks = []
    for c in range(TOKEN_SUBLANES):
        def part(k):
            first = pl.multiple_of((slot * TOP_K + k) * (tm * TOKEN_SUBLANES), tm * TOKEN_SUBLANES) + c
            return gate_ref[:, k:k + 1] * rows[pl.ds(first, tm, stride=TOKEN_SUBLANES), :]

        fc = part(0)
        for k in range(1, TOP_K):
            fc = fc + part(k)
        chunks.append(fc)
    f = jnp.concatenate(chunks, axis=1)
    o_ref[...] = _layer_norm(DEEPNORM_ALPHA * x_ref[...] + g2_ref[0] * f, lng_ref[...], lnb_ref[...])


def _combine(x2, gates, pos_flat, ybuf, g2, lng, lnb, seq_rows, mod_row):
    t = x2.shape[0]
    tm = COMBINE_TILE
    row = pl.BlockSpec((tm, D_MODEL), lambda i: (i, 0))
    const = pl.BlockSpec((1, D_MODEL), lambda i: (0, 0))
    return pl.pallas_call(
        _combine_kernel,
        grid=(t // tm,),
        in_specs=[row, pl.BlockSpec((tm, LANES), lambda i: (i, 0)), _mod_spec(tm, seq_rows, mod_row), const, const,
                  pl.BlockSpec(memory_space=pl.ANY), pl.BlockSpec(memory_space=pl.ANY)],
        out_specs=row,
        out_shape=jax.ShapeDtypeStruct((t, D_MODEL), F32),
        scratch_shapes=[
            pltpu.SMEM((tm * TOP_K,), jnp.int32),
            pltpu.VMEM((2 * TOP_K * tm * TOKEN_SUBLANES, LANES), F32),
            pltpu.SemaphoreType.DMA,
            pltpu.SemaphoreType.DMA((2,)),
        ],
        compiler_params=_params("arbitrary"),
        name="combine",
    )(x2, gates, g2, lng, lnb, pos_flat, ybuf)


def _moe(v_parts, logits, layer, wu, bu, wd, bd):
    t = logits.shape[0]
    n = t * TOP_K
    idx, gates, cnt = _route(logits)
    counts = cnt[0, :N_EXPERTS].astype(jnp.int32)
    padded = (counts + EXPERT_ROWS - 1) // EXPERT_ROWS * EXPERT_ROWS
    pad_ends = jnp.cumsum(padded)
    pad_starts = pad_ends - padded
    experts = jnp.arange(N_EXPERTS, dtype=jnp.int32)
    start = jnp.sum(jnp.where(idx[:, :TOP_K, None] == experts, pad_starts, 0), axis=-1)
    pos_flat = (start + idx[:, TOP_K:2 * TOP_K]).reshape(n)
    n_blk = n // EXPERT_ROWS + N_EXPERTS
    blk_start = jnp.arange(n_blk, dtype=jnp.int32) * EXPERT_ROWS
    blk_e = jnp.minimum(jnp.sum((pad_ends[None, :] <= blk_start[:, None]).astype(jnp.int32), axis=1), N_EXPERTS - 1)
    n_used = pad_ends[-1:] // EXPERT_ROWS
    nonempty = counts > 0
    run_idx = jnp.cumsum(nonempty.astype(jnp.int32)) - 1
    later = nonempty[None, :] & (experts[None, :] > experts[:, None])
    next_expert = jnp.min(jnp.where(later, experts[None, :], N_EXPERTS), axis=1)
    next_expert = jnp.where(next_expert == N_EXPERTS, -1, next_expert)
    fill_start, fill_len = pad_starts + counts, padded - counts
    xbuf = _dispatch(fill_start, fill_len, n_used, v_parts, pos_flat, n_blk * EXPERT_ROWS)
    ybuf = _experts(blk_e, n_used, run_idx, next_expert, xbuf, layer, wu, bu, wd, bd)
    return ybuf, pos_flat, gates


def _rope_tables(seq):
    rows = seq // GRID_W
    row = jnp.repeat(jnp.arange(rows), GRID_W).astype(F32)
    col = jnp.tile(jnp.arange(GRID_W), rows).astype(F32)
    n_freq = HEAD_DIM // 4
    inv = ROPE_BASE ** (-jnp.arange(n_freq, dtype=F32) / n_freq)
    ang = jnp.concatenate([row[:, None] * inv, col[:, None] * inv], -1)
    ang = jnp.concatenate([ang, ang], -1)
    sign = jnp.where(jnp.arange(HEAD_DIM) < HEAD_DIM // 2, -1.0, 1.0).astype(F32)
    cos = jnp.cos(ang)
    sin = jnp.sin(ang) * sign
    return jnp.tile(cos, (1, LANES // HEAD_DIM)), jnp.tile(sin, (1, LANES // HEAD_DIM))


def kernel(x, c, ctx, c_ctx, w_mod, b_mod, w_in_even, sink, w_out_even, w_in_odd, conv_w, w_out_odd, ln_g, ln_b,
           w_router, b_router, w_up, b_up, w_down, b_down):
    B, S, D = x.shape
    L = ctx.shape[1]
    T, TC = B * S, B * L
    cos, sin = _rope_tables(S)
    cond = jnp.concatenate([c, c_ctx[None, :], jnp.zeros((MOD_ROWS - B - 1, D), F32)], 0)
    mod = _modulation(cond, w_mod, b_mod)

    bu = b_up.reshape(DEPTH, N_EXPERTS, 1, 2 * D_EXPERT)
    bd = b_down.reshape(DEPTH, N_EXPERTS, 1, D)
    wr_hi = w_router.astype(BF16)
    wr_lo = (w_router - wr_hi.astype(F32)).astype(BF16)
    pad_r = ((0, 0), (0, 0), (0, LANES - N_EXPERTS))
    wr_hi, wr_lo = jnp.pad(wr_hi, pad_r), jnp.pad(wr_lo, pad_r)
    br = jnp.pad(b_router, ((0, 0), (0, LANES - N_EXPERTS)), constant_values=NEG_INF).reshape(DEPTH, 1, LANES)

    x2 = x.reshape(T, D)
    xc2 = ctx.reshape(TC, D)
    for l in range(DEPTH):
        even = l % 2 == 0
        j = l // 2
        ctx_after = any(m % 2 == 0 for m in range(l + 1, DEPTH))
        sh1, sc1, g1, sh2, sc2, g2 = [m.reshape(MOD_ROWS, 1, D) for m in jnp.split(mod[l], 6, axis=-1)]
        lng1, lnb1 = ln_g[l, 0].reshape(1, D), ln_b[l, 0].reshape(1, D)
        lng2, lnb2 = ln_g[l, 1].reshape(1, D), ln_b[l, 1].reshape(1, D)
        post = (g1, sc2, sh2, lng1, lnb1, wr_hi[l], wr_lo[l], br[l])
        if even:
            w_in = w_in_even[j].astype(BF16)
            w_out = w_out_even[j].astype(BF16)
            h = _inproj_even(x2, sc1, sh1, w_in, cos, sin, S, None, True)
            hc = _inproj_even(xc2, sc1, sh1, w_in, cos, sin, L, B, False)
            kcol = ATTN_WIDTH // KV_WIDTH
            attn = _win_attention(h, hc, sink[j], B, S, kcol, kcol + 1)
            four = _fourier_mix(h, B, S)
            x2, v, logits = _outproj("even", (attn, four), w_out, x2, *post, S, None)
            if ctx_after:
                attn_c = _ctx_attention(hc, sink[j], B)
                four_c = _fourier_mix(hc, B, L)
                xc2, vc, logits_c = _outproj("even", (attn_c, four_c), w_out, xc2, *post, L, B)
        else:
            w_in = w_in_odd[j].astype(BF16)
            w_out = w_out_odd[j].astype(BF16)
            cw = jnp.pad(conv_w[j], ((0, 8 - conv_w.shape[1]), (0, 0)))
            bz = _inproj_odd(x2, sc1, sh1, w_in, S, None)
            x2, v, logits = _outproj("odd", (bz,), w_out, x2, *post, S, None, conv_w=cw)
            if ctx_after:
                bzc = _inproj_odd(xc2, sc1, sh1, w_in, L, B)
                xc2, vc, logits_c = _outproj("odd", (bzc,), w_out, xc2, *post, L, B, conv_w=cw)
        if ctx_after:
            ybuf, pos, gates = _moe((v, vc), jnp.concatenate([logits, logits_c], 0), l, w_up, bu[l], w_down, bd[l])
            xc2 = _combine(xc2, gates[T:], pos[T * TOP_K:], ybuf, g2, lng2, lnb2, L, B)
        else:
            ybuf, pos, gates = _moe((v,), logits, l, w_up, bu[l], w_down, bd[l])
        x2 = _combine(x2, gates[:T], pos[:T * TOP_K], ybuf, g2, lng2, lnb2, S, None)
    return x2.reshape(B, S, D)
```

```python
import functools

import numpy as np
import jax
import jax.numpy as jnp
from jax import lax
from jax.experimental import pallas as pl
from jax.experimental.pallas import tpu as pltpu

D_MODEL = 1024
DEPTH = 4
GRID_W = 64
HEAD_DIM = 64
N_Q_HEADS = 12
N_KV_HEADS = 4
Q_PER_KV = N_Q_HEADS // N_KV_HEADS
BLOCK = 128
ROPE_BASE = 10000.0
FOURIER_GROUP_DIM = 64
FOURIER_WIDTH = 256
ATTN_WIDTH = N_Q_HEADS * HEAD_DIM
KV_WIDTH = N_KV_HEADS * HEAD_DIM
EVEN_IN_WIDTH = ATTN_WIDTH + 2 * KV_WIDTH + FOURIER_WIDTH
N_EXPERTS = 32
TOP_K = 4
D_EXPERT = D_MODEL
SWIGLU_LIMIT = 7.0
SWIGLU_ALPHA = 1.702
LN_EPS = 1e-5
NEG_INF = -1e30
DEEPNORM_ALPHA = (2 * DEPTH) ** 0.25

LANES = 128
SUBLANES = 8
TOKEN_SUBLANES = D_MODEL // LANES
ROW_TILE = 1024
EXPERT_ROWS = 512
MOD_ROWS = 16
ROUTE_TILE = 512
DISPATCH_TILE = 1024
COMBINE_TILE = 512
DMA_UNROLL = 8
VMEM_LIMIT = 48 * 1024 * 1024
EXPERT_VMEM_LIMIT = 56 * 1024 * 1024

F32 = jnp.float32
BF16 = jnp.bfloat16


def _params(*sem):
    return pltpu.CompilerParams(dimension_semantics=sem, vmem_limit_bytes=VMEM_LIMIT)


def _mod_spec(tm, seq_rows, mod_row):
    if mod_row is None:
        return pl.BlockSpec((1, 1, D_MODEL), lambda i: ((i * tm) // seq_rows, 0, 0))
    return pl.BlockSpec((1, 1, D_MODEL), lambda i: (mod_row, 0, 0))


def _mod_kernel(c_ref, w_ref, b_ref, o_ref):
    c = c_ref[...]
    s = (c * jax.nn.sigmoid(c)).astype(BF16)
    o_ref[0] = jnp.dot(s, w_ref[0].astype(BF16), preferred_element_type=F32) + b_ref[0]


def _modulation(cond, w_mod, b_mod):
    tn = 1536
    n = w_mod.shape[-1]
    return pl.pallas_call(
        _mod_kernel,
        grid=(DEPTH, n // tn),
        in_specs=[
            pl.BlockSpec((MOD_ROWS, D_MODEL), lambda l, j: (0, 0)),
            pl.BlockSpec((1, D_MODEL, tn), lambda l, j: (l, 0, j)),
            pl.BlockSpec((1, 1, tn), lambda l, j: (l, 0, j)),
        ],
        out_specs=pl.BlockSpec((1, MOD_ROWS, tn), lambda l, j: (l, 0, j)),
        out_shape=jax.ShapeDtypeStruct((DEPTH, MOD_ROWS, n), F32),
        compiler_params=_params("arbitrary", "arbitrary"),
        name="modulation",
    )(cond, w_mod, b_mod.reshape(DEPTH, 1, n))


def _rope_chunk(h, cos, sin_signed, first_half):
    swapped = jnp.where(first_half, pltpu.roll(h, LANES - HEAD_DIM // 2, axis=1), pltpu.roll(h, HEAD_DIM // 2, axis=1))
    return h * cos + swapped * sin_signed


def _inproj_even_kernel(x_ref, sc_ref, sh_ref, w_ref, cos_ref, sin_ref, o_ref, *, rope):
    _inproj_even_body(x_ref[...], sc_ref, sh_ref, w_ref, cos_ref, sin_ref, o_ref, rope)


def _inproj_even_body(x, sc_ref, sh_ref, w_ref, cos_ref, sin_ref, o_ref, rope):
    u = (x * (1.0 + sc_ref[0]) + sh_ref[0]).astype(BF16)
    tn = 512
    if rope:
        cos = cos_ref[...]
        sin = sin_ref[...]
        lane = lax.broadcasted_iota(jnp.int32, cos.shape, 1)
        first_half = (lane % HEAD_DIM) < HEAD_DIM // 2
    scale = HEAD_DIM ** -0.5
    for jt in range(EVEN_IN_WIDTH // tn):
        acc = jnp.dot(u, w_ref[:, jt * tn:(jt + 1) * tn], preferred_element_type=F32)
        for k in range(tn // LANES):
            col = jt * tn + k * LANES
            h = acc[:, k * LANES:(k + 1) * LANES]
            if rope and col < ATTN_WIDTH + KV_WIDTH:
                h = _rope_chunk(h, cos, sin, first_half)
            if col < ATTN_WIDTH:
                h = h * scale
            o_ref[:, col:col + LANES] = h.astype(BF16)


def _inproj_even(x2, sc, sh, w, cos, sin, seq_rows, mod_row, rope):
    t = x2.shape[0]
    tm = min(ROW_TILE, seq_rows)
    pos_tiles = cos.shape[0] // tm
    mod = _mod_spec(tm, seq_rows, mod_row)

    return pl.pallas_call(
        functools.partial(_inproj_even_kernel, rope=rope),
        grid=(t // tm,),
        in_specs=[
            pl.BlockSpec((tm, D_MODEL), lambda i: (i, 0)),
            mod,
            mod,
            pl.BlockSpec((D_MODEL, EVEN_IN_WIDTH), lambda i: (0, 0)),
            pl.BlockSpec((tm, LANES), lambda i: (i % pos_tiles, 0)),
            pl.BlockSpec((tm, LANES), lambda i: (i % pos_tiles, 0)),
        ],
        out_specs=pl.BlockSpec((tm, EVEN_IN_WIDTH), lambda i: (i, 0)),
        out_shape=jax.ShapeDtypeStruct((t, EVEN_IN_WIDTH), BF16),
        compiler_params=_params("arbitrary"),
        name="inproj_even",
    )(x2, sc, sh, w, cos, sin)


def _attend(q_ref, k_all, v_all, valid, sink_ref, o_ref):
    rows = q_ref.shape[0]
    lane = lax.broadcasted_iota(jnp.int32, (rows, LANES), 1)
    low = lane < HEAD_DIM
    scores, sks = [], []
    for kv in range(N_KV_HEADS):
        kv_chunk, kv_half = divmod(kv, 2)
        keep = low if kv_half == 0 else jnp.logical_not(low)
        stack, sinks = [], []
        for g in range(Q_PER_KV):
            head = kv * Q_PER_KV + g
            chunk, half = divmod(head, 2)
            qh = q_ref[:, chunk * LANES:(chunk + 1) * LANES].astype(F32)
            if half != kv_half:
                qh = pltpu.roll(qh, HEAD_DIM, axis=1)
            stack.append(jnp.where(keep, qh, 0.0).astype(BF16))
            sinks.append(jnp.full((rows, 1), sink_ref[head], F32))
        qs = jnp.concatenate(stack, axis=0)
        sks.append(jnp.concatenate(sinks, axis=0))
        kc = k_all[:, kv_chunk * LANES:(kv_chunk + 1) * LANES]
        s = lax.dot_general(qs, kc, (((1,), (1,)), ((), ())), preferred_element_type=F32)
        if valid is not None:
            s = jnp.concatenate([s[:, j * BLOCK:(j + 1) * BLOCK] if ok is None
                                 else jnp.where(ok, s[:, j * BLOCK:(j + 1) * BLOCK], NEG_INF)
                                 for j, ok in enumerate(valid)], axis=1)
        scores.append(s)
    probs, sink_terms = [], []
    for kv in range(N_KV_HEADS):
        m = jnp.maximum(sks[kv], jnp.max(scores[kv], axis=-1, keepdims=True))
        probs.append(jnp.exp(scores[kv] - m).astype(BF16))
        sink_terms.append(jnp.exp(sks[kv] - m))
    key_lane = lax.broadcasted_iota(jnp.int32, (v_all.shape[0], LANES), 1)
    for kv in range(N_KV_HEADS):
        kv_chunk, kv_half = divmod(kv, 2)
        vc = v_all[:, kv_chunk * LANES:(kv_chunk + 1) * LANES]
        own = (key_lane < HEAD_DIM) if kv_half == 0 else (key_lane >= HEAD_DIM)
        o = jnp.dot(probs[kv], jnp.where(own, vc, jnp.ones_like(vc)), preferred_element_type=F32)
        other = (1 - kv_half) * HEAD_DIM
        o = o / (o[:, other:other + 1] + sink_terms[kv])
        for g in range(Q_PER_KV):
            head = kv * Q_PER_KV + g
            half = head % 2
            oh = o[g * rows:(g + 1) * rows]
            if half != kv_half:
                oh = pltpu.roll(oh, HEAD_DIM, axis=1)
            o_ref[:, head * HEAD_DIM:(head + 1) * HEAD_DIM] = oh[:, half * HEAD_DIM:(half + 1) * HEAD_DIM].astype(BF16)


def _win_attn_kernel(sink_ref, q_ref, kp_ref, kc_ref, kn_ref, vp_ref, vc_ref, vn_ref, kx_ref, vx_ref, o_ref, *, nb):
    n = pl.program_id(1)
    k_all = jnp.concatenate([kp_ref[...], kc_ref[...], kn_ref[...], kx_ref[...]], axis=0)
    v_all = jnp.concatenate([vp_ref[...], vc_ref[...], vn_ref[...], vx_ref[...]], axis=0)
    r = lax.broadcasted_iota(jnp.int32, (Q_PER_KV * BLOCK, BLOCK), 0) % BLOCK
    c = lax.broadcasted_iota(jnp.int32, (Q_PER_KV * BLOCK, BLOCK), 1)
    in_prev = c >= jnp.where(n > 0, r, BLOCK)
    in_next = c <= jnp.where(n < nb - 1, r, -1)
    valid = [in_prev, None, in_next] + [None] * (kx_ref.shape[0] // BLOCK)
    _attend(q_ref, k_all, v_all, valid, sink_ref, o_ref)


def _win_attention(h, hc, sink, batch, seq, kc_col, vc_col):
    nb = seq // BLOCK
    qcol, kcol, vcol = 0, ATTN_WIDTH // KV_WIDTH, ATTN_WIDTH // KV_WIDTH + 1

    def blk(shift, col):
        return pl.BlockSpec((BLOCK, KV_WIDTH), lambda b, n: (b * nb + jnp.clip(n + shift, 0, nb - 1), col))

    return pl.pallas_call(
        functools.partial(_win_attn_kernel, nb=nb),
        grid=(batch, nb),
        in_specs=[
            pl.BlockSpec(memory_space=pltpu.SMEM),
            pl.BlockSpec((BLOCK, ATTN_WIDTH), lambda b, n: (b * nb + n, qcol)),
            blk(-1, kcol), blk(0, kcol), blk(1, kcol),
            blk(-1, vcol), blk(0, vcol), blk(1, vcol),
            pl.BlockSpec((hc.shape[0] // batch, KV_WIDTH), lambda b, n: (b, kc_col)),
            pl.BlockSpec((hc.shape[0] // batch, KV_WIDTH), lambda b, n: (b, vc_col)),
        ],
        out_specs=pl.BlockSpec((BLOCK, ATTN_WIDTH), lambda b, n: (b * nb + n, 0)),
        out_shape=jax.ShapeDtypeStruct((batch * seq, ATTN_WIDTH), BF16),
        compiler_params=_params("arbitrary", "arbitrary"),
        name="window_attention",
    )(sink, h, h, h, h, h, h, h, hc, hc)


def _ctx_attn_kernel(sink_ref, q_ref, k_ref, v_ref, o_ref):
    _attend(q_ref, k_ref[...], v_ref[...], None, sink_ref, o_ref)


def _ctx_attention(hc, sink, batch):
    ctx_len = hc.shape[0] // batch
    kcol, vcol = ATTN_WIDTH // KV_WIDTH, ATTN_WIDTH // KV_WIDTH + 1
    return pl.pallas_call(
        _ctx_attn_kernel,
        grid=(batch,),
        in_specs=[
            pl.BlockSpec(memory_space=pltpu.SMEM),
            pl.BlockSpec((ctx_len, ATTN_WIDTH), lambda b: (b, 0)),
            pl.BlockSpec((ctx_len, KV_WIDTH), lambda b: (b, kcol)),
            pl.BlockSpec((ctx_len, KV_WIDTH), lambda b: (b, vcol)),
        ],
        out_specs=pl.BlockSpec((ctx_len, ATTN_WIDTH), lambda b: (b, 0)),
        out_shape=jax.ShapeDtypeStruct((hc.shape[0], ATTN_WIDTH), BF16),
        compiler_params=_params("arbitrary"),
        name="context_attention",
    )(sink, hc, hc, hc)


def _dft_mats(n):
    k = np.arange(n, dtype=np.int64)
    ang = 2.0 * np.pi * ((k[:, None] * k[None, :]) % n).astype(np.float64) / n
    return np.cos(ang), np.sin(ang)


def _group_dft():
    c, s = _dft_mats(FOURIER_GROUP_DIM)
    groups = FOURIER_WIDTH // FOURIER_GROUP_DIM
    eye = np.eye(groups)
    return np.concatenate([np.kron(eye, c), np.kron(eye, s)], axis=1)


def _fourier_kernel(f_ref, w1_ref, c_ref, s_ref, o_ref, ab_ref, *, scale):
    b = pl.program_id(1)

    @pl.when(pl.program_id(0) == 0)
    def _():
        ab_ref[b] = jnp.dot(f_ref[...], w1_ref[...], preferred_element_type=F32).astype(BF16)

    y = (jnp.dot(c_ref[...], ab_ref[b, :, :FOURIER_WIDTH], preferred_element_type=F32)
         - jnp.dot(s_ref[...], ab_ref[b, :, FOURIER_WIDTH:], preferred_element_type=F32))
    o_ref[...] = (y * scale).astype(BF16)


def _fourier_mix(h, batch, n):
    tr = 256
    cn, sn = _dft_mats(n)
    fcol = (EVEN_IN_WIDTH - FOURIER_WIDTH) // FOURIER_WIDTH
    return pl.pallas_call(
        functools.partial(_fourier_kernel, scale=float((n * FOURIER_GROUP_DIM) ** -0.5)),
        grid=(n // tr, batch),
        in_specs=[
            pl.BlockSpec((n, FOURIER_WIDTH), lambda i, b: (jnp.where(i == 0, b, batch - 1), fcol)),
            pl.BlockSpec((FOURIER_WIDTH, 2 * FOURIER_WIDTH), lambda i, b: (0, 0)),
            pl.BlockSpec((tr, n), lambda i, b: (i, 0)),
            pl.BlockSpec((tr, n), lambda i, b: (i, 0)),
        ],
        out_specs=pl.BlockSpec((tr, FOURIER_WIDTH), lambda i, b: (b * (n // tr) + i, 0)),
        out_shape=jax.ShapeDtypeStruct((batch * n, FOURIER_WIDTH), BF16),
        scratch_shapes=[pltpu.VMEM((batch, n, 2 * FOURIER_WIDTH), BF16)],
        compiler_params=_params("arbitrary", "arbitrary"),
        name="fourier_mix",
    )(h, jnp.asarray(_group_dft(), BF16), jnp.asarray(cn, BF16), jnp.asarray(sn, BF16))


def _inproj_odd_kernel(x_ref, sc_ref, sh_ref, w_ref, o_ref):
    _inproj_odd_body(x_ref[...], sc_ref, sh_ref, w_ref, o_ref)


def _inproj_odd_body(x, sc_ref, sh_ref, w_ref, o_ref):
    u = (x * (1.0 + sc_ref[0]) + sh_ref[0]).astype(BF16)
    tn = 512
    for jt in range(D_MODEL // tn):
        cols = slice(jt * tn, (jt + 1) * tn)
        b_gate = jnp.dot(u, w_ref[:, jt * tn:(jt + 1) * tn], preferred_element_type=F32)
        o_ref[:, cols] = b_gate.astype(BF16)
        c_gate = jnp.dot(u, w_ref[:, D_MODEL + jt * tn:D_MODEL + (jt + 1) * tn], preferred_element_type=F32)
        hh = jnp.dot(u, w_ref[:, 2 * D_MODEL + jt * tn:2 * D_MODEL + (jt + 1) * tn], preferred_element_type=F32)
        o_ref[:, D_MODEL + jt * tn:D_MODEL + (jt + 1) * tn] = (c_gate * hh).astype(BF16)


def _inproj_odd(x2, sc, sh, w, seq_rows, mod_row):
    t = x2.shape[0]
    tm = min(ROW_TILE, seq_rows)
    mod = _mod_spec(tm, seq_rows, mod_row)

    return pl.pallas_call(
        _inproj_odd_kernel,
        grid=(t // tm,),
        in_specs=[
            pl.BlockSpec((tm, D_MODEL), lambda i: (i, 0)),
            mod,
            mod,
            pl.BlockSpec((D_MODEL, 3 * D_MODEL), lambda i: (0, 0)),
        ],
        out_specs=pl.BlockSpec((tm, 2 * D_MODEL), lambda i: (i, 0)),
        out_shape=jax.ShapeDtypeStruct((t, 2 * D_MODEL), BF16),
        compiler_params=_params("arbitrary"),
        name="inproj_odd",
    )(x2, sc, sh, w)


def _layer_norm(r, g, b):
    mu = jnp.mean(r, axis=-1, keepdims=True)
    d = r - mu
    var = jnp.mean(d * d, axis=-1, keepdims=True)
    return d * lax.rsqrt(var + LN_EPS) * g + b


def _store_token_tiles(ref, rows):
    m = rows.shape[0]
    for c in range(TOKEN_SUBLANES):
        ref[pl.ds(c, m, stride=TOKEN_SUBLANES), :] = rows[:, c * LANES:(c + 1) * LANES]


def _load_token_tiles(ref, m, first_row=0):
    return jnp.concatenate([ref[pl.ds(first_row + c, m, stride=TOKEN_SUBLANES), :] for c in range(TOKEN_SUBLANES)], axis=1)


def _token_tile(ref, t):
    row = t * TOKEN_SUBLANES
    return ref.at[pl.ds(row if isinstance(row, int) else pl.multiple_of(row, TOKEN_SUBLANES), TOKEN_SUBLANES), :]


def _post_mixer(y, x_ref, g1_ref, sc2_ref, sh2_ref, lng_ref, lnb_ref, wrh_ref, wrl_ref, br_ref, xo_ref, v_ref, lg_ref):
    xn = _layer_norm(DEEPNORM_ALPHA * x_ref[...] + g1_ref[0] * y, lng_ref[...], lnb_ref[...])
    xo_ref[...] = xn
    v = xn * (1.0 + sc2_ref[0]) + sh2_ref[0]
    _store_token_tiles(v_ref, v)
    v_hi = v.astype(BF16)
    v_lo = (v - v_hi.astype(F32)).astype(BF16)
    lg_ref[...] = (jnp.dot(v_hi, wrh_ref[...], preferred_element_type=F32)
                   + (jnp.dot(v_lo, wrh_ref[...], preferred_element_type=F32)
                      + jnp.dot(v_hi, wrl_ref[...], preferred_element_type=F32))) + br_ref[...]


def _outproj_even_kernel(a_ref, f_ref, w_ref, *rest):
    y = (jnp.dot(a_ref[...], w_ref[:ATTN_WIDTH, :], preferred_element_type=F32)
         + jnp.dot(f_ref[...], w_ref[ATTN_WIDTH:, :], preferred_element_type=F32))
    _post_mixer(y, *rest)


def _outproj_odd_kernel(b_ref, z_ref, zp_ref, zn_ref, cw_ref, w_ref, *rest, rows_per_seq):
    tm = z_ref.shape[0]
    halo = zp_ref.shape[0]
    row0 = pl.program_id(0) * tm
    z = z_ref[...].astype(F32)
    prev_row = jnp.where(row0 % rows_per_seq == 0, 0.0, zp_ref[halo - 1:halo, :].astype(F32))
    next_row = jnp.where((row0 + tm) % rows_per_seq == 0, 0.0, zn_ref[0:1, :].astype(F32))
    r = lax.broadcasted_iota(jnp.int32, z.shape, 0)
    z_prev = jnp.where(r == 0, prev_row, pltpu.roll(z, 1, axis=0))
    z_next = jnp.where(r == tm - 1, next_row, pltpu.roll(z, tm - 1, axis=0))
    conv = cw_ref[0:1, :] * z_prev + cw_ref[1:2, :] * z + cw_ref[2:3, :] * z_next
    a = (b_ref[...].astype(F32) * conv).astype(BF16)
    y = jnp.dot(a, w_ref[...], preferred_element_type=F32)
    _post_mixer(y, *rest)


def _outproj(kind, mix_inputs, w_out, x2, g1, sc2, sh2, lng, lnb, wr_hi, wr_lo, br, seq_rows, mod_row, conv_w=None):
    t = x2.shape[0]
    tm = min(ROW_TILE, seq_rows)
    mod = _mod_spec(tm, seq_rows, mod_row)

    row = lambda width: pl.BlockSpec((tm, width), lambda i: (i, 0))
    const = lambda shape: pl.BlockSpec(shape, lambda i: (0,) * len(shape))
    if kind == "even":
        attn, four = mix_inputs
        kern = _outproj_even_kernel
        head_specs = [row(ATTN_WIDTH), row(FOURIER_WIDTH)]
        head_args = [attn, four]
    else:
        (bz,) = mix_inputs
        halo = 16
        last = t // halo - 1
        kern = functools.partial(_outproj_odd_kernel, rows_per_seq=seq_rows)
        head_specs = [
            pl.BlockSpec((tm, D_MODEL), lambda i: (i, 0)),
            pl.BlockSpec((tm, D_MODEL), lambda i: (i, 1)),
            pl.BlockSpec((halo, D_MODEL), lambda i: (jnp.maximum(i * (tm // halo) - 1, 0), 1)),
            pl.BlockSpec((halo, D_MODEL), lambda i: (jnp.minimum((i + 1) * (tm // halo), last), 1)),
            const((8, D_MODEL)),
        ]
        head_args = [bz, bz, bz, bz, conv_w]
    return pl.pallas_call(
        kern,
        grid=(t // tm,),
        in_specs=head_specs + [
            const((D_MODEL, D_MODEL)),
            row(D_MODEL),
            mod,
            mod,
            mod,
            const((1, D_MODEL)), const((1, D_MODEL)),
            const((D_MODEL, LANES)), const((D_MODEL, LANES)), const((1, LANES)),
        ],
        out_specs=[row(D_MODEL), pl.BlockSpec((tm * TOKEN_SUBLANES, LANES), lambda i: (i, 0)), row(LANES)],
        out_shape=[
            jax.ShapeDtypeStruct((t, D_MODEL), F32),
            jax.ShapeDtypeStruct((t * TOKEN_SUBLANES, LANES), F32),
            jax.ShapeDtypeStruct((t, LANES), F32),
        ],
        compiler_params=_params("arbitrary"),
        name="outproj_" + kind,
    )(*head_args, w_out, x2, g1, sc2, sh2, lng, lnb, wr_hi, wr_lo, br)


def _route_kernel(lg_ref, tri_ref, idx_ref, gate_ref, cnt_ref, carry_ref):
    @pl.when(pl.program_id(0) == 0)
    def _():
        carry_ref[...] = jnp.zeros_like(carry_ref)

    work = lg_ref[...]
    lane = lax.broadcasted_iota(jnp.int32, work.shape, 1).astype(F32)
    sels, vals, ids = [], [], []
    for _ in range(TOP_K):
        m = jnp.max(work, axis=1, keepdims=True)
        first = jnp.min(jnp.where(work == m, lane, float(LANES)), axis=1, keepdims=True)
        sel = lane == first
        work = jnp.where(sel, -jnp.inf, work)
        sels.append(sel)
        vals.append(m)
        ids.append(first)
    onehot = jnp.where(sels[0] | sels[1] | sels[2] | sels[3], 1.0, 0.0)
    before = jnp.dot(tri_ref[...], onehot.astype(BF16), preferred_element_type=F32) + carry_ref[...]
    carry_ref[...] = carry_ref[...] + jnp.sum(onehot, axis=0, keepdims=True)
    cnt_ref[...] = carry_ref[...]
    exps = [jnp.exp(v - vals[0]) for v in vals]
    denom = exps[0] + exps[1] + exps[2] + exps[3]
    idx_out = jnp.zeros_like(work)
    gate_out = jnp.zeros_like(work)
    for k in range(TOP_K):
        rank = jnp.sum(jnp.where(sels[k], before, 0.0), axis=1, keepdims=True)
        idx_out = jnp.where(lane == k, ids[k], jnp.where(lane == TOP_K + k, rank, idx_out))
        gate_out = jnp.where(lane == k, exps[k] / denom, gate_out)
    idx_ref[...] = idx_out.astype(jnp.int32)
    gate_ref[...] = gate_out


def _route(logits):
    t = logits.shape[0]
    tm = ROUTE_TILE
    tri = jnp.asarray(np.tril(np.ones((tm, tm), np.float32), -1), BF16)
    row = pl.BlockSpec((tm, LANES), lambda i: (i, 0))
    return pl.pallas_call(
        _route_kernel,
        grid=(t // tm,),
        in_specs=[row, pl.BlockSpec((tm, tm), lambda i: (0, 0))],
        out_specs=[row, row, pl.BlockSpec((1, LANES), lambda i: (0, 0))],
        out_shape=[
            jax.ShapeDtypeStruct((t, LANES), jnp.int32),
            jax.ShapeDtypeStruct((t, LANES), F32),
            jax.ShapeDtypeStruct((1, LANES), F32),
        ],
        scratch_shapes=[pltpu.VMEM((1, LANES), F32)],
        compiler_params=_params("arbitrary"),
        name="route",
    )(logits, tri)


def _zero_fill(fill_start_ref, fill_len_ref, n_used_ref, buf_hbm, zero_vmem, sem_fill):
    n_blk = buf_hbm.shape[0] // (EXPERT_ROWS * TOKEN_SUBLANES)
    zero_vmem[...] = jnp.zeros_like(zero_vmem)
    chunks = [1 << b for b in reversed(range(EXPERT_ROWS.bit_length() - 1))]

    def pads(wait):
        def body(e, carry):
            start, length = fill_start_ref[e], fill_len_ref[e]
            for c in chunks:
                @pl.when((length & c) != 0)
                def _():
                    row = pl.multiple_of((start + (length & (-2 * c))) * TOKEN_SUBLANES, TOKEN_SUBLANES)
                    cp = pltpu.make_async_copy(zero_vmem.at[pl.ds(0, c * TOKEN_SUBLANES), :],
                                               buf_hbm.at[pl.ds(row, c * TOKEN_SUBLANES), :], sem_fill)
                    cp.wait() if wait else cp.start()
            return carry

        lax.fori_loop(0, N_EXPERTS, body, 0)

    def tail(wait):
        def body(j, carry):
            row = pl.multiple_of(j * (EXPERT_ROWS * TOKEN_SUBLANES), EXPERT_ROWS * TOKEN_SUBLANES)
            cp = pltpu.make_async_copy(zero_vmem, buf_hbm.at[pl.ds(row, EXPERT_ROWS * TOKEN_SUBLANES), :], sem_fill)
            cp.wait() if wait else cp.start()
            return carry

        lax.fori_loop(n_used_ref[0], n_blk, body, 0)

    pads(False)
    tail(False)
    pads(True)
    tail(True)


def _dispatch_kernel(fill_start_ref, fill_len_ref, n_used_ref, *rest, tiles):
    v_refs = rest[:len(tiles)]
    pos_hbm, buf_hbm, idx_smem, sem_idx, sem_rows, zero_vmem, sem_fill = rest[len(tiles):]
    tm = v_refs[0].shape[0] // TOKEN_SUBLANES
    per_tile = tm * TOP_K
    i = pl.program_id(0)
    fetch = pltpu.make_async_copy(pos_hbm.at[pl.ds(i * per_tile, per_tile)], idx_smem, sem_idx)
    fetch.start()

    @pl.when(i == 0)
    def _():
        _zero_fill(fill_start_ref, fill_len_ref, n_used_ref, buf_hbm, zero_vmem, sem_fill)

    fetch.wait()

    def scatter(v_ref):
        def row_copy(t, p):
            return pltpu.make_async_copy(_token_tile(v_ref, t), _token_tile(buf_hbm, p), sem_rows)

        def issue(j, carry):
            for u in range(DMA_UNROLL):
                t = j * DMA_UNROLL + u
                for k in range(TOP_K):
                    row_copy(t, idx_smem[t * TOP_K + k]).start(priority=k % 2)
            return carry

        def drain(j, carry):
            for _ in range(DMA_UNROLL * TOP_K):
                row_copy(0, 0).wait()
            return carry

        lax.fori_loop(0, tm // DMA_UNROLL, issue, 0)
        lax.fori_loop(0, tm // DMA_UNROLL, drain, 0)

    first = 0
    for v_ref, n_tiles in zip(v_refs, tiles):
        @pl.when((i >= first) & (i < first + n_tiles))
        def _():
            scatter(v_ref)

        first += n_tiles


def _dispatch(fill_start, fill_len, n_used, v_parts, pos_flat, n_rows):
    tm = DISPATCH_TILE
    tiles = tuple(v.shape[0] // (tm * TOKEN_SUBLANES) for v in v_parts)
    firsts = [sum(tiles[:p]) for p in range(len(tiles))]
    any_spec = pl.BlockSpec(memory_space=pl.ANY)

    def part_spec(first, n_tiles):
        return pl.BlockSpec((tm * TOKEN_SUBLANES, LANES), lambda i, *_: (jnp.clip(i - first, 0, n_tiles - 1), 0))

    grid_spec = pltpu.PrefetchScalarGridSpec(
        num_scalar_prefetch=3,
        grid=(sum(tiles),),
        in_specs=[part_spec(f, n) for f, n in zip(firsts, tiles)] + [any_spec],
        out_specs=any_spec,
        scratch_shapes=[pltpu.SMEM((tm * TOP_K,), jnp.int32), pltpu.SemaphoreType.DMA, pltpu.SemaphoreType.DMA,
                        pltpu.VMEM((EXPERT_ROWS * TOKEN_SUBLANES, LANES), F32), pltpu.SemaphoreType.DMA],
    )
    return pl.pallas_call(
        functools.partial(_dispatch_kernel, tiles=tiles),
        grid_spec=grid_spec,
        out_shape=jax.ShapeDtypeStruct((n_rows * TOKEN_SUBLANES, LANES), F32),
        compiler_params=_params("arbitrary"),
        name="dispatch",
    )(fill_start, fill_len, n_used, *v_parts, pos_flat)


def _pair_shuffle():
    l = np.arange(LANES)
    p = np.zeros((LANES, LANES), np.float32)
    p[l, l // 2 + (LANES // 2) * (l % 2)] = 1.0
    return p


def _expert_kernel(blk_e_ref, n_used_ref, run_ref, next_ref, x_ref, bu_ref, bd_ref, perm_ref, wu_hbm, wd_hbm, o_ref,
                   wu_f32, wd_f32, wu_bf, wd_bf, sem_u, sem_d, *, layer):
    i = pl.program_id(0)
    active = i < n_used_ref[0]
    expert = blk_e_ref[i]
    new_expert = (i == 0) | (expert != blk_e_ref[jnp.maximum(i - 1, 0)])

    def fetch(e, slot):
        return (pltpu.make_async_copy(wu_hbm.at[layer, e], wu_f32.at[slot], sem_u.at[slot]),
                pltpu.make_async_copy(wd_hbm.at[layer, e], wd_f32.at[slot], sem_d.at[slot]))

    @pl.when(active & new_expert)
    def _():
        slot = run_ref[expert] % 2

        @pl.when(i == 0)
        def _():
            for cp in fetch(expert, slot):
                cp.start()

        for cp in fetch(expert, slot):
            cp.wait()
        nxt = next_ref[expert]

        @pl.when(nxt >= 0)
        def _():
            for cp in fetch(nxt, 1 - slot):
                cp.start()

        for c in range(D_MODEL // LANES):
            r = c * LANES
            wu_bf[pl.ds(r, LANES), :] = wu_f32[slot, pl.ds(r, LANES), :].astype(BF16)
            wd_bf[pl.ds(r, LANES), :] = jnp.dot(perm_ref[...], wd_f32[slot, pl.ds(r, LANES), :].astype(BF16),
                                               preferred_element_type=F32).astype(BF16)

    @pl.when(active)
    def _():
        h = jnp.dot(_load_token_tiles(x_ref, EXPERT_ROWS).astype(BF16), wu_bf[...], preferred_element_type=F32) + bu_ref[0]
        even = lax.broadcasted_iota(jnp.int32, (EXPERT_ROWS, LANES), 1) % 2 == 0
        acts = []
        for k in range(D_EXPERT // LANES):
            ha = h[:, 2 * k * LANES:(2 * k + 1) * LANES]
            hb = h[:, (2 * k + 1) * LANES:(2 * k + 2) * LANES]
            glu = jnp.where(even, ha, pltpu.roll(hb, 1, axis=1))
            lin = jnp.where(even, pltpu.roll(ha, LANES - 1, axis=1), hb)
            glu = jnp.minimum(glu, SWIGLU_LIMIT)
            lin = jnp.clip(lin, -SWIGLU_LIMIT, SWIGLU_LIMIT)
            acts.append((glu * jax.nn.sigmoid(SWIGLU_ALPHA * glu) * (lin + 1.0)).astype(BF16))
        a = jnp.concatenate(acts, axis=1)
        _store_token_tiles(o_ref, jnp.dot(a, wd_bf[...], preferred_element_type=F32) + bd_ref[0])

    @pl.when(jnp.logical_not(active))
    def _():
        o_ref[...] = jnp.zeros_like(o_ref)


def _experts(blk_e, n_used, run_idx, next_expert, xbuf, layer, wu, bu, wd, bd):
    n_blk = xbuf.shape[0] // (EXPERT_ROWS * TOKEN_SUBLANES)
    grid_spec = pltpu.PrefetchScalarGridSpec(
        num_scalar_prefetch=4,
        grid=(n_blk,),
        in_specs=[
            pl.BlockSpec((EXPERT_ROWS * TOKEN_SUBLANES, LANES), lambda i, e, *_: (i, 0)),
            pl.BlockSpec((1, 1, 2 * D_EXPERT), lambda i, e, *_: (e[i], 0, 0)),
            pl.BlockSpec((1, 1, D_MODEL), lambda i, e, *_: (e[i], 0, 0)),
            pl.BlockSpec((LANES, LANES), lambda i, e, *_: (0, 0)),
            pl.BlockSpec(memory_space=pl.ANY),
            pl.BlockSpec(memory_space=pl.ANY),
        ],
        out_specs=pl.BlockSpec((EXPERT_ROWS * TOKEN_SUBLANES, LANES), lambda i, e, *_: (i, 0)),
        scratch_shapes=[
            pltpu.VMEM((2, D_MODEL, 2 * D_EXPERT), F32),
            pltpu.VMEM((2, D_EXPERT, D_MODEL), F32),
            pltpu.VMEM((D_MODEL, 2 * D_EXPERT), BF16),
            pltpu.VMEM((D_EXPERT, D_MODEL), BF16),
            pltpu.SemaphoreType.DMA((2,)),
            pltpu.SemaphoreType.DMA((2,)),
        ],
    )
    return pl.pallas_call(
        functools.partial(_expert_kernel, layer=layer),
        grid_spec=grid_spec,
        out_shape=jax.ShapeDtypeStruct(xbuf.shape, F32),
        compiler_params=pltpu.CompilerParams(dimension_semantics=("arbitrary",), vmem_limit_bytes=EXPERT_VMEM_LIMIT),
        name="experts",
    )(blk_e, n_used, run_idx, next_expert, xbuf, bu, bd, jnp.asarray(_pair_shuffle(), BF16), wu, wd)


def _combine_kernel(x_ref, gate_ref, g2_ref, lng_ref, lnb_ref, *rest, nxt):
    n_proj = {None: 0, "odd": 3, "even": 5}[nxt]
    proj_refs, rest = rest[:n_proj], rest[n_proj:]
    pos_hbm, y_hbm, o_ref = rest[:3]
    idx_smem, rows, sem_idx, sem_rows = rest[-4:]
    tm = x_ref.shape[0]
    per_tile = tm * TOP_K
    i = pl.program_id(0)

    def row_copy(slot, t, k, p):
        return pltpu.make_async_copy(_token_tile(y_hbm, p), _token_tile(rows, (slot * TOP_K + k) * tm + t), sem_rows.at[slot])

    def start_gather(tile, slot):
        fetch = pltpu.make_async_copy(pos_hbm.at[pl.ds(tile * per_tile, per_tile)], idx_smem, sem_idx)
        fetch.start()
        fetch.wait()

        def issue(j, carry):
            for u in range(DMA_UNROLL):
                t = j * DMA_UNROLL + u
                for k in range(TOP_K):
                    row_copy(slot, t, k, idx_smem[t * TOP_K + k]).start(priority=k % 2)
            return carry

        lax.fori_loop(0, tm // DMA_UNROLL, issue, 0)

    @pl.when(i == 0)
    def _():
        start_gather(0, 0)

    @pl.when(i + 1 < pl.num_programs(0))
    def _():
        start_gather(i + 1, (i + 1) % 2)

    slot = i % 2

    def drain(j, carry):
        for _ in range(DMA_UNROLL * TOP_K):
            row_copy(slot, 0, 0, 0).wait()
        return carry

    lax.fori_loop(0, tm // DMA_UNROLL, drain, 0)
    chunks = []
    for c in range(TOKEN_SUBLANES):
        def part(k):
            first = pl.multiple_of((slot * TOP_K + k) * (tm * TOKEN_SUBLANES), tm * TOKEN_SUBLANES) + c
            return gate_ref[:, k:k + 1] * rows[pl.ds(first, tm, stride=TOKEN_SUBLANES), :]

        fc = part(0)
        for k in range(1, TOP_K):
            fc = fc + part(k)
        chunks.append(fc)
    f = jnp.concatenate(chunks, axis=1)
    xn = _layer_norm(DEEPNORM_ALPHA * x_ref[...] + g2_ref[0] * f, lng_ref[...], lnb_ref[...])
    o_ref[...] = xn
    if nxt == "odd":
        _inproj_odd_body(xn, *proj_refs, rest[3])
    elif nxt == "even":
        _inproj_even_body(xn, *proj_refs, rest[3], True)


def _combine(x2, gates, pos_flat, ybuf, g2, lng, lnb, seq_rows, mod_row, nxt=None, proj=()):
    t = x2.shape[0]
    tm = COMBINE_TILE
    row = pl.BlockSpec((tm, D_MODEL), lambda i: (i, 0))
    const = pl.BlockSpec((1, D_MODEL), lambda i: (0, 0))
    out_specs, out_shape, proj_specs = [row], [jax.ShapeDtypeStruct((t, D_MODEL), F32)], []
    if nxt is not None:
        width = 2 * D_MODEL if nxt == "odd" else EVEN_IN_WIDTH
        out_specs.append(pl.BlockSpec((tm, width), lambda i: (i, 0)))
        out_shape.append(jax.ShapeDtypeStruct((t, width), BF16))
        mod = _mod_spec(tm, seq_rows, mod_row)
        proj_specs = [mod, mod, pl.BlockSpec(proj[2].shape, lambda i: (0, 0))]
        if nxt == "even":
            pos_tiles = proj[3].shape[0] // tm
            proj_specs += [pl.BlockSpec((tm, LANES), lambda i: (i % pos_tiles, 0))] * 2
    out = pl.pallas_call(
        functools.partial(_combine_kernel, nxt=nxt),
        grid=(t // tm,),
        in_specs=[row, pl.BlockSpec((tm, LANES), lambda i: (i, 0)), _mod_spec(tm, seq_rows, mod_row), const, const,
                  *proj_specs, pl.BlockSpec(memory_space=pl.ANY), pl.BlockSpec(memory_space=pl.ANY)],
        out_specs=out_specs,
        out_shape=out_shape,
        scratch_shapes=[
            pltpu.SMEM((tm * TOP_K,), jnp.int32),
            pltpu.VMEM((2 * TOP_K * tm * TOKEN_SUBLANES, LANES), F32),
            pltpu.SemaphoreType.DMA,
            pltpu.SemaphoreType.DMA((2,)),
        ],
        compiler_params=pltpu.CompilerParams(dimension_semantics=("arbitrary",),
                                             vmem_limit_bytes=VMEM_LIMIT if nxt is None else EXPERT_VMEM_LIMIT),
        name="combine" if nxt is None else "combine_inproj_" + nxt,
    )(x2, gates, g2, lng, lnb, *proj, pos_flat, ybuf)
    return out[0] if nxt is None else out


def _moe(v_parts, logits, layer, wu, bu, wd, bd):
    t = logits.shape[0]
    n = t * TOP_K
    idx, gates, cnt = _route(logits)
    counts = cnt[0, :N_EXPERTS].astype(jnp.int32)
    padded = (counts + EXPERT_ROWS - 1) // EXPERT_ROWS * EXPERT_ROWS
    pad_ends = jnp.cumsum(padded)
    pad_starts = pad_ends - padded
    experts = jnp.arange(N_EXPERTS, dtype=jnp.int32)
    start = jnp.sum(jnp.where(idx[:, :TOP_K, None] == experts, pad_starts, 0), axis=-1)
    pos_flat = (start + idx[:, TOP_K:2 * TOP_K]).reshape(n)
    n_blk = n // EXPERT_ROWS + N_EXPERTS
    blk_start = jnp.arange(n_blk, dtype=jnp.int32) * EXPERT_ROWS
    blk_e = jnp.minimum(jnp.sum((pad_ends[None, :] <= blk_start[:, None]).astype(jnp.int32), axis=1), N_EXPERTS - 1)
    n_used = pad_ends[-1:] // EXPERT_ROWS
    nonempty = counts > 0
    run_idx = jnp.cumsum(nonempty.astype(jnp.int32)) - 1
    later = nonempty[None, :] & (experts[None, :] > experts[:, None])
    next_expert = jnp.min(jnp.where(later, experts[None, :], N_EXPERTS), axis=1)
    next_expert = jnp.where(next_expert == N_EXPERTS, -1, next_expert)
    fill_start, fill_len = pad_starts + counts, padded - counts
    xbuf = _dispatch(fill_start, fill_len, n_used, v_parts, pos_flat, n_blk * EXPERT_ROWS)
    ybuf = _experts(blk_e, n_used, run_idx, next_expert, xbuf, layer, wu, bu, wd, bd)
    return ybuf, pos_flat, gates


def _rope_tables(seq):
    rows = seq // GRID_W
    row = jnp.repeat(jnp.arange(rows), GRID_W).astype(F32)
    col = jnp.tile(jnp.arange(GRID_W), rows).astype(F32)
    n_freq = HEAD_DIM // 4
    inv = ROPE_BASE ** (-jnp.arange(n_freq, dtype=F32) / n_freq)
    ang = jnp.concatenate([row[:, None] * inv, col[:, None] * inv], -1)
    ang = jnp.concatenate([ang, ang], -1)
    sign = jnp.where(jnp.arange(HEAD_DIM) < HEAD_DIM // 2, -1.0, 1.0).astype(F32)
    cos = jnp.cos(ang)
    sin = jnp.sin(ang) * sign
    return jnp.tile(cos, (1, LANES // HEAD_DIM)), jnp.tile(sin, (1, LANES // HEAD_DIM))


def kernel(x, c, ctx, c_ctx, w_mod, b_mod, w_in_even, sink, w_out_even, w_in_odd, conv_w, w_out_odd, ln_g, ln_b,
           w_router, b_router, w_up, b_up, w_down, b_down):
    B, S, D = x.shape
    L = ctx.shape[1]
    T, TC = B * S, B * L
    cos, sin = _rope_tables(S)
    cond = jnp.concatenate([c, c_ctx[None, :], jnp.zeros((MOD_ROWS - B - 1, D), F32)], 0)
    mod = _modulation(cond, w_mod, b_mod)

    bu = b_up.reshape(DEPTH, N_EXPERTS, 1, 2 * D_EXPERT)
    bd = b_down.reshape(DEPTH, N_EXPERTS, 1, D)
    wr_hi = w_router.astype(BF16)
    wr_lo = (w_router - wr_hi.astype(F32)).astype(BF16)
    pad_r = ((0, 0), (0, 0), (0, LANES - N_EXPERTS))
    wr_hi, wr_lo = jnp.pad(wr_hi, pad_r), jnp.pad(wr_lo, pad_r)
    br = jnp.pad(b_router, ((0, 0), (0, LANES - N_EXPERTS)), constant_values=NEG_INF).reshape(DEPTH, 1, LANES)

    x2 = x.reshape(T, D)
    xc2 = ctx.reshape(TC, D)
    pre = None
    for l in range(DEPTH):
        even = l % 2 == 0
        j = l // 2
        ctx_after = any(m % 2 == 0 for m in range(l + 1, DEPTH))
        sh1, sc1, g1, sh2, sc2, g2 = [m.reshape(MOD_ROWS, 1, D) for m in jnp.split(mod[l], 6, axis=-1)]
        lng1, lnb1 = ln_g[l, 0].reshape(1, D), ln_b[l, 0].reshape(1, D)
        lng2, lnb2 = ln_g[l, 1].reshape(1, D), ln_b[l, 1].reshape(1, D)
        post = (g1, sc2, sh2, lng1, lnb1, wr_hi[l], wr_lo[l], br[l])
        if even:
            w_in = w_in_even[j].astype(BF16)
            w_out = w_out_even[j].astype(BF16)
            h = pre if pre is not None else _inproj_even(x2, sc1, sh1, w_in, cos, sin, S, None, True)
            hc = _inproj_even(xc2, sc1, sh1, w_in, cos, sin, L, B, False)
            kcol = ATTN_WIDTH // KV_WIDTH
            attn = _win_attention(h, hc, sink[j], B, S, kcol, kcol + 1)
            four = _fourier_mix(h, B, S)
            x2, v, logits = _outproj("even", (attn, four), w_out, x2, *post, S, None)
            if ctx_after:
                attn_c = _ctx_attention(hc, sink[j], B)
                four_c = _fourier_mix(hc, B, L)
                xc2, vc, logits_c = _outproj("even", (attn_c, four_c), w_out, xc2, *post, L, B)
        else:
            w_in = w_in_odd[j].astype(BF16)
            w_out = w_out_odd[j].astype(BF16)
            cw = jnp.pad(conv_w[j], ((0, 8 - conv_w.shape[1]), (0, 0)))
            bz = pre if pre is not None else _inproj_odd(x2, sc1, sh1, w_in, S, None)
            x2, v, logits = _outproj("odd", (bz,), w_out, x2, *post, S, None, conv_w=cw)
            if ctx_after:
                bzc = _inproj_odd(xc2, sc1, sh1, w_in, L, B)
                xc2, vc, logits_c = _outproj("odd", (bzc,), w_out, xc2, *post, L, B, conv_w=cw)
        if ctx_after:
            ybuf, pos, gates = _moe((v, vc), jnp.concatenate([logits, logits_c], 0), l, w_up, bu[l], w_down, bd[l])
            xc2 = _combine(xc2, gates[T:], pos[T * TOP_K:], ybuf, g2, lng2, lnb2, L, B)
        else:
            ybuf, pos, gates = _moe((v,), logits, l, w_up, bu[l], w_down, bd[l])
        if l + 1 < DEPTH:
            sh1n, sc1n = [m.reshape(MOD_ROWS, 1, D) for m in jnp.split(mod[l + 1], 6, axis=-1)[:2]]
            if (l + 1) % 2 == 0:
                nxt, proj = "even", (sc1n, sh1n, w_in_even[(l + 1) // 2].astype(BF16), cos, sin)
            else:
                nxt, proj = "odd", (sc1n, sh1n, w_in_odd[(l + 1) // 2].astype(BF16))
            x2, pre = _combine(x2, gates[:T], pos[:T * TOP_K], ybuf, g2, lng2, lnb2, S, None, nxt, proj)
        else:
            x2 = _combine(x2, gates[:T], pos[:T * TOP_K], ybuf, g2, lng2, lnb2, S, None)
    return x2.reshape(B, S, D)
```

```python
import functools

import numpy as np
import jax
import jax.numpy as jnp
from jax import lax
from jax.experimental import pallas as pl
from jax.experimental.pallas import tpu as pltpu

D_MODEL = 1024
DEPTH = 4
GRID_W = 64
HEAD_DIM = 64
N_Q_HEADS = 12
N_KV_HEADS = 4
Q_PER_KV = N_Q_HEADS // N_KV_HEADS
BLOCK = 128
ROPE_BASE = 10000.0
FOURIER_GROUP_DIM = 64
FOURIER_WIDTH = 256
ATTN_WIDTH = N_Q_HEADS * HEAD_DIM
KV_WIDTH = N_KV_HEADS * HEAD_DIM
EVEN_IN_WIDTH = ATTN_WIDTH + 2 * KV_WIDTH + FOURIER_WIDTH
N_EXPERTS = 32
TOP_K = 4
D_EXPERT = D_MODEL
SWIGLU_LIMIT = 7.0
SWIGLU_ALPHA = 1.702
LN_EPS = 1e-5
NEG_INF = -1e30
DEEPNORM_ALPHA = (2 * DEPTH) ** 0.25

LANES = 128
SUBLANES = 8
TOKEN_SUBLANES = D_MODEL // LANES
ROW_TILE = 1024
EXPERT_ROWS = 512
MOD_ROWS = 16
ROUTE_TILE = 512
DISPATCH_TILE = 1024
COMBINE_TILE = 512
DMA_UNROLL = 8
VMEM_LIMIT = 48 * 1024 * 1024
EXPERT_VMEM_LIMIT = 56 * 1024 * 1024

F32 = jnp.float32
BF16 = jnp.bfloat16


def _params(*sem):
    return pltpu.CompilerParams(dimension_semantics=sem, vmem_limit_bytes=VMEM_LIMIT)


def _mod_spec(tm, seq_rows, mod_row):
    if mod_row is None:
        return pl.BlockSpec((1, 1, D_MODEL), lambda i: ((i * tm) // seq_rows, 0, 0))
    return pl.BlockSpec((1, 1, D_MODEL), lambda i: (mod_row, 0, 0))


def _mod_kernel(c_ref, w_ref, b_ref, o_ref):
    c = c_ref[...]
    s = (c * jax.nn.sigmoid(c)).astype(BF16)
    o_ref[0] = jnp.dot(s, w_ref[0].astype(BF16), preferred_element_type=F32) + b_ref[0]


def _modulation(cond, w_mod, b_mod):
    tn = 1536
    n = w_mod.shape[-1]
    return pl.pallas_call(
        _mod_kernel,
        grid=(DEPTH, n // tn),
        in_specs=[
            pl.BlockSpec((MOD_ROWS, D_MODEL), lambda l, j: (0, 0)),
            pl.BlockSpec((1, D_MODEL, tn), lambda l, j: (l, 0, j)),
            pl.BlockSpec((1, 1, tn), lambda l, j: (l, 0, j)),
        ],
        out_specs=pl.BlockSpec((1, MOD_ROWS, tn), lambda l, j: (l, 0, j)),
        out_shape=jax.ShapeDtypeStruct((DEPTH, MOD_ROWS, n), F32),
        compiler_params=_params("arbitrary", "arbitrary"),
        name="modulation",
    )(cond, w_mod, b_mod.reshape(DEPTH, 1, n))


def _rope_chunk(h, cos, sin_signed, first_half):
    swapped = jnp.where(first_half, pltpu.roll(h, LANES - HEAD_DIM // 2, axis=1), pltpu.roll(h, HEAD_DIM // 2, axis=1))
    return h * cos + swapped * sin_signed


def _inproj_even_kernel(x_ref, sc_ref, sh_ref, w_ref, cos_ref, sin_ref, o_ref, *, rope):
    u = (x_ref[...] * (1.0 + sc_ref[0]) + sh_ref[0]).astype(BF16)
    tn = 512
    if rope:
        cos = cos_ref[...]
        sin = sin_ref[...]
        lane = lax.broadcasted_iota(jnp.int32, cos.shape, 1)
        first_half = (lane % HEAD_DIM) < HEAD_DIM // 2
    scale = HEAD_DIM ** -0.5
    for jt in range(EVEN_IN_WIDTH // tn):
        acc = jnp.dot(u, w_ref[:, jt * tn:(jt + 1) * tn], preferred_element_type=F32)
        for k in range(tn // LANES):
            col = jt * tn + k * LANES
            h = acc[:, k * LANES:(k + 1) * LANES]
            if rope and col < ATTN_WIDTH + KV_WIDTH:
                h = _rope_chunk(h, cos, sin, first_half)
            if col < ATTN_WIDTH:
                h = h * scale
            o_ref[:, col:col + LANES] = h.astype(BF16)


def _inproj_even(x2, sc, sh, w, cos, sin, seq_rows, mod_row, rope):
    t = x2.shape[0]
    tm = min(ROW_TILE, seq_rows)
    pos_tiles = cos.shape[0] // tm
    mod = _mod_spec(tm, seq_rows, mod_row)

    return pl.pallas_call(
        functools.partial(_inproj_even_kernel, rope=rope),
        grid=(t // tm,),
        in_specs=[
            pl.BlockSpec((tm, D_MODEL), lambda i: (i, 0)),
            mod,
            mod,
            pl.BlockSpec((D_MODEL, EVEN_IN_WIDTH), lambda i: (0, 0)),
            pl.BlockSpec((tm, LANES), lambda i: (i % pos_tiles, 0)),
            pl.BlockSpec((tm, LANES), lambda i: (i % pos_tiles, 0)),
        ],
        out_specs=pl.BlockSpec((tm, EVEN_IN_WIDTH), lambda i: (i, 0)),
        out_shape=jax.ShapeDtypeStruct((t, EVEN_IN_WIDTH), BF16),
        compiler_params=_params("arbitrary"),
        name="inproj_even",
    )(x2, sc, sh, w, cos, sin)


def _attend(q_ref, k_all, v_all, valid, sink_ref, o_ref):
    rows = q_ref.shape[0]
    lane = lax.broadcasted_iota(jnp.int32, (rows, LANES), 1)
    low = lane < HEAD_DIM
    scores, sks = [], []
    for kv in range(N_KV_HEADS):
        kv_chunk, kv_half = divmod(kv, 2)
        keep = low if kv_half == 0 else jnp.logical_not(low)
        stack, sinks = [], []
        for g in range(Q_PER_KV):
            head = kv * Q_PER_KV + g
            chunk, half = divmod(head, 2)
            qh = q_ref[:, chunk * LANES:(chunk + 1) * LANES].astype(F32)
            if half != kv_half:
                qh = pltpu.roll(qh, HEAD_DIM, axis=1)
            stack.append(jnp.where(keep, qh, 0.0).astype(BF16))
            sinks.append(jnp.full((rows, 1), sink_ref[head], F32))
        qs = jnp.concatenate(stack, axis=0)
        sks.append(jnp.concatenate(sinks, axis=0))
        kc = k_all[:, kv_chunk * LANES:(kv_chunk + 1) * LANES]
        s = lax.dot_general(qs, kc, (((1,), (1,)), ((), ())), preferred_element_type=F32)
        if valid is not None:
            s = jnp.concatenate([s[:, j * BLOCK:(j + 1) * BLOCK] if ok is None
                                 else jnp.where(ok, s[:, j * BLOCK:(j + 1) * BLOCK], NEG_INF)
                                 for j, ok in enumerate(valid)], axis=1)
        scores.append(s)
    probs, sink_terms = [], []
    for kv in range(N_KV_HEADS):
        m = jnp.maximum(sks[kv], jnp.max(scores[kv], axis=-1, keepdims=True))
        probs.append(jnp.exp(scores[kv] - m).astype(BF16))
        sink_terms.append(jnp.exp(sks[kv] - m))
    key_lane = lax.broadcasted_iota(jnp.int32, (v_all.shape[0], LANES), 1)
    for kv in range(N_KV_HEADS):
        kv_chunk, kv_half = divmod(kv, 2)
        vc = v_all[:, kv_chunk * LANES:(kv_chunk + 1) * LANES]
        own = (key_lane < HEAD_DIM) if kv_half == 0 else (key_lane >= HEAD_DIM)
        o = jnp.dot(probs[kv], jnp.where(own, vc, jnp.ones_like(vc)), preferred_element_type=F32)
        other = (1 - kv_half) * HEAD_DIM
        o = o / (o[:, other:other + 1] + sink_terms[kv])
        for g in range(Q_PER_KV):
            head = kv * Q_PER_KV + g
            half = head % 2
            oh = o[g * rows:(g + 1) * rows]
            if half != kv_half:
                oh = pltpu.roll(oh, HEAD_DIM, axis=1)
            o_ref[:, head * HEAD_DIM:(head + 1) * HEAD_DIM] = oh[:, half * HEAD_DIM:(half + 1) * HEAD_DIM].astype(BF16)


def _win_attn_kernel(sink_ref, q_ref, kp_ref, kc_ref, kn_ref, vp_ref, vc_ref, vn_ref, kx_ref, vx_ref, o_ref, *, nb):
    n = pl.program_id(1)
    k_all = jnp.concatenate([kp_ref[...], kc_ref[...], kn_ref[...], kx_ref[...]], axis=0)
    v_all = jnp.concatenate([vp_ref[...], vc_ref[...], vn_ref[...], vx_ref[...]], axis=0)
    r = lax.broadcasted_iota(jnp.int32, (Q_PER_KV * BLOCK, BLOCK), 0) % BLOCK
    c = lax.broadcasted_iota(jnp.int32, (Q_PER_KV * BLOCK, BLOCK), 1)
    in_prev = c >= jnp.where(n > 0, r, BLOCK)
    in_next = c <= jnp.where(n < nb - 1, r, -1)
    valid = [in_prev, None, in_next] + [None] * (kx_ref.shape[0] // BLOCK)
    _attend(q_ref, k_all, v_all, valid, sink_ref, o_ref)


def _win_attention(h, hc, sink, batch, seq, kc_col, vc_col):
    nb = seq // BLOCK
    qcol, kcol, vcol = 0, ATTN_WIDTH // KV_WIDTH, ATTN_WIDTH // KV_WIDTH + 1

    def blk(shift, col):
        return pl.BlockSpec((BLOCK, KV_WIDTH), lambda b, n: (b * nb + jnp.clip(n + shift, 0, nb - 1), col))

    return pl.pallas_call(
        functools.partial(_win_attn_kernel, nb=nb),
        grid=(batch, nb),
        in_specs=[
            pl.BlockSpec(memory_space=pltpu.SMEM),
            pl.BlockSpec((BLOCK, ATTN_WIDTH), lambda b, n: (b * nb + n, qcol)),
            blk(-1, kcol), blk(0, kcol), blk(1, kcol),
            blk(-1, vcol), blk(0, vcol), blk(1, vcol),
            pl.BlockSpec((hc.shape[0] // batch, KV_WIDTH), lambda b, n: (b, kc_col)),
            pl.BlockSpec((hc.shape[0] // batch, KV_WIDTH), lambda b, n: (b, vc_col)),
        ],
        out_specs=pl.BlockSpec((BLOCK, ATTN_WIDTH), lambda b, n: (b * nb + n, 0)),
        out_shape=jax.ShapeDtypeStruct((batch * seq, ATTN_WIDTH), BF16),
        compiler_params=_params("arbitrary", "arbitrary"),
        name="window_attention",
    )(sink, h, h, h, h, h, h, h, hc, hc)


def _ctx_attn_kernel(sink_ref, q_ref, k_ref, v_ref, o_ref):
    _attend(q_ref, k_ref[...], v_ref[...], None, sink_ref, o_ref)


def _ctx_attention(hc, sink, batch):
    ctx_len = hc.shape[0] // batch
    kcol, vcol = ATTN_WIDTH // KV_WIDTH, ATTN_WIDTH // KV_WIDTH + 1
    return pl.pallas_call(
        _ctx_attn_kernel,
        grid=(batch,),
        in_specs=[
            pl.BlockSpec(memory_space=pltpu.SMEM),
            pl.BlockSpec((ctx_len, ATTN_WIDTH), lambda b: (b, 0)),
            pl.BlockSpec((ctx_len, KV_WIDTH), lambda b: (b, kcol)),
            pl.BlockSpec((ctx_len, KV_WIDTH), lambda b: (b, vcol)),
        ],
        out_specs=pl.BlockSpec((ctx_len, ATTN_WIDTH), lambda b: (b, 0)),
        out_shape=jax.ShapeDtypeStruct((hc.shape[0], ATTN_WIDTH), BF16),
        compiler_params=_params("arbitrary"),
        name="context_attention",
    )(sink, hc, hc, hc)


def _dft_mats(n):
    k = np.arange(n, dtype=np.int64)
    ang = 2.0 * np.pi * ((k[:, None] * k[None, :]) % n).astype(np.float64) / n
    return np.cos(ang), np.sin(ang)


def _group_dft():
    c, s = _dft_mats(FOURIER_GROUP_DIM)
    groups = FOURIER_WIDTH // FOURIER_GROUP_DIM
    eye = np.eye(groups)
    return np.concatenate([np.kron(eye, c), np.kron(eye, s)], axis=1)


def _fourier_kernel(f_ref, w1_ref, c_ref, s_ref, o_ref, ab_ref, *, scale):
    b = pl.program_id(1)

    @pl.when(pl.program_id(0) == 0)
    def _():
        ab_ref[b] = jnp.dot(f_ref[...], w1_ref[...], preferred_element_type=F32).astype(BF16)

    y = (jnp.dot(c_ref[...], ab_ref[b, :, :FOURIER_WIDTH], preferred_element_type=F32)
         - jnp.dot(s_ref[...], ab_ref[b, :, FOURIER_WIDTH:], preferred_element_type=F32))
    o_ref[...] = (y * scale).astype(BF16)


def _fourier_mix(h, batch, n):
    tr = 256
    cn, sn = _dft_mats(n)
    fcol = (EVEN_IN_WIDTH - FOURIER_WIDTH) // FOURIER_WIDTH
    return pl.pallas_call(
        functools.partial(_fourier_kernel, scale=float((n * FOURIER_GROUP_DIM) ** -0.5)),
        grid=(n // tr, batch),
        in_specs=[
            pl.BlockSpec((n, FOURIER_WIDTH), lambda i, b: (jnp.where(i == 0, b, batch - 1), fcol)),
            pl.BlockSpec((FOURIER_WIDTH, 2 * FOURIER_WIDTH), lambda i, b: (0, 0)),
            pl.BlockSpec((tr, n), lambda i, b: (i, 0)),
            pl.BlockSpec((tr, n), lambda i, b: (i, 0)),
        ],
        out_specs=pl.BlockSpec((tr, FOURIER_WIDTH), lambda i, b: (b * (n // tr) + i, 0)),
        out_shape=jax.ShapeDtypeStruct((batch * n, FOURIER_WIDTH), BF16),
        scratch_shapes=[pltpu.VMEM((batch, n, 2 * FOURIER_WIDTH), BF16)],
        compiler_params=_params("arbitrary", "arbitrary"),
        name="fourier_mix",
    )(h, jnp.asarray(_group_dft(), BF16), jnp.asarray(cn, BF16), jnp.asarray(sn, BF16))


def _inproj_odd_kernel(x_ref, sc_ref, sh_ref, w_ref, o_ref):
    u = (x_ref[...] * (1.0 + sc_ref[0]) + sh_ref[0]).astype(BF16)
    tn = 512
    for jt in range(D_MODEL // tn):
        cols = slice(jt * tn, (jt + 1) * tn)
        b_gate = jnp.dot(u, w_ref[:, jt * tn:(jt + 1) * tn], preferred_element_type=F32)
        o_ref[:, cols] = b_gate.astype(BF16)
        c_gate = jnp.dot(u, w_ref[:, D_MODEL + jt * tn:D_MODEL + (jt + 1) * tn], preferred_element_type=F32)
        hh = jnp.dot(u, w_ref[:, 2 * D_MODEL + jt * tn:2 * D_MODEL + (jt + 1) * tn], preferred_element_type=F32)
        o_ref[:, D_MODEL + jt * tn:D_MODEL + (jt + 1) * tn] = (c_gate * hh).astype(BF16)


def _inproj_odd(x2, sc, sh, w, seq_rows, mod_row):
    t = x2.shape[0]
    tm = min(ROW_TILE, seq_rows)
    mod = _mod_spec(tm, seq_rows, mod_row)

    return pl.pallas_call(
        _inproj_odd_kernel,
        grid=(t // tm,),
        in_specs=[
            pl.BlockSpec((tm, D_MODEL), lambda i: (i, 0)),
            mod,
            mod,
            pl.BlockSpec((D_MODEL, 3 * D_MODEL), lambda i: (0, 0)),
        ],
        out_specs=pl.BlockSpec((tm, 2 * D_MODEL), lambda i: (i, 0)),
        out_shape=jax.ShapeDtypeStruct((t, 2 * D_MODEL), BF16),
        compiler_params=_params("arbitrary"),
        name="inproj_odd",
    )(x2, sc, sh, w)


def _layer_norm(r, g, b):
    mu = jnp.mean(r, axis=-1, keepdims=True)
    d = r - mu
    var = jnp.mean(d * d, axis=-1, keepdims=True)
    return d * lax.rsqrt(var + LN_EPS) * g + b


def _store_token_tiles(ref, rows):
    m = rows.shape[0]
    for c in range(TOKEN_SUBLANES):
        ref[pl.ds(c, m, stride=TOKEN_SUBLANES), :] = rows[:, c * LANES:(c + 1) * LANES]


def _load_token_tiles(ref, m, first_row=0):
    return jnp.concatenate([ref[pl.ds(first_row + c, m, stride=TOKEN_SUBLANES), :] for c in range(TOKEN_SUBLANES)], axis=1)


def _token_tile(ref, t):
    row = t * TOKEN_SUBLANES
    return ref.at[pl.ds(row if isinstance(row, int) else pl.multiple_of(row, TOKEN_SUBLANES), TOKEN_SUBLANES), :]


def _post_mixer(y, x_ref, g1_ref, sc2_ref, sh2_ref, lng_ref, lnb_ref, wrh_ref, wrl_ref, br_ref, xo_ref, v_ref, lg_ref):
    xn = _layer_norm(DEEPNORM_ALPHA * x_ref[...] + g1_ref[0] * y, lng_ref[...], lnb_ref[...])
    xo_ref[...] = xn
    v = xn * (1.0 + sc2_ref[0]) + sh2_ref[0]
    _store_token_tiles(v_ref, v)
    v_hi = v.astype(BF16)
    v_lo = (v - v_hi.astype(F32)).astype(BF16)
    lg_ref[...] = (jnp.dot(v_hi, wrh_ref[...], preferred_element_type=F32)
                   + (jnp.dot(v_lo, wrh_ref[...], preferred_element_type=F32)
                      + jnp.dot(v_hi, wrl_ref[...], preferred_element_type=F32))) + br_ref[...]


def _outproj_even_kernel(a_ref, f_ref, w_ref, *rest):
    y = (jnp.dot(a_ref[...], w_ref[:ATTN_WIDTH, :], preferred_element_type=F32)
         + jnp.dot(f_ref[...], w_ref[ATTN_WIDTH:, :], preferred_element_type=F32))
    _post_mixer(y, *rest)


def _outproj_odd_kernel(b_ref, z_ref, zp_ref, zn_ref, cw_ref, w_ref, *rest, rows_per_seq):
    tm = z_ref.shape[0]
    halo = zp_ref.shape[0]
    row0 = pl.program_id(0) * tm
    z = z_ref[...].astype(F32)
    prev_row = jnp.where(row0 % rows_per_seq == 0, 0.0, zp_ref[halo - 1:halo, :].astype(F32))
    next_row = jnp.where((row0 + tm) % rows_per_seq == 0, 0.0, zn_ref[0:1, :].astype(F32))
    r = lax.broadcasted_iota(jnp.int32, z.shape, 0)
    z_prev = jnp.where(r == 0, prev_row, pltpu.roll(z, 1, axis=0))
    z_next = jnp.where(r == tm - 1, next_row, pltpu.roll(z, tm - 1, axis=0))
    conv = cw_ref[0:1, :] * z_prev + cw_ref[1:2, :] * z + cw_ref[2:3, :] * z_next
    a = (b_ref[...].astype(F32) * conv).astype(BF16)
    y = jnp.dot(a, w_ref[...], preferred_element_type=F32)
    _post_mixer(y, *rest)


def _outproj(kind, mix_inputs, w_out, x2, g1, sc2, sh2, lng, lnb, wr_hi, wr_lo, br, seq_rows, mod_row, conv_w=None):
    t = x2.shape[0]
    tm = min(ROW_TILE, seq_rows)
    mod = _mod_spec(tm, seq_rows, mod_row)

    row = lambda width: pl.BlockSpec((tm, width), lambda i: (i, 0))
    const = lambda shape: pl.BlockSpec(shape, lambda i: (0,) * len(shape))
    if kind == "even":
        attn, four = mix_inputs
        kern = _outproj_even_kernel
        head_specs = [row(ATTN_WIDTH), row(FOURIER_WIDTH)]
        head_args = [attn, four]
    else:
        (bz,) = mix_inputs
        halo = 16
        last = t // halo - 1
        kern = functools.partial(_outproj_odd_kernel, rows_per_seq=seq_rows)
        head_specs = [
            pl.BlockSpec((tm, D_MODEL), lambda i: (i, 0)),
            pl.BlockSpec((tm, D_MODEL), lambda i: (i, 1)),
            pl.BlockSpec((halo, D_MODEL), lambda i: (jnp.maximum(i * (tm // halo) - 1, 0), 1)),
            pl.BlockSpec((halo, D_MODEL), lambda i: (jnp.minimum((i + 1) * (tm // halo), last), 1)),
            const((8, D_MODEL)),
        ]
        head_args = [bz, bz, bz, bz, conv_w]
    return pl.pallas_call(
        kern,
        grid=(t // tm,),
        in_specs=head_specs + [
            const((D_MODEL, D_MODEL)),
            row(D_MODEL),
            mod,
            mod,
            mod,
            const((1, D_MODEL)), const((1, D_MODEL)),
            const((D_MODEL, LANES)), const((D_MODEL, LANES)), const((1, LANES)),
        ],
        out_specs=[row(D_MODEL), pl.BlockSpec((tm * TOKEN_SUBLANES, LANES), lambda i: (i, 0)), row(LANES)],
        out_shape=[
            jax.ShapeDtypeStruct((t, D_MODEL), F32),
            jax.ShapeDtypeStruct((t * TOKEN_SUBLANES, LANES), F32),
            jax.ShapeDtypeStruct((t, LANES), F32),
        ],
        compiler_params=_params("arbitrary"),
        name="outproj_" + kind,
    )(*head_args, w_out, x2, g1, sc2, sh2, lng, lnb, wr_hi, wr_lo, br)


def _route_kernel(lg_ref, tri_ref, idx_ref, gate_ref, cnt_ref, carry_ref):
    @pl.when(pl.program_id(0) == 0)
    def _():
        carry_ref[...] = jnp.zeros_like(carry_ref)

    work = lg_ref[...]
    lane = lax.broadcasted_iota(jnp.int32, work.shape, 1).astype(F32)
    sels, vals, ids = [], [], []
    for _ in range(TOP_K):
        m = jnp.max(work, axis=1, keepdims=True)
        first = jnp.min(jnp.where(work == m, lane, float(LANES)), axis=1, keepdims=True)
        sel = lane == first
        work = jnp.where(sel, -jnp.inf, work)
        sels.append(sel)
        vals.append(m)
        ids.append(first)
    onehot = jnp.where(sels[0] | sels[1] | sels[2] | sels[3], 1.0, 0.0)
    before = jnp.dot(tri_ref[...], onehot.astype(BF16), preferred_element_type=F32) + carry_ref[...]
    carry_ref[...] = carry_ref[...] + jnp.sum(onehot, axis=0, keepdims=True)
    cnt_ref[...] = carry_ref[...]
    exps = [jnp.exp(v - vals[0]) for v in vals]
    denom = exps[0] + exps[1] + exps[2] + exps[3]
    idx_out = jnp.zeros_like(work)
    gate_out = jnp.zeros_like(work)
    for k in range(TOP_K):
        rank = jnp.sum(jnp.where(sels[k], before, 0.0), axis=1, keepdims=True)
        idx_out = jnp.where(lane == k, ids[k], jnp.where(lane == TOP_K + k, rank, idx_out))
        gate_out = jnp.where(lane == k, exps[k] / denom, gate_out)
    idx_ref[...] = idx_out.astype(jnp.int32)
    gate_ref[...] = gate_out


def _route(logits):
    t = logits.shape[0]
    tm = ROUTE_TILE
    tri = jnp.asarray(np.tril(np.ones((tm, tm), np.float32), -1), BF16)
    row = pl.BlockSpec((tm, LANES), lambda i: (i, 0))
    return pl.pallas_call(
        _route_kernel,
        grid=(t // tm,),
        in_specs=[row, pl.BlockSpec((tm, tm), lambda i: (0, 0))],
        out_specs=[row, row, pl.BlockSpec((1, LANES), lambda i: (0, 0))],
        out_shape=[
            jax.ShapeDtypeStruct((t, LANES), jnp.int32),
            jax.ShapeDtypeStruct((t, LANES), F32),
            jax.ShapeDtypeStruct((1, LANES), F32),
        ],
        scratch_shapes=[pltpu.VMEM((1, LANES), F32)],
        compiler_params=_params("arbitrary"),
        name="route",
    )(logits, tri)


def _zero_fill(fill_start_ref, fill_len_ref, n_used_ref, buf_hbm, zero_vmem, sem_fill):
    n_blk = buf_hbm.shape[0] // (EXPERT_ROWS * TOKEN_SUBLANES)
    zero_vmem[...] = jnp.zeros_like(zero_vmem)
    chunks = [1 << b for b in reversed(range(EXPERT_ROWS.bit_length() - 1))]

    def pads(wait):
        def body(e, carry):
            start, length = fill_start_ref[e], fill_len_ref[e]
            for c in chunks:
                @pl.when((length & c) != 0)
                def _():
                    row = pl.multiple_of((start + (length & (-2 * c))) * TOKEN_SUBLANES, TOKEN_SUBLANES)
                    cp = pltpu.make_async_copy(zero_vmem.at[pl.ds(0, c * TOKEN_SUBLANES), :],
                                               buf_hbm.at[pl.ds(row, c * TOKEN_SUBLANES), :], sem_fill)
                    cp.wait() if wait else cp.start()
            return carry

        lax.fori_loop(0, N_EXPERTS, body, 0)

    def tail(wait):
        def body(j, carry):
            row = pl.multiple_of(j * (EXPERT_ROWS * TOKEN_SUBLANES), EXPERT_ROWS * TOKEN_SUBLANES)
            cp = pltpu.make_async_copy(zero_vmem, buf_hbm.at[pl.ds(row, EXPERT_ROWS * TOKEN_SUBLANES), :], sem_fill)
            cp.wait() if wait else cp.start()
            return carry

        lax.fori_loop(n_used_ref[0], n_blk, body, 0)

    pads(False)
    tail(False)
    pads(True)
    tail(True)


def _dispatch_kernel(fill_start_ref, fill_len_ref, n_used_ref, *rest, tiles):
    v_refs = rest[:len(tiles)]
    pos_hbm, buf_hbm, idx_smem, sem_idx, sem_rows, zero_vmem, sem_fill = rest[len(tiles):]
    tm = v_refs[0].shape[0] // TOKEN_SUBLANES
    per_tile = tm * TOP_K
    i = pl.program_id(0)
    fetch = pltpu.make_async_copy(pos_hbm.at[pl.ds(i * per_tile, per_tile)], idx_smem, sem_idx)
    fetch.start()

    @pl.when(i == 0)
    def _():
        _zero_fill(fill_start_ref, fill_len_ref, n_used_ref, buf_hbm, zero_vmem, sem_fill)

    fetch.wait()

    def scatter(v_ref):
        def row_copy(t, p):
            return pltpu.make_async_copy(_token_tile(v_ref, t), _token_tile(buf_hbm, p), sem_rows)

        def issue(j, carry):
            for u in range(DMA_UNROLL):
                t = j * DMA_UNROLL + u
                for k in range(TOP_K):
                    row_copy(t, idx_smem[t * TOP_K + k]).start(priority=k % 2)
            return carry

        def drain(j, carry):
            for _ in range(DMA_UNROLL * TOP_K):
                row_copy(0, 0).wait()
            return carry

        lax.fori_loop(0, tm // DMA_UNROLL, issue, 0)
        lax.fori_loop(0, tm // DMA_UNROLL, drain, 0)

    first = 0
    for v_ref, n_tiles in zip(v_refs, tiles):
        @pl.when((i >= first) & (i < first + n_tiles))
        def _():
            scatter(v_ref)

        first += n_tiles


def _dispatch(fill_start, fill_len, n_used, v_parts, pos_flat, n_rows):
    tm = DISPATCH_TILE
    tiles = tuple(v.shape[0] // (tm * TOKEN_SUBLANES) for v in v_parts)
    firsts = [sum(tiles[:p]) for p in range(len(tiles))]
    any_spec = pl.BlockSpec(memory_space=pl.ANY)

    def part_spec(first, n_tiles):
        return pl.BlockSpec((tm * TOKEN_SUBLANES, LANES), lambda i, *_: (jnp.clip(i - first, 0, n_tiles - 1), 0))

    grid_spec = pltpu.PrefetchScalarGridSpec(
        num_scalar_prefetch=3,
        grid=(sum(tiles),),
        in_specs=[part_spec(f, n) for f, n in zip(firsts, tiles)] + [any_spec],
        out_specs=any_spec,
        scratch_shapes=[pltpu.SMEM((tm * TOP_K,), jnp.int32), pltpu.SemaphoreType.DMA, pltpu.SemaphoreType.DMA,
                        pltpu.VMEM((EXPERT_ROWS * TOKEN_SUBLANES, LANES), F32), pltpu.SemaphoreType.DMA],
    )
    return pl.pallas_call(
        functools.partial(_dispatch_kernel, tiles=tiles),
        grid_spec=grid_spec,
        out_shape=jax.ShapeDtypeStruct((n_rows * TOKEN_SUBLANES, LANES), F32),
        compiler_params=_params("arbitrary"),
        name="dispatch",
    )(fill_start, fill_len, n_used, *v_parts, pos_flat)


def _pair_shuffle():
    l = np.arange(LANES)
    p = np.zeros((LANES, LANES), np.float32)
    p[l, l // 2 + (LANES // 2) * (l % 2)] = 1.0
    return p


def _expert_kernel(blk_e_ref, n_used_ref, run_ref, next_ref, x_ref, bu_ref, bd_ref, perm_ref, wu_hbm, wd_hbm, o_ref,
                   wu_f32, wd_f32, wu_bf, wd_bf, sem_u, sem_d, *, layer):
    i = pl.program_id(0)
    active = i < n_used_ref[0]
    expert = blk_e_ref[i]
    new_expert = (i == 0) | (expert != blk_e_ref[jnp.maximum(i - 1, 0)])

    def fetch(e, slot):
        return (pltpu.make_async_copy(wu_hbm.at[layer, e], wu_f32.at[slot], sem_u.at[slot]),
                pltpu.make_async_copy(wd_hbm.at[layer, e], wd_f32.at[slot], sem_d.at[slot]))

    @pl.when(active & new_expert)
    def _():
        slot = run_ref[expert] % 2

        @pl.when(i == 0)
        def _():
            for cp in fetch(expert, slot):
                cp.start()

        for cp in fetch(expert, slot):
            cp.wait()
        nxt = next_ref[expert]

        @pl.when(nxt >= 0)
        def _():
            for cp in fetch(nxt, 1 - slot):
                cp.start()

        for c in range(D_MODEL // LANES):
            r = c * LANES
            wu_bf[pl.ds(r, LANES), :] = wu_f32[slot, pl.ds(r, LANES), :].astype(BF16)
            wd_bf[pl.ds(r, LANES), :] = jnp.dot(perm_ref[...], wd_f32[slot, pl.ds(r, LANES), :].astype(BF16),
                                               preferred_element_type=F32).astype(BF16)

    @pl.when(active)
    def _():
        h = jnp.dot(_load_token_tiles(x_ref, EXPERT_ROWS).astype(BF16), wu_bf[...], preferred_element_type=F32) + bu_ref[0]
        even = lax.broadcasted_iota(jnp.int32, (EXPERT_ROWS, LANES), 1) % 2 == 0
        acts = []
        for k in range(D_EXPERT // LANES):
            ha = h[:, 2 * k * LANES:(2 * k + 1) * LANES]
            hb = h[:, (2 * k + 1) * LANES:(2 * k + 2) * LANES]
            glu = jnp.where(even, ha, pltpu.roll(hb, 1, axis=1))
            lin = jnp.where(even, pltpu.roll(ha, LANES - 1, axis=1), hb)
            glu = jnp.minimum(glu, SWIGLU_LIMIT)
            lin = jnp.clip(lin, -SWIGLU_LIMIT, SWIGLU_LIMIT)
            acts.append((glu * jax.nn.sigmoid(SWIGLU_ALPHA * glu) * (lin + 1.0)).astype(BF16))
        a = jnp.concatenate(acts, axis=1)
        _store_token_tiles(o_ref, jnp.dot(a, wd_bf[...], preferred_element_type=F32) + bd_ref[0])

    @pl.when(jnp.logical_not(active))
    def _():
        o_ref[...] = jnp.zeros_like(o_ref)


def _experts(blk_e, n_used, run_idx, next_expert, xbuf, layer, wu, bu, wd, bd):
    n_blk = xbuf.shape[0] // (EXPERT_ROWS * TOKEN_SUBLANES)
    grid_spec = pltpu.PrefetchScalarGridSpec(
        num_scalar_prefetch=4,
        grid=(n_blk,),
        in_specs=[
            pl.BlockSpec((EXPERT_ROWS * TOKEN_SUBLANES, LANES), lambda i, e, *_: (i, 0)),
            pl.BlockSpec((1, 1, 2 * D_EXPERT), lambda i, e, *_: (e[i], 0, 0)),
            pl.BlockSpec((1, 1, D_MODEL), lambda i, e, *_: (e[i], 0, 0)),
            pl.BlockSpec((LANES, LANES), lambda i, e, *_: (0, 0)),
            pl.BlockSpec(memory_space=pl.ANY),
            pl.BlockSpec(memory_space=pl.ANY),
        ],
        out_specs=pl.BlockSpec((EXPERT_ROWS * TOKEN_SUBLANES, LANES), lambda i, e, *_: (i, 0)),
        scratch_shapes=[
            pltpu.VMEM((2, D_MODEL, 2 * D_EXPERT), F32),
            pltpu.VMEM((2, D_EXPERT, D_MODEL), F32),
            pltpu.VMEM((D_MODEL, 2 * D_EXPERT), BF16),
            pltpu.VMEM((D_EXPERT, D_MODEL), BF16),
            pltpu.SemaphoreType.DMA((2,)),
            pltpu.SemaphoreType.DMA((2,)),
        ],
    )
    return pl.pallas_call(
        functools.partial(_expert_kernel, layer=layer),
        grid_spec=grid_spec,
        out_shape=jax.ShapeDtypeStruct(xbuf.shape, F32),
        compiler_params=pltpu.CompilerParams(dimension_semantics=("arbitrary",), vmem_limit_bytes=EXPERT_VMEM_LIMIT),
        name="experts",
    )(blk_e, n_used, run_idx, next_expert, xbuf, bu, bd, jnp.asarray(_pair_shuffle(), BF16), wu, wd)


def _combine_kernel(x_ref, gate_ref, g2_ref, lng_ref, lnb_ref, pos_hbm, y_hbm, o_ref, idx_smem, rows, sem_idx, sem_rows):
    tm = x_ref.shape[0]
    per_tile = tm * TOP_K
    i = pl.program_id(0)

    def row_copy(slot, t, k, p):
        return pltpu.make_async_copy(_token_tile(y_hbm, p), _token_tile(rows, (slot * TOP_K + k) * tm + t), sem_rows.at[slot])

    def start_gather(tile, slot):
        fetch = pltpu.make_async_copy(pos_hbm.at[pl.ds(tile * per_tile, per_tile)], idx_smem, sem_idx)
        fetch.start()
        fetch.wait()

        def issue(j, carry):
            for u in range(DMA_UNROLL):
                t = j * DMA_UNROLL + u
                for k in range(TOP_K):
                    row_copy(slot, t, k, idx_smem[t * TOP_K + k]).start(priority=k % 2)
            return carry

        lax.fori_loop(0, tm // DMA_UNROLL, issue, 0)

    @pl.when(i == 0)
    def _():
        start_gather(0, 0)

    @pl.when(i + 1 < pl.num_programs(0))
    def _():
        start_gather(i + 1, (i + 1) % 2)

    slot = i % 2

    def drain(j, carry):
        for _ in range(DMA_UNROLL * TOP_K):
            row_copy(slot, 0, 0, 0).wait()
        return carry

    lax.fori_loop(0, tm // DMA_UNROLL, drain, 0)
    chunks = []
    for c in range(TOKEN_SUBLANES):
        def part(k):
            first = pl.multiple_of((slot * TOP_K + k) * (tm * TOKEN_SUBLANES), tm * TOKEN_SUBLANES) + c
            return gate_ref[:, k:k + 1] * rows[pl.ds(first, tm, stride=TOKEN_SUBLANES), :]

        fc = part(0)
        for k in range(1, TOP_K):
            fc = fc + part(k)
        chunks.append(fc)
    f = jnp.concatenate(chunks, axis=1)
    o_ref[...] = _layer_norm(DEEPNORM_ALPHA * x_ref[...] + g2_ref[0] * f, lng_ref[...], lnb_ref[...])


def _combine(x2, gates, pos_flat, ybuf, g2, lng, lnb, seq_rows, mod_row):
    t = x2.shape[0]
    tm = COMBINE_TILE
    row = pl.BlockSpec((tm, D_MODEL), lambda i: (i, 0))
    const = pl.BlockSpec((1, D_MODEL), lambda i: (0, 0))
    return pl.pallas_call(
        _combine_kernel,
        grid=(t // tm,),
        in_specs=[row, pl.BlockSpec((tm, LANES), lambda i: (i, 0)), _mod_spec(tm, seq_rows, mod_row), const, const,
                  pl.BlockSpec(memory_space=pl.ANY), pl.BlockSpec(memory_space=pl.ANY)],
        out_specs=row,
        out_shape=jax.ShapeDtypeStruct((t, D_MODEL), F32),
        scratch_shapes=[
            pltpu.SMEM((tm * TOP_K,), jnp.int32),
            pltpu.VMEM((2 * TOP_K * tm * TOKEN_SUBLANES, LANES), F32),
            pltpu.SemaphoreType.DMA,
            pltpu.SemaphoreType.DMA((2,)),
        ],
        compiler_params=_params("arbitrary"),
        name="combine",
    )(x2, gates, g2, lng, lnb, pos_flat, ybuf)


def _moe(v_parts, logits, layer, wu, bu, wd, bd):
    t = logits.shape[0]
    n = t * TOP_K
    idx, gates, cnt = _route(logits)
    counts = cnt[0, :N_EXPERTS].astype(jnp.int32)
    padded = (counts + EXPERT_ROWS - 1) // EXPERT_ROWS * EXPERT_ROWS
    pad_ends = jnp.cumsum(padded)
    pad_starts = pad_ends - padded
    experts = jnp.arange(N_EXPERTS, dtype=jnp.int32)
    start = jnp.sum(jnp.where(idx[:, :TOP_K, None] == experts, pad_starts, 0), axis=-1)
    pos_flat = (start + idx[:, TOP_K:2 * TOP_K]).reshape(n)
    n_blk = n // EXPERT_ROWS + N_EXPERTS
    blk_start = jnp.arange(n_blk, dtype=jnp.int32) * EXPERT_ROWS
    blk_e = jnp.minimum(jnp.sum((pad_ends[None, :] <= blk_start[:, None]).astype(jnp.int32), axis=1), N_EXPERTS - 1)
    n_used = pad_ends[-1:] // EXPERT_ROWS
    nonempty = counts > 0
    run_idx = jnp.cumsum(nonempty.astype(jnp.int32)) - 1
    later = nonempty[None, :] & (experts[None, :] > experts[:, None])
    next_expert = jnp.min(jnp.where(later, experts[None, :], N_EXPERTS), axis=1)
    next_expert = jnp.where(next_expert == N_EXPERTS, -1, next_expert)
    fill_start, fill_len = pad_starts + counts, padded - counts
    xbuf = _dispatch(fill_start, fill_len, n_used, v_parts, pos_flat, n_blk * EXPERT_ROWS)
    ybuf = _experts(blk_e, n_used, run_idx, next_expert, xbuf, layer, wu, bu, wd, bd)
    return ybuf, pos_flat, gates


def _rope_tables(seq):
    rows = seq // GRID_W
    row = jnp.repeat(jnp.arange(rows), GRID_W).astype(F32)
    col = jnp.tile(jnp.arange(GRID_W), rows).astype(F32)
    n_freq = HEAD_DIM // 4
    inv = ROPE_BASE ** (-jnp.arange(n_freq, dtype=F32) / n_freq)
    ang = jnp.concatenate([row[:, None] * inv, col[:, None] * inv], -1)
    ang = jnp.concatenate([ang, ang], -1)
    sign = jnp.where(jnp.arange(HEAD_DIM) < HEAD_DIM // 2, -1.0, 1.0).astype(F32)
    cos = jnp.cos(ang)
    sin = jnp.sin(ang) * sign
    return jnp.tile(cos, (1, LANES // HEAD_DIM)), jnp.tile(sin, (1, LANES // HEAD_DIM))


def kernel(x, c, ctx, c_ctx, w_mod, b_mod, w_in_even, sink, w_out_even, w_in_odd, conv_w, w_out_odd, ln_g, ln_b,
           w_router, b_router, w_up, b_up, w_down, b_down):
    B, S, D = x.shape
    L = ctx.shape[1]
    T, TC = B * S, B * L
    cos, sin = _rope_tables(S)
    cond = jnp.concatenate([c, c_ctx[None, :], jnp.zeros((MOD_ROWS - B - 1, D), F32)], 0)
    mod = _modulation(cond, w_mod, b_mod)

    bu = b_up.reshape(DEPTH, N_EXPERTS, 1, 2 * D_EXPERT)
    bd = b_down.reshape(DEPTH, N_EXPERTS, 1, D)
    wr_hi = w_router.astype(BF16)
    wr_lo = (w_router - wr_hi.astype(F32)).astype(BF16)
    pad_r = ((0, 0), (0, 0), (0, LANES - N_EXPERTS))
    wr_hi, wr_lo = jnp.pad(wr_hi, pad_r), jnp.pad(wr_lo, pad_r)
    br = jnp.pad(b_router, ((0, 0), (0, LANES - N_EXPERTS)), constant_values=NEG_INF).reshape(DEPTH, 1, LANES)

    x2 = x.reshape(T, D)
    xc2 = ctx.reshape(TC, D)
    for l in range(DEPTH):
        even = l % 2 == 0
        j = l // 2
        ctx_after = any(m % 2 == 0 for m in range(l + 1, DEPTH))
        sh1, sc1, g1, sh2, sc2, g2 = [m.reshape(MOD_ROWS, 1, D) for m in jnp.split(mod[l], 6, axis=-1)]
        lng1, lnb1 = ln_g[l, 0].reshape(1, D), ln_b[l, 0].reshape(1, D)
        lng2, lnb2 = ln_g[l, 1].reshape(1, D), ln_b[l, 1].reshape(1, D)
        post = (g1, sc2, sh2, lng1, lnb1, wr_hi[l], wr_lo[l], br[l])
        if even:
            w_in = w_in_even[j].astype(BF16)
            w_out = w_out_even[j].astype(BF16)
            h = _inproj_even(x2, sc1, sh1, w_in, cos, sin, S, None, True)
            hc = _inproj_even(xc2, sc1, sh1, w_in, cos, sin, L, B, False)
            kcol = ATTN_WIDTH // KV_WIDTH
            attn = _win_attention(h, hc, sink[j], B, S, kcol, kcol + 1)
            four = _fourier_mix(h, B, S)
            x2, v, logits = _outproj("even", (attn, four), w_out, x2, *post, S, None)
            if ctx_after:
                attn_c = _ctx_attention(hc, sink[j], B)
                four_c = _fourier_mix(hc, B, L)
                xc2, vc, logits_c = _outproj("even", (attn_c, four_c), w_out, xc2, *post, L, B)
        else:
            w_in = w_in_odd[j].astype(BF16)
            w_out = w_out_odd[j].astype(BF16)
            cw = jnp.pad(conv_w[j], ((0, 8 - conv_w.shape[1]), (0, 0)))
            bz = _inproj_odd(x2, sc1, sh1, w_in, S, None)
            x2, v, logits = _outproj("odd", (bz,), w_out, x2, *post, S, None, conv_w=cw)
            if ctx_after:
                bzc = _inproj_odd(xc2, sc1, sh1, w_in, L, B)
                xc2, vc, logits_c = _outproj("odd", (bzc,), w_out, xc2, *post, L, B, conv_w=cw)
        if ctx_after:
            ybuf, pos, gates = _moe((v, vc), jnp.concatenate([logits, logits_c], 0), l, w_up, bu[l], w_down, bd[l])
            xc2 = _combine(xc2, gates[T:], pos[T * TOP_K:], ybuf, g2, lng2, lnb2, L, B)
        else:
            ybuf, pos, gates = _moe((v,), logits, l, w_up, bu[l], w_down, bd[l])
        x2 = _combine(x2, gates[:T], pos[:T * TOP_K], ybuf, g2, lng2, lnb2, S, None)
    return x2.reshape(B, S, D)
```

```python
import functools

import numpy as np
import jax
import jax.numpy as jnp
from jax import lax
from jax.experimental import pallas as pl
from jax.experimental.pallas import tpu as pltpu

D_MODEL = 1024
DEPTH = 4
GRID_W = 64
HEAD_DIM = 64
N_Q_HEADS = 12
N_KV_HEADS = 4
Q_PER_KV = N_Q_HEADS // N_KV_HEADS
BLOCK = 128
ROPE_BASE = 10000.0
FOURIER_GROUP_DIM = 64
FOURIER_WIDTH = 256
ATTN_WIDTH = N_Q_HEADS * HEAD_DIM
KV_WIDTH = N_KV_HEADS * HEAD_DIM
EVEN_IN_WIDTH = ATTN_WIDTH + 2 * KV_WIDTH + FOURIER_WIDTH
N_EXPERTS = 32
TOP_K = 4
D_EXPERT = D_MODEL
SWIGLU_LIMIT = 7.0
SWIGLU_ALPHA = 1.702
LN_EPS = 1e-5
NEG_INF = -1e30
DEEPNORM_ALPHA = (2 * DEPTH) ** 0.25

LANES = 128
SUBLANES = 8
TOKEN_SUBLANES = D_MODEL // LANES
ROW_TILE = 1024
EXPERT_ROWS = 512
MOD_ROWS = 16
ROUTE_TILE = 512
DISPATCH_TILE = 1024
COMBINE_TILE = 512
DMA_UNROLL = 8
VMEM_LIMIT = 48 * 1024 * 1024
EXPERT_VMEM_LIMIT = 56 * 1024 * 1024

F32 = jnp.float32
BF16 = jnp.bfloat16


def _params(*sem):
    return pltpu.CompilerParams(dimension_semantics=sem, vmem_limit_bytes=VMEM_LIMIT)


def _mod_spec(tm, seq_rows, mod_row):
    if mod_row is None:
        return pl.BlockSpec((1, 1, D_MODEL), lambda i: ((i * tm) // seq_rows, 0, 0))
    return pl.BlockSpec((1, 1, D_MODEL), lambda i: (mod_row, 0, 0))


def _mod_kernel(c_ref, w_ref, b_ref, o_ref):
    c = c_ref[...]
    s = (c * jax.nn.sigmoid(c)).astype(BF16)
    o_ref[0] = jnp.dot(s, w_ref[0].astype(BF16), preferred_element_type=F32) + b_ref[0]


def _modulation(cond, w_mod, b_mod):
    tn = 1536
    n = w_mod.shape[-1]
    return pl.pallas_call(
        _mod_kernel,
        grid=(DEPTH, n // tn),
        in_specs=[
            pl.BlockSpec((MOD_ROWS, D_MODEL), lambda l, j: (0, 0)),
            pl.BlockSpec((1, D_MODEL, tn), lambda l, j: (l, 0, j)),
            pl.BlockSpec((1, 1, tn), lambda l, j: (l, 0, j)),
        ],
        out_specs=pl.BlockSpec((1, MOD_ROWS, tn), lambda l, j: (l, 0, j)),
        out_shape=jax.ShapeDtypeStruct((DEPTH, MOD_ROWS, n), F32),
        compiler_params=_params("arbitrary", "arbitrary"),
        name="modulation",
    )(cond, w_mod, b_mod.reshape(DEPTH, 1, n))


def _rope_chunk(h, cos, sin_signed, first_half):
    swapped = jnp.where(first_half, pltpu.roll(h, LANES - HEAD_DIM // 2, axis=1), pltpu.roll(h, HEAD_DIM // 2, axis=1))
    return h * cos + swapped * sin_signed


def _inproj_even_kernel(x_ref, sc_ref, sh_ref, w_ref, cos_ref, sin_ref, o_ref, *, rope):
    u = (x_ref[...] * (1.0 + sc_ref[0]) + sh_ref[0]).astype(BF16)
    tn = 512
    if rope:
        cos = cos_ref[...]
        sin = sin_ref[...]
        lane = lax.broadcasted_iota(jnp.int32, cos.shape, 1)
        first_half = (lane % HEAD_DIM) < HEAD_DIM // 2
    scale = HEAD_DIM ** -0.5
    for jt in range(EVEN_IN_WIDTH // tn):
        acc = jnp.dot(u, w_ref[:, jt * tn:(jt + 1) * tn], preferred_element_type=F32)
        for k in range(tn // LANES):
            col = jt * tn + k * LANES
            h = acc[:, k * LANES:(k + 1) * LANES]
            if rope and col < ATTN_WIDTH + KV_WIDTH:
                h = _rope_chunk(h, cos, sin, first_half)
            if col < ATTN_WIDTH:
                h = h * scale
            o_ref[:, col:col + LANES] = h.astype(BF16)


def _inproj_even(x2, sc, sh, w, cos, sin, seq_rows, mod_row, rope):
    t = x2.shape[0]
    tm = min(ROW_TILE, seq_rows)
    pos_tiles = cos.shape[0] // tm
    mod = _mod_spec(tm, seq_rows, mod_row)

    return pl.pallas_call(
        functools.partial(_inproj_even_kernel, rope=rope),
        grid=(t // tm,),
        in_specs=[
            pl.BlockSpec((tm, D_MODEL), lambda i: (i, 0)),
            mod,
            mod,
            pl.BlockSpec((D_MODEL, EVEN_IN_WIDTH), lambda i: (0, 0)),
            pl.BlockSpec((tm, LANES), lambda i: (i % pos_tiles, 0)),
            pl.BlockSpec((tm, LANES), lambda i: (i % pos_tiles, 0)),
        ],
        out_specs=pl.BlockSpec((tm, EVEN_IN_WIDTH), lambda i: (i, 0)),
        out_shape=jax.ShapeDtypeStruct((t, EVEN_IN_WIDTH), BF16),
        compiler_params=_params("arbitrary"),
        name="inproj_even",
    )(x2, sc, sh, w, cos, sin)


def _attend(q_ref, k_all, v_all, valid, sink_ref, o_ref):
    rows = q_ref.shape[0]
    lane = lax.broadcasted_iota(jnp.int32, (rows, LANES), 1)
    low = lane < HEAD_DIM
    scores, sks = [], []
    for kv in range(N_KV_HEADS):
        kv_chunk, kv_half = divmod(kv, 2)
        keep = low if kv_half == 0 else jnp.logical_not(low)
        stack, sinks = [], []
        for g in range(Q_PER_KV):
            head = kv * Q_PER_KV + g
            chunk, half = divmod(head, 2)
            qh = q_ref[:, chunk * LANES:(chunk + 1) * LANES].astype(F32)
            if half != kv_half:
                qh = pltpu.roll(qh, HEAD_DIM, axis=1)
            stack.append(jnp.where(keep, qh, 0.0).astype(BF16))
            sinks.append(jnp.full((rows, 1), sink_ref[head], F32))
        qs = jnp.concatenate(stack, axis=0)
        sks.append(jnp.concatenate(sinks, axis=0))
        kc = k_all[:, kv_chunk * LANES:(kv_chunk + 1) * LANES]
        s = lax.dot_general(qs, kc, (((1,), (1,)), ((), ())), preferred_element_type=F32)
        if valid is not None:
            s = jnp.concatenate([s[:, j * BLOCK:(j + 1) * BLOCK] if ok is None
                                 else jnp.where(ok, s[:, j * BLOCK:(j + 1) * BLOCK], NEG_INF)
                                 for j, ok in enumerate(valid)], axis=1)
        scores.append(s)
    probs, sink_terms = [], []
    for kv in range(N_KV_HEADS):
        m = jnp.maximum(sks[kv], jnp.max(scores[kv], axis=-1, keepdims=True))
        probs.append(jnp.exp(scores[kv] - m).astype(BF16))
        sink_terms.append(jnp.exp(sks[kv] - m))
    key_lane = lax.broadcasted_iota(jnp.int32, (v_all.shape[0], LANES), 1)
    for kv in range(N_KV_HEADS):
        kv_chunk, kv_half = divmod(kv, 2)
        vc = v_all[:, kv_chunk * LANES:(kv_chunk + 1) * LANES]
        own = (key_lane < HEAD_DIM) if kv_half == 0 else (key_lane >= HEAD_DIM)
        o = jnp.dot(probs[kv], jnp.where(own, vc, jnp.ones_like(vc)), preferred_element_type=F32)
        other = (1 - kv_half) * HEAD_DIM
        o = o / (o[:, other:other + 1] + sink_terms[kv])
        for g in range(Q_PER_KV):
            head = kv * Q_PER_KV + g
            half = head % 2
            oh = o[g * rows:(g + 1) * rows]
            if half != kv_half:
                oh = pltpu.roll(oh, HEAD_DIM, axis=1)
            o_ref[:, head * HEAD_DIM:(head + 1) * HEAD_DIM] = oh[:, half * HEAD_DIM:(half + 1) * HEAD_DIM].astype(BF16)


def _win_attn_kernel(sink_ref, q_ref, kp_ref, kc_ref, kn_ref, vp_ref, vc_ref, vn_ref, kx_ref, vx_ref, o_ref, *, nb):
    n = pl.program_id(1)
    k_all = jnp.concatenate([kp_ref[...], kc_ref[...], kn_ref[...], kx_ref[...]], axis=0)
    v_all = jnp.concatenate([vp_ref[...], vc_ref[...], vn_ref[...], vx_ref[...]], axis=0)
    r = lax.broadcasted_iota(jnp.int32, (Q_PER_KV * BLOCK, BLOCK), 0) % BLOCK
    c = lax.broadcasted_iota(jnp.int32, (Q_PER_KV * BLOCK, BLOCK), 1)
    in_prev = c >= jnp.where(n > 0, r, BLOCK)
    in_next = c <= jnp.where(n < nb - 1, r, -1)
    valid = [in_prev, None, in_next] + [None] * (kx_ref.shape[0] // BLOCK)
    _attend(q_ref, k_all, v_all, valid, sink_ref, o_ref)


def _win_attention(h, hc, sink, batch, seq, kc_col, vc_col):
    nb = seq // BLOCK
    qcol, kcol, vcol = 0, ATTN_WIDTH // KV_WIDTH, ATTN_WIDTH // KV_WIDTH + 1

    def blk(shift, col):
        return pl.BlockSpec((BLOCK, KV_WIDTH), lambda b, n: (b * nb + jnp.clip(n + shift, 0, nb - 1), col))

    return pl.pallas_call(
        functools.partial(_win_attn_kernel, nb=nb),
        grid=(batch, nb),
        in_specs=[
            pl.BlockSpec(memory_space=pltpu.SMEM),
            pl.BlockSpec((BLOCK, ATTN_WIDTH), lambda b, n: (b * nb + n, qcol)),
            blk(-1, kcol), blk(0, kcol), blk(1, kcol),
            blk(-1, vcol), blk(0, vcol), blk(1, vcol),
            pl.BlockSpec((hc.shape[0] // batch, KV_WIDTH), lambda b, n: (b, kc_col)),
            pl.BlockSpec((hc.shape[0] // batch, KV_WIDTH), lambda b, n: (b, vc_col)),
        ],
        out_specs=pl.BlockSpec((BLOCK, ATTN_WIDTH), lambda b, n: (b * nb + n, 0)),
        out_shape=jax.ShapeDtypeStruct((batch * seq, ATTN_WIDTH), BF16),
        compiler_params=_params("arbitrary", "arbitrary"),
        name="window_attention",
    )(sink, h, h, h, h, h, h, h, hc, hc)


def _ctx_attn_kernel(sink_ref, q_ref, k_ref, v_ref, o_ref):
    _attend(q_ref, k_ref[...], v_ref[...], None, sink_ref, o_ref)


def _ctx_attention(hc, sink, batch):
    ctx_len = hc.shape[0] // batch
    kcol, vcol = ATTN_WIDTH // KV_WIDTH, ATTN_WIDTH // KV_WIDTH + 1
    return pl.pallas_call(
        _ctx_attn_kernel,
        grid=(batch,),
        in_specs=[
            pl.BlockSpec(memory_space=pltpu.SMEM),
            pl.BlockSpec((ctx_len, ATTN_WIDTH), lambda b: (b, 0)),
            pl.BlockSpec((ctx_len, KV_WIDTH), lambda b: (b, kcol)),
            pl.BlockSpec((ctx_len, KV_WIDTH), lambda b: (b, vcol)),
        ],
        out_specs=pl.BlockSpec((ctx_len, ATTN_WIDTH), lambda b: (b, 0)),
        out_shape=jax.ShapeDtypeStruct((hc.shape[0], ATTN_WIDTH), BF16),
        compiler_params=_params("arbitrary"),
        name="context_attention",
    )(sink, hc, hc, hc)


def _dft_mats(n):
    k = np.arange(n, dtype=np.int64)
    ang = 2.0 * np.pi * ((k[:, None] * k[None, :]) % n).astype(np.float64) / n
    return np.cos(ang), np.sin(ang)


def _group_dft():
    c, s = _dft_mats(FOURIER_GROUP_DIM)
    groups = FOURIER_WIDTH // FOURIER_GROUP_DIM
    eye = np.eye(groups)
    return np.concatenate([np.kron(eye, c), np.kron(eye, s)], axis=1)


def _fourier_kernel(f_ref, w1_ref, c_ref, s_ref, o_ref, ab_ref, *, scale):
    b = pl.program_id(1)

    @pl.when(pl.program_id(0) == 0)
    def _():
        ab_ref[b] = jnp.dot(f_ref[...], w1_ref[...], preferred_element_type=F32).astype(BF16)

    y = (jnp.dot(c_ref[...], ab_ref[b, :, :FOURIER_WIDTH], preferred_element_type=F32)
         - jnp.dot(s_ref[...], ab_ref[b, :, FOURIER_WIDTH:], preferred_element_type=F32))
    o_ref[...] = (y * scale).astype(BF16)


def _fourier_mix(h, batch, n):
    tr = 256
    cn, sn = _dft_mats(n)
    fcol = (EVEN_IN_WIDTH - FOURIER_WIDTH) // FOURIER_WIDTH
    return pl.pallas_call(
        functools.partial(_fourier_kernel, scale=float((n * FOURIER_GROUP_DIM) ** -0.5)),
        grid=(n // tr, batch),
        in_specs=[
            pl.BlockSpec((n, FOURIER_WIDTH), lambda i, b: (jnp.where(i == 0, b, batch - 1), fcol)),
            pl.BlockSpec((FOURIER_WIDTH, 2 * FOURIER_WIDTH), lambda i, b: (0, 0)),
            pl.BlockSpec((tr, n), lambda i, b: (i, 0)),
            pl.BlockSpec((tr, n), lambda i, b: (i, 0)),
        ],
        out_specs=pl.BlockSpec((tr, FOURIER_WIDTH), lambda i, b: (b * (n // tr) + i, 0)),
        out_shape=jax.ShapeDtypeStruct((batch * n, FOURIER_WIDTH), BF16),
        scratch_shapes=[pltpu.VMEM((batch, n, 2 * FOURIER_WIDTH), BF16)],
        compiler_params=_params("arbitrary", "arbitrary"),
        name="fourier_mix",
    )(h, jnp.asarray(_group_dft(), BF16), jnp.asarray(cn, BF16), jnp.asarray(sn, BF16))


def _inproj_odd_kernel(x_ref, sc_ref, sh_ref, w_ref, o_ref):
    u = (x_ref[...] * (1.0 + sc_ref[0]) + sh_ref[0]).astype(BF16)
    tn = 512
    for jt in range(D_MODEL // tn):
        cols = slice(jt * tn, (jt + 1) * tn)
        b_gate = jnp.dot(u, w_ref[:, jt * tn:(jt + 1) * tn], preferred_element_type=F32)
        o_ref[:, cols] = b_gate.astype(BF16)
        c_gate = jnp.dot(u, w_ref[:, D_MODEL + jt * tn:D_MODEL + (jt + 1) * tn], preferred_element_type=F32)
        hh = jnp.dot(u, w_ref[:, 2 * D_MODEL + jt * tn:2 * D_MODEL + (jt + 1) * tn], preferred_element_type=F32)
        o_ref[:, D_MODEL + jt * tn:D_MODEL + (jt + 1) * tn] = (c_gate * hh).astype(BF16)


def _inproj_odd(x2, sc, sh, w, seq_rows, mod_row):
    t = x2.shape[0]
    tm = min(ROW_TILE, seq_rows)
    mod = _mod_spec(tm, seq_rows, mod_row)

    return pl.pallas_call(
        _inproj_odd_kernel,
        grid=(t // tm,),
        in_specs=[
            pl.BlockSpec((tm, D_MODEL), lambda i: (i, 0)),
            mod,
            mod,
            pl.BlockSpec((D_MODEL, 3 * D_MODEL), lambda i: (0, 0)),
        ],
        out_specs=pl.BlockSpec((tm, 2 * D_MODEL), lambda i: (i, 0)),
        out_shape=jax.ShapeDtypeStruct((t, 2 * D_MODEL), BF16),
        compiler_params=_params("arbitrary"),
        name="inproj_odd",
    )(x2, sc, sh, w)


def _layer_norm(r, g, b):
    mu = jnp.mean(r, axis=-1, keepdims=True)
    d = r - mu
    var = jnp.mean(d * d, axis=-1, keepdims=True)
    return d * lax.rsqrt(var + LN_EPS) * g + b


def _store_token_tiles(ref, rows):
    m = rows.shape[0]
    for c in range(TOKEN_SUBLANES):
        ref[pl.ds(c, m, stride=TOKEN_SUBLANES), :] = rows[:, c * LANES:(c + 1) * LANES]


def _load_token_tiles(ref, m, first_row=0):
    return jnp.concatenate([ref[pl.ds(first_row + c, m, stride=TOKEN_SUBLANES), :] for c in range(TOKEN_SUBLANES)], axis=1)


def _token_tile(ref, t):
    row = t * TOKEN_SUBLANES
    return ref.at[pl.ds(row if isinstance(row, int) else pl.multiple_of(row, TOKEN_SUBLANES), TOKEN_SUBLANES), :]


def _post_mixer(y, x_ref, g1_ref, sc2_ref, sh2_ref, lng_ref, lnb_ref, wrh_ref, wrl_ref, br_ref, xo_ref, v_ref, lg_ref):
    xn = _layer_norm(DEEPNORM_ALPHA * x_ref[...] + g1_ref[0] * y, lng_ref[...], lnb_ref[...])
    xo_ref[...] = xn
    v = xn * (1.0 + sc2_ref[0]) + sh2_ref[0]
    _store_token_tiles(v_ref, v)
    v_hi = v.astype(BF16)
    v_lo = (v - v_hi.astype(F32)).astype(BF16)
    lg_ref[...] = (jnp.dot(v_hi, wrh_ref[...], preferred_element_type=F32)
                   + (jnp.dot(v_lo, wrh_ref[...], preferred_element_type=F32)
                      + jnp.dot(v_hi, wrl_ref[...], preferred_element_type=F32))) + br_ref[...]


def _outproj_even_kernel(a_ref, f_ref, w_ref, *rest):
    y = (jnp.dot(a_ref[...], w_ref[:ATTN_WIDTH, :], preferred_element_type=F32)
         + jnp.dot(f_ref[...], w_ref[ATTN_WIDTH:, :], preferred_element_type=F32))
    _post_mixer(y, *rest)


def _outproj_odd_kernel(b_ref, z_ref, zp_ref, zn_ref, cw_ref, w_ref, *rest, rows_per_seq):
    tm = z_ref.shape[0]
    halo = zp_ref.shape[0]
    row0 = pl.program_id(0) * tm
    z = z_ref[...].astype(F32)
    prev_row = jnp.where(row0 % rows_per_seq == 0, 0.0, zp_ref[halo - 1:halo, :].astype(F32))
    next_row = jnp.where((row0 + tm) % rows_per_seq == 0, 0.0, zn_ref[0:1, :].astype(F32))
    r = lax.broadcasted_iota(jnp.int32, z.shape, 0)
    z_prev = jnp.where(r == 0, prev_row, pltpu.roll(z, 1, axis=0))
    z_next = jnp.where(r == tm - 1, next_row, pltpu.roll(z, tm - 1, axis=0))
    conv = cw_ref[0:1, :] * z_prev + cw_ref[1:2, :] * z + cw_ref[2:3, :] * z_next
    a = (b_ref[...].astype(F32) * conv).astype(BF16)
    y = jnp.dot(a, w_ref[...], preferred_element_type=F32)
    _post_mixer(y, *rest)


def _outproj(kind, mix_inputs, w_out, x2, g1, sc2, sh2, lng, lnb, wr_hi, wr_lo, br, seq_rows, mod_row, conv_w=None):
    t = x2.shape[0]
    tm = min(ROW_TILE, seq_rows)
    mod = _mod_spec(tm, seq_rows, mod_row)

    row = lambda width: pl.BlockSpec((tm, width), lambda i: (i, 0))
    const = lambda shape: pl.BlockSpec(shape, lambda i: (0,) * len(shape))
    if kind == "even":
        attn, four = mix_inputs
        kern = _outproj_even_kernel
        head_specs = [row(ATTN_WIDTH), row(FOURIER_WIDTH)]
        head_args = [attn, four]
    else:
        (bz,) = mix_inputs
        halo = 16
        last = t // halo - 1
        kern = functools.partial(_outproj_odd_kernel, rows_per_seq=seq_rows)
        head_specs = [
            pl.BlockSpec((tm, D_MODEL), lambda i: (i, 0)),
            pl.BlockSpec((tm, D_MODEL), lambda i: (i, 1)),
            pl.BlockSpec((halo, D_MODEL), lambda i: (jnp.maximum(i * (tm // halo) - 1, 0), 1)),
            pl.BlockSpec((halo, D_MODEL), lambda i: (jnp.minimum((i + 1) * (tm // halo), last), 1)),
            const((8, D_MODEL)),
        ]
        head_args = [bz, bz, bz, bz, conv_w]
    return pl.pallas_call(
        kern,
        grid=(t // tm,),
        in_specs=head_specs + [
            const((D_MODEL, D_MODEL)),
            row(D_MODEL),
            mod,
            mod,
            mod,
            const((1, D_MODEL)), const((1, D_MODEL)),
            const((D_MODEL, LANES)), const((D_MODEL, LANES)), const((1, LANES)),
        ],
        out_specs=[row(D_MODEL), pl.BlockSpec((tm * TOKEN_SUBLANES, LANES), lambda i: (i, 0)), row(LANES)],
        out_shape=[
            jax.ShapeDtypeStruct((t, D_MODEL), F32),
            jax.ShapeDtypeStruct((t * TOKEN_SUBLANES, LANES), F32),
            jax.ShapeDtypeStruct((t, LANES), F32),
        ],
        compiler_params=_params("arbitrary"),
        name="outproj_" + kind,
    )(*head_args, w_out, x2, g1, sc2, sh2, lng, lnb, wr_hi, wr_lo, br)


def _route_kernel(lg_ref, tri_ref, idx_ref, gate_ref, cnt_ref, carry_ref):
    @pl.when(pl.program_id(0) == 0)
    def _():
        carry_ref[...] = jnp.zeros_like(carry_ref)

    work = lg_ref[...]
    lane = lax.broadcasted_iota(jnp.int32, work.shape, 1).astype(F32)
    sels, vals, ids = [], [], []
    for _ in range(TOP_K):
        m = jnp.max(work, axis=1, keepdims=True)
        first = jnp.min(jnp.where(work == m, lane, float(LANES)), axis=1, keepdims=True)
        sel = lane == first
        work = jnp.where(sel, -jnp.inf, work)
        sels.append(sel)
        vals.append(m)
        ids.append(first)
    onehot = jnp.where(sels[0] | sels[1] | sels[2] | sels[3], 1.0, 0.0)
    before = jnp.dot(tri_ref[...], onehot.astype(BF16), preferred_element_type=F32) + carry_ref[...]
    carry_ref[...] = carry_ref[...] + jnp.sum(onehot, axis=0, keepdims=True)
    cnt_ref[...] = carry_ref[...]
    exps = [jnp.exp(v - vals[0]) for v in vals]
    denom = exps[0] + exps[1] + exps[2] + exps[3]
    idx_out = jnp.zeros_like(work)
    gate_out = jnp.zeros_like(work)
    for k in range(TOP_K):
        rank = jnp.sum(jnp.where(sels[k], before, 0.0), axis=1, keepdims=True)
        idx_out = jnp.where(lane == k, ids[k], jnp.where(lane == TOP_K + k, rank, idx_out))
        gate_out = jnp.where(lane == k, exps[k] / denom, gate_out)
    idx_ref[...] = idx_out.astype(jnp.int32)
    gate_ref[...] = gate_out


def _route(logits):
    t = logits.shape[0]
    tm = ROUTE_TILE
    tri = jnp.asarray(np.tril(np.ones((tm, tm), np.float32), -1), BF16)
    row = pl.BlockSpec((tm, LANES), lambda i: (i, 0))
    return pl.pallas_call(
        _route_kernel,
        grid=(t // tm,),
        in_specs=[row, pl.BlockSpec((tm, tm), lambda i: (0, 0))],
        out_specs=[row, row, pl.BlockSpec((1, LANES), lambda i: (0, 0))],
        out_shape=[
            jax.ShapeDtypeStruct((t, LANES), jnp.int32),
            jax.ShapeDtypeStruct((t, LANES), F32),
            jax.ShapeDtypeStruct((1, LANES), F32),
        ],
        scratch_shapes=[pltpu.VMEM((1, LANES), F32)],
        compiler_params=_params("arbitrary"),
        name="route",
    )(logits, tri)


def _zero_fill(fill_start_ref, fill_len_ref, n_used_ref, buf_hbm, zero_vmem, sem_fill):
    n_blk = buf_hbm.shape[0] // (EXPERT_ROWS * TOKEN_SUBLANES)
    zero_vmem[...] = jnp.zeros_like(zero_vmem)
    chunks = [1 << b for b in reversed(range(EXPERT_ROWS.bit_length() - 1))]

    def pads(wait):
        def body(e, carry):
            start, length = fill_start_ref[e], fill_len_ref[e]
            for c in chunks:
                @pl.when((length & c) != 0)
                def _():
                    row = pl.multiple_of((start + (length & (-2 * c))) * TOKEN_SUBLANES, TOKEN_SUBLANES)
                    cp = pltpu.make_async_copy(zero_vmem.at[pl.ds(0, c * TOKEN_SUBLANES), :],
                                               buf_hbm.at[pl.ds(row, c * TOKEN_SUBLANES), :], sem_fill)
                    cp.wait() if wait else cp.start()
            return carry

        lax.fori_loop(0, N_EXPERTS, body, 0)

    def tail(wait):
        def body(j, carry):
            row = pl.multiple_of(j * (EXPERT_ROWS * TOKEN_SUBLANES), EXPERT_ROWS * TOKEN_SUBLANES)
            cp = pltpu.make_async_copy(zero_vmem, buf_hbm.at[pl.ds(row, EXPERT_ROWS * TOKEN_SUBLANES), :], sem_fill)
            cp.wait() if wait else cp.start()
            return carry

        lax.fori_loop(n_used_ref[0], n_blk, body, 0)

    pads(False)
    tail(False)
    pads(True)
    tail(True)


def _dispatch_kernel(fill_start_ref, fill_len_ref, n_used_ref, *rest, tiles):
    v_refs = rest[:len(tiles)]
    pos_hbm, buf_hbm, idx_smem, sem_idx, sem_rows, zero_vmem, sem_fill = rest[len(tiles):]
    tm = v_refs[0].shape[0] // TOKEN_SUBLANES
    per_tile = tm * TOP_K
    i = pl.program_id(0)
    fetch = pltpu.make_async_copy(pos_hbm.at[pl.ds(i * per_tile, per_tile)], idx_smem, sem_idx)
    fetch.start()

    @pl.when(i == 0)
    def _():
        _zero_fill(fill_start_ref, fill_len_ref, n_used_ref, buf_hbm, zero_vmem, sem_fill)

    fetch.wait()

    def scatter(v_ref):
        def row_copy(t, p):
            return pltpu.make_async_copy(_token_tile(v_ref, t), _token_tile(buf_hbm, p), sem_rows)

        def issue(j, carry):
            for u in range(DMA_UNROLL):
                t = j * DMA_UNROLL + u
                for k in range(TOP_K):
                    row_copy(t, idx_smem[t * TOP_K + k]).start(priority=k % 2)
            return carry

        def drain(j, carry):
            for _ in range(DMA_UNROLL * TOP_K):
                row_copy(0, 0).wait()
            return carry

        lax.fori_loop(0, tm // DMA_UNROLL, issue, 0)
        lax.fori_loop(0, tm // DMA_UNROLL, drain, 0)

    first = 0
    for v_ref, n_tiles in zip(v_refs, tiles):
        @pl.when((i >= first) & (i < first + n_tiles))
        def _():
            scatter(v_ref)

        first += n_tiles


def _dispatch(fill_start, fill_len, n_used, v_parts, pos_flat, n_rows):
    tm = DISPATCH_TILE
    tiles = tuple(v.shape[0] // (tm * TOKEN_SUBLANES) for v in v_parts)
    firsts = [sum(tiles[:p]) for p in range(len(tiles))]
    any_spec = pl.BlockSpec(memory_space=pl.ANY)

    def part_spec(first, n_tiles):
        return pl.BlockSpec((tm * TOKEN_SUBLANES, LANES), lambda i, *_: (jnp.clip(i - first, 0, n_tiles - 1), 0))

    grid_spec = pltpu.PrefetchScalarGridSpec(
        num_scalar_prefetch=3,
        grid=(sum(tiles),),
        in_specs=[part_spec(f, n) for f, n in zip(firsts, tiles)] + [any_spec],
        out_specs=any_spec,
        scratch_shapes=[pltpu.SMEM((tm * TOP_K,), jnp.int32), pltpu.SemaphoreType.DMA, pltpu.SemaphoreType.DMA,
                        pltpu.VMEM((EXPERT_ROWS * TOKEN_SUBLANES, LANES), F32), pltpu.SemaphoreType.DMA],
    )
    return pl.pallas_call(
        functools.partial(_dispatch_kernel, tiles=tiles),
        grid_spec=grid_spec,
        out_shape=jax.ShapeDtypeStruct((n_rows * TOKEN_SUBLANES, LANES), F32),
        compiler_params=_params("arbitrary"),
        name="dispatch",
    )(fill_start, fill_len, n_used, *v_parts, pos_flat)


def _pair_shuffle():
    l = np.arange(LANES)
    p = np.zeros((LANES, LANES), np.float32)
    p[l, l // 2 + (LANES // 2) * (l % 2)] = 1.0
    return p


def _expert_kernel(blk_e_ref, n_used_ref, run_ref, next_ref, x_ref, bu_ref, bd_ref, perm_ref, wu_hbm, wd_hbm, o_ref,
                   wu_f32, wd_f32, wu_bf, wd_bf, sem_u, sem_d, *, layer):
    i = pl.program_id(0)
    active = i < n_used_ref[0]
    expert = blk_e_ref[i]
    new_expert = (i == 0) | (expert != blk_e_ref[jnp.maximum(i - 1, 0)])

    def fetch(e, slot):
        return (pltpu.make_async_copy(wu_hbm.at[layer, e], wu_f32.at[slot], sem_u.at[slot]),
                pltpu.make_async_copy(wd_hbm.at[layer, e], wd_f32.at[slot], sem_d.at[slot]))

    @pl.when(active & new_expert)
    def _():
        slot = run_ref[expert] % 2

        @pl.when(i == 0)
        def _():
            for cp in fetch(expert, slot):
                cp.start()

        for cp in fetch(expert, slot):
            cp.wait()
        nxt = next_ref[expert]

        @pl.when(nxt >= 0)
        def _():
            for cp in fetch(nxt, 1 - slot):
                cp.start()

        for c in range(D_MODEL // LANES):
            r = c * LANES
            wu_bf[pl.ds(r, LANES), :] = wu_f32[slot, pl.ds(r, LANES), :].astype(BF16)
            wd_bf[pl.ds(r, LANES), :] = jnp.dot(perm_ref[...], wd_f32[slot, pl.ds(r, LANES), :].astype(BF16),
                                               preferred_element_type=F32).astype(BF16)

    @pl.when(active)
    def _():
        h = jnp.dot(_load_token_tiles(x_ref, EXPERT_ROWS).astype(BF16), wu_bf[...], preferred_element_type=F32) + bu_ref[0]
        even = lax.broadcasted_iota(jnp.int32, (EXPERT_ROWS, LANES), 1) % 2 == 0
        acts = []
        for k in range(D_EXPERT // LANES):
            ha = h[:, 2 * k * LANES:(2 * k + 1) * LANES]
            hb = h[:, (2 * k + 1) * LANES:(2 * k + 2) * LANES]
            glu = jnp.where(even, ha, pltpu.roll(hb, 1, axis=1))
            lin = jnp.where(even, pltpu.roll(ha, LANES - 1, axis=1), hb)
            glu = jnp.minimum(glu, SWIGLU_LIMIT)
            lin = jnp.clip(lin, -SWIGLU_LIMIT, SWIGLU_LIMIT)
            acts.append((glu * jax.nn.sigmoid(SWIGLU_ALPHA * glu) * (lin + 1.0)).astype(BF16))
        a = jnp.concatenate(acts, axis=1)
        _store_token_tiles(o_ref, jnp.dot(a, wd_bf[...], preferred_element_type=F32) + bd_ref[0])

    @pl.when(jnp.logical_not(active))
    def _():
        o_ref[...] = jnp.zeros_like(o_ref)


def _experts(blk_e, n_used, run_idx, next_expert, xbuf, layer, wu, bu, wd, bd):
    n_blk = xbuf.shape[0] // (EXPERT_ROWS * TOKEN_SUBLANES)
    grid_spec = pltpu.PrefetchScalarGridSpec(
        num_scalar_prefetch=4,
        grid=(n_blk,),
        in_specs=[
            pl.BlockSpec((EXPERT_ROWS * TOKEN_SUBLANES, LANES), lambda i, e, u, *_: (jnp.minimum(i, u[0] - 1), 0)),
            pl.BlockSpec((1, 1, 2 * D_EXPERT), lambda i, e, *_: (e[i], 0, 0)),
            pl.BlockSpec((1, 1, D_MODEL), lambda i, e, *_: (e[i], 0, 0)),
            pl.BlockSpec((LANES, LANES), lambda i, e, *_: (0, 0)),
            pl.BlockSpec(memory_space=pl.ANY),
            pl.BlockSpec(memory_space=pl.ANY),
        ],
        out_specs=pl.BlockSpec((EXPERT_ROWS * TOKEN_SUBLANES, LANES), lambda i, e, *_: (i, 0)),
        scratch_shapes=[
            pltpu.VMEM((2, D_MODEL, 2 * D_EXPERT), F32),
            pltpu.VMEM((2, D_EXPERT, D_MODEL), F32),
            pltpu.VMEM((D_MODEL, 2 * D_EXPERT), BF16),
            pltpu.VMEM((D_EXPERT, D_MODEL), BF16),
            pltpu.SemaphoreType.DMA((2,)),
            pltpu.SemaphoreType.DMA((2,)),
        ],
    )
    return pl.pallas_call(
        functools.partial(_expert_kernel, layer=layer),
        grid_spec=grid_spec,
        out_shape=jax.ShapeDtypeStruct(xbuf.shape, F32),
        compiler_params=pltpu.CompilerParams(dimension_semantics=("arbitrary",), vmem_limit_bytes=EXPERT_VMEM_LIMIT),
        name="experts",
    )(blk_e, n_used, run_idx, next_expert, xbuf, bu, bd, jnp.asarray(_pair_shuffle(), BF16), wu, wd)


def _combine_kernel(x_ref, gate_ref, g2_ref, lng_ref, lnb_ref, pos_hbm, y_hbm, o_ref, idx_smem, rows, sem_idx, sem_rows):
    tm = x_ref.shape[0]
    per_tile = tm * TOP_K
    i = pl.program_id(0)

    def row_copy(slot, t, k, p):
        return pltpu.make_async_copy(_token_tile(y_hbm, p), _token_tile(rows, (slot * TOP_K + k) * tm + t), sem_rows.at[slot])

    def start_gather(tile, slot):
        fetch = pltpu.make_async_copy(pos_hbm.at[pl.ds(tile * per_tile, per_tile)], idx_smem, sem_idx)
        fetch.start()
        fetch.wait()

        def issue(j, carry):
            for u in range(DMA_UNROLL):
                t = j * DMA_UNROLL + u
                for k in range(TOP_K):
                    row_copy(slot, t, k, idx_smem[t * TOP_K + k]).start(priority=k % 2)
            return carry

        lax.fori_loop(0, tm // DMA_UNROLL, issue, 0)

    @pl.when(i == 0)
    def _():
        start_gather(0, 0)

    @pl.when(i + 1 < pl.num_programs(0))
    def _():
        start_gather(i + 1, (i + 1) % 2)

    slot = i % 2

    def drain(j, carry):
        for _ in range(DMA_UNROLL * TOP_K):
            row_copy(slot, 0, 0, 0).wait()
        return carry

    lax.fori_loop(0, tm // DMA_UNROLL, drain, 0)
    chunks = []
    for c in range(TOKEN_SUBLANES):
        def part(k):
            first = pl.multiple_of((slot * TOP_K + k) * (tm * TOKEN_SUBLANES), tm * TOKEN_SUBLANES) + c
            return gate_ref[:, k:k + 1] * rows[pl.ds(first, tm, stride=TOKEN_SUBLANES), :]

        fc = part(0)
        for k in range(1, TOP_K):
            fc = fc + part(k)
        chunks.append(fc)
    f = jnp.concatenate(chunks, axis=1)
    o_ref[...] = _layer_norm(DEEPNORM_ALPHA * x_ref[...] + g2_ref[0] * f, lng_ref[...], lnb_ref[...])


def _combine(x2, gates, pos_flat, ybuf, g2, lng, lnb, seq_rows, mod_row):
    t = x2.shape[0]
    tm = COMBINE_TILE
    row = pl.BlockSpec((tm, D_MODEL), lambda i: (i, 0))
    const = pl.BlockSpec((1, D_MODEL), lambda i: (0, 0))
    return pl.pallas_call(
        _combine_kernel,
        grid=(t // tm,),
        in_specs=[row, pl.BlockSpec((tm, LANES), lambda i: (i, 0)), _mod_spec(tm, seq_rows, mod_row), const, const,
                  pl.BlockSpec(memory_space=pl.ANY), pl.BlockSpec(memory_space=pl.ANY)],
        out_specs=row,
        out_shape=jax.ShapeDtypeStruct((t, D_MODEL), F32),
        scratch_shapes=[
            pltpu.SMEM((tm * TOP_K,), jnp.int32),
            pltpu.VMEM((2 * TOP_K * tm * TOKEN_SUBLANES, LANES), F32),
            pltpu.SemaphoreType.DMA,
            pltpu.SemaphoreType.DMA((2,)),
        ],
        compiler_params=_params("arbitrary"),
        name="combine",
    )(x2, gates, g2, lng, lnb, pos_flat, ybuf)


def _moe(v_parts, logits, layer, wu, bu, wd, bd):
    t = logits.shape[0]
    n = t * TOP_K
    idx, gates, cnt = _route(logits)
    counts = cnt[0, :N_EXPERTS].astype(jnp.int32)
    padded = (counts + EXPERT_ROWS - 1) // EXPERT_ROWS * EXPERT_ROWS
    pad_ends = jnp.cumsum(padded)
    pad_starts = pad_ends - padded
    experts = jnp.arange(N_EXPERTS, dtype=jnp.int32)
    start = jnp.sum(jnp.where(idx[:, :TOP_K, None] == experts, pad_starts, 0), axis=-1)
    pos_flat = (start + idx[:, TOP_K:2 * TOP_K]).reshape(n)
    n_blk = n // EXPERT_ROWS + N_EXPERTS
    blk_start = jnp.arange(n_blk, dtype=jnp.int32) * EXPERT_ROWS
    blk_e = jnp.minimum(jnp.sum((pad_ends[None, :] <= blk_start[:, None]).astype(jnp.int32), axis=1), N_EXPERTS - 1)
    n_used = pad_ends[-1:] // EXPERT_ROWS
    nonempty = counts > 0
    run_idx = jnp.cumsum(nonempty.astype(jnp.int32)) - 1
    later = nonempty[None, :] & (experts[None, :] > experts[:, None])
    next_expert = jnp.min(jnp.where(later, experts[None, :], N_EXPERTS), axis=1)
    next_expert = jnp.where(next_expert == N_EXPERTS, -1, next_expert)
    fill_start, fill_len = pad_starts + counts, padded - counts
    xbuf = _dispatch(fill_start, fill_len, n_used, v_parts, pos_flat, n_blk * EXPERT_ROWS)
    ybuf = _experts(blk_e, n_used, run_idx, next_expert, xbuf, layer, wu, bu, wd, bd)
    return ybuf, pos_flat, gates


def _rope_tables(seq):
    rows = seq // GRID_W
    row = jnp.repeat(jnp.arange(rows), GRID_W).astype(F32)
    col = jnp.tile(jnp.arange(GRID_W), rows).astype(F32)
    n_freq = HEAD_DIM // 4
    inv = ROPE_BASE ** (-jnp.arange(n_freq, dtype=F32) / n_freq)
    ang = jnp.concatenate([row[:, None] * inv, col[:, None] * inv], -1)
    ang = jnp.concatenate([ang, ang], -1)
    sign = jnp.where(jnp.arange(HEAD_DIM) < HEAD_DIM // 2, -1.0, 1.0).astype(F32)
    cos = jnp.cos(ang)
    sin = jnp.sin(ang) * sign
    return jnp.tile(cos, (1, LANES // HEAD_DIM)), jnp.tile(sin, (1, LANES // HEAD_DIM))


def kernel(x, c, ctx, c_ctx, w_mod, b_mod, w_in_even, sink, w_out_even, w_in_odd, conv_w, w_out_odd, ln_g, ln_b,
           w_router, b_router, w_up, b_up, w_down, b_down):
    B, S, D = x.shape
    L = ctx.shape[1]
    T, TC = B * S, B * L
    cos, sin = _rope_tables(S)
    cond = jnp.concatenate([c, c_ctx[None, :], jnp.zeros((MOD_ROWS - B - 1, D), F32)], 0)
    mod = _modulation(cond, w_mod, b_mod)

    bu = b_up.reshape(DEPTH, N_EXPERTS, 1, 2 * D_EXPERT)
    bd = b_down.reshape(DEPTH, N_EXPERTS, 1, D)
    wr_hi = w_router.astype(BF16)
    wr_lo = (w_router - wr_hi.astype(F32)).astype(BF16)
    pad_r = ((0, 0), (0, 0), (0, LANES - N_EXPERTS))
    wr_hi, wr_lo = jnp.pad(wr_hi, pad_r), jnp.pad(wr_lo, pad_r)
    br = jnp.pad(b_router, ((0, 0), (0, LANES - N_EXPERTS)), constant_values=NEG_INF).reshape(DEPTH, 1, LANES)

    x2 = x.reshape(T, D)
    xc2 = ctx.reshape(TC, D)
    for l in range(DEPTH):
        even = l % 2 == 0
        j = l // 2
        ctx_after = any(m % 2 == 0 for m in range(l + 1, DEPTH))
        sh1, sc1, g1, sh2, sc2, g2 = [m.reshape(MOD_ROWS, 1, D) for m in jnp.split(mod[l], 6, axis=-1)]
        lng1, lnb1 = ln_g[l, 0].reshape(1, D), ln_b[l, 0].reshape(1, D)
        lng2, lnb2 = ln_g[l, 1].reshape(1, D), ln_b[l, 1].reshape(1, D)
        post = (g1, sc2, sh2, lng1, lnb1, wr_hi[l], wr_lo[l], br[l])
        if even:
            w_in = w_in_even[j].astype(BF16)
            w_out = w_out_even[j].astype(BF16)
            h = _inproj_even(x2, sc1, sh1, w_in, cos, sin, S, None, True)
            hc = _inproj_even(xc2, sc1, sh1, w_in, cos, sin, L, B, False)
            kcol = ATTN_WIDTH // KV_WIDTH
            attn = _win_attention(h, hc, sink[j], B, S, kcol, kcol + 1)
            four = _fourier_mix(h, B, S)
            x2, v, logits = _outproj("even", (attn, four), w_out, x2, *post, S, None)
            if ctx_after:
                attn_c = _ctx_attention(hc, sink[j], B)
                four_c = _fourier_mix(hc, B, L)
                xc2, vc, logits_c = _outproj("even", (attn_c, four_c), w_out, xc2, *post, L, B)
        else:
            w_in = w_in_odd[j].astype(BF16)
            w_out = w_out_odd[j].astype(BF16)
            cw = jnp.pad(conv_w[j], ((0, 8 - conv_w.shape[1]), (0, 0)))
            bz = _inproj_odd(x2, sc1, sh1, w_in, S, None)
            x2, v, logits = _outproj("odd", (bz,), w_out, x2, *post, S, None, conv_w=cw)
            if ctx_after:
                bzc = _inproj_odd(xc2, sc1, sh1, w_in, L, B)
                xc2, vc, logits_c = _outproj("odd", (bzc,), w_out, xc2, *post, L, B, conv_w=cw)
        if ctx_after:
            ybuf, pos, gates = _moe((v, vc), jnp.concatenate([logits, logits_c], 0), l, w_up, bu[l], w_down, bd[l])
            xc2 = _combine(xc2, gates[T:], pos[T * TOP_K:], ybuf, g2, lng2, lnb2, L, B)
        else:
            ybuf, pos, gates = _moe((v,), logits, l, w_up, bu[l], w_down, bd[l])
        x2 = _combine(x2, gates[:T], pos[:T * TOP_K], ybuf, g2, lng2, lnb2, S, None)
    return x2.reshape(B, S, D)
```
